```python
import math
import jax
import jax.numpy as jnp
from jax import lax
import numpy as np

D_MODEL = 2048
BATCH = 8
SEQ = 2048
DEPTH = 4

N_META = 16
N_A_LAYERS = DEPTH // 2
N_B_LAYERS = DEPTH - N_A_LAYERS
N_DENSE = (DEPTH + 1) // 2
N_MOE = DEPTH // 2
DN_ALPHA = (2 * DEPTH) ** 0.25
DN_BETA = (8 * DEPTH) ** -0.25
LN_EPS = 1e-5

GLA_HEADS = 4
GLA_DK = D_MODEL // 2
GLA_DV = D_MODEL
GLA_DK_HEAD = GLA_DK // GLA_HEADS
GLA_DV_HEAD = GLA_DV // GLA_HEADS
GLA_GATE_RANK = 16
GLA_GATE_TAU = 16.0
GLA_CHUNK = 64
GLA_IN = 2 * GLA_DK + 2 * GLA_DV + GLA_GATE_RANK

SWA_HEAD_DIM = 64
SWA_Q_HEADS = D_MODEL // SWA_HEAD_DIM
SWA_GROUP = 8
SWA_KV_HEADS = SWA_Q_HEADS // SWA_GROUP
SWA_WINDOW = 128
SWA_BLOCK = 128
SWA_KEYS = N_META + 2 * SWA_BLOCK

REL_BUCKETS = 32
REL_MAX_DIST = 128

FFN_DIM = 7 * D_MODEL // 2
N_EXPERTS = 8
TOP_K = 2
MOE_BLOCK = 512

NEG_INF = -1e9

kernel_name = 'yoco_gla_swa_sink_moe_deepnorm'


def layer_norm(x, g, b):
    xf = x.astype(jnp.float32)
    mu = jnp.mean(xf, axis=-1, keepdims=True)
    var = jnp.mean(jnp.square(xf - mu), axis=-1, keepdims=True)
    y = (xf - mu) * lax.rsqrt(var + LN_EPS)
    return (y * g.astype(jnp.float32) + b.astype(jnp.float32)).astype(x.dtype)


def swiglu(x, w_gu, w_down):
    a, u = jnp.split(x @ w_gu, 2, axis=-1)
    return (jax.nn.silu(a) * u) @ w_down


def gla_mixer(x, w_in, w_gate2, b_gate, norm_gain, w_out):
    bsz, L, _ = x.shape
    H, dk, dv, C = GLA_HEADS, GLA_DK_HEAD, GLA_DV_HEAD, GLA_CHUNK
    offs = [GLA_DK, 2 * GLA_DK, 2 * GLA_DK + GLA_DV, 2 * GLA_DK + 2 * GLA_DV]
    q, k, v, r, g_low = jnp.split(x @ w_in, offs, axis=-1)
    log_a = jax.nn.log_sigmoid((g_low @ w_gate2 + b_gate).astype(jnp.float32)) / GLA_GATE_TAU
    pad = C - N_META

    def to_chunks(t, d):
        t = jnp.pad(t.astype(jnp.float32), ((0, 0), (pad, 0), (0, 0)))
        n = t.shape[1] // C
        return t.reshape(bsz, n, C, H, d).transpose(0, 3, 1, 2, 4)

    qc = to_chunks(q, dk) * (dk ** -0.5)
    kc = to_chunks(k, dk)
    vc = to_chunks(v, dv)
    bcum = jnp.cumsum(to_chunks(log_a, dk), axis=3)
    q_dec = qc * jnp.exp(bcum)
    k_inv = kc * jnp.exp(-bcum)
    causal = jnp.tril(jnp.ones((C, C), dtype=bool))
    att = jnp.where(causal, jnp.einsum('bhncd,bhnsd->bhncs', q_dec, k_inv), 0.0)
    o_intra = jnp.einsum('bhncs,bhnsv->bhncv', att, vc)
    b_last = bcum[:, :, :, -1:, :]
    k_state = kc * jnp.exp(b_last - bcum)
    decay = jnp.exp(b_last[:, :, :, 0, :])

    def step(S, inp):
        qd, ks, vv, dec = inp
        o = jnp.einsum('bhcd,bhdv->bhcv', qd, S)
        S = dec[..., None] * S + jnp.einsum('bhcd,bhcv->bhdv', ks, vv)
        return S, o

    S0 = jnp.zeros((bsz, H, dk, dv), jnp.float32)
    xs = (jnp.moveaxis(q_dec, 2, 0), jnp.moveaxis(k_state, 2, 0),
          jnp.moveaxis(vc, 2, 0), jnp.moveaxis(decay, 2, 0))
    _, o_inter = lax.scan(step, S0, xs)
    o = o_intra + jnp.moveaxis(o_inter, 0, 2)
    o = o.transpose(0, 2, 3, 1, 4).reshape(bsz, -1, H, dv)[:, pad:]
    mu = jnp.mean(o, axis=-1, keepdims=True)
    var = jnp.mean(jnp.square(o - mu), axis=-1, keepdims=True)
    o = ((o - mu) * lax.rsqrt(var + LN_EPS)).reshape(bsz, L, GLA_DV) * norm_gain.astype(jnp.float32)
    return (o.astype(x.dtype) * jax.nn.silu(r)) @ w_out


def t5_bucket(dist):
    exact = REL_BUCKETS // 2
    d = jnp.maximum(dist, 0)
    df = jnp.maximum(d, 1).astype(jnp.float32)
    large = exact + (jnp.log(df / exact) / math.log(REL_MAX_DIST / exact)
                     * (REL_BUCKETS - exact)).astype(jnp.int32)
    large = jnp.minimum(large, REL_BUCKETS - 1)
    return jnp.where(d < exact, d, large)


def relative_bias(table, n_blocks):
    tab = table.astype(jnp.float32)
    qpos = N_META + jnp.arange(n_blocks * SWA_BLOCK, dtype=jnp.int32).reshape(n_blocks, SWA_BLOCK)
    blk = jnp.arange(n_blocks, dtype=jnp.int32)[:, None]
    j = jnp.arange(SWA_BLOCK, dtype=jnp.int32)[None, :]
    kpos = jnp.concatenate([
        jnp.broadcast_to(jnp.arange(N_META, dtype=jnp.int32)[None, :], (n_blocks, N_META)),
        N_META + (blk - 1) * SWA_BLOCK + j,
        N_META + blk * SWA_BLOCK + j], axis=1)
    is_meta = jnp.arange(SWA_KEYS) < N_META
    dist = qpos[:, :, None] - kpos[:, None, :]
    valid = is_meta[None, None, :] | ((dist >= 0) & (dist < SWA_WINDOW) & (kpos[:, None, :] >= N_META))
    band = jnp.where(valid[..., None], tab[t5_bucket(dist)], NEG_INF).transpose(0, 3, 1, 2)
    mdist = jnp.arange(N_META)[:, None] - jnp.arange(N_META)[None, :]
    meta = jnp.where((mdist >= 0)[..., None], tab[t5_bucket(mdist)], NEG_INF).transpose(2, 0, 1)
    return band, meta


def shared_kv(h, w_kv, n_blocks):
    bsz, L, _ = h.shape
    kv = (h @ w_kv).reshape(bsz, L, 2, SWA_KV_HEADS, SWA_HEAD_DIM)
    k, v = kv[:, :, 0], kv[:, :, 1]

    def band(t):
        meta = t[:, :N_META]
        real = t[:, N_META:].reshape(bsz, n_blocks, SWA_BLOCK, SWA_KV_HEADS, SWA_HEAD_DIM)
        prev = jnp.concatenate([jnp.zeros_like(real[:, :1]), real[:, :-1]], axis=1)
        meta_b = jnp.broadcast_to(meta[:, None], (bsz, n_blocks, N_META, SWA_KV_HEADS, SWA_HEAD_DIM))
        return jnp.concatenate([meta_b, prev, real], axis=2)

    return k[:, :N_META], v[:, :N_META], band(k), band(v)


def swa_mixer(x, k_meta, v_meta, k_band, v_band, band_bias, meta_bias, w_q, sinks, w_out):
    bsz, L, _ = x.shape
    n_blocks = k_band.shape[1]
    KV, G, hd = SWA_KV_HEADS, SWA_GROUP, SWA_HEAD_DIM
    q = (x @ w_q).reshape(bsz, L, KV, G, hd) * (hd ** -0.5)
    sink = sinks.astype(jnp.float32).reshape(KV, G)

    def attend(qb, kb, vb, bias):
        nq, ns = qb.shape[1], kb.shape[1]
        s = jnp.einsum('bqkgd,bskd->bkgqs', qb, kb).astype(jnp.float32) + bias.reshape(KV, G, nq, ns)
        sink_col = jnp.broadcast_to(sink[None, :, :, None, None], s.shape[:-1] + (1,))
        p = jax.nn.softmax(jnp.concatenate([s, sink_col], axis=-1), axis=-1)[..., :-1]
        return jnp.einsum('bkgqs,bskd->bqkgd', p.astype(vb.dtype), vb)

    o_meta = attend(q[:, :N_META], k_meta, v_meta, meta_bias)
    q_real = q[:, N_META:].reshape(bsz, n_blocks, SWA_BLOCK, KV, G, hd)
    o_real = lax.map(lambda a: attend(a[0], a[1], a[2], a[3]),
                     (jnp.moveaxis(q_real, 1, 0), jnp.moveaxis(k_band, 1, 0),
                      jnp.moveaxis(v_band, 1, 0), band_bias))
    o_real = jnp.moveaxis(o_real, 0, 1).reshape(bsz, n_blocks * SWA_BLOCK, KV, G, hd)
    o = jnp.concatenate([o_meta, o_real], axis=1).reshape(bsz, L, SWA_Q_HEADS * hd)
    return o @ w_out


def moe_swiglu(x2d, w_router, w_gu, w_down):
    T, d = x2d.shape
    logits = (x2d @ w_router).astype(jnp.float32)
    top_val, top_idx = lax.top_k(logits, TOP_K)
    gates = jax.nn.softmax(top_val, axis=-1).astype(x2d.dtype)
    flat_e = top_idx.reshape(-1)
    flat_tok = jnp.repeat(jnp.arange(T, dtype=jnp.int32), TOP_K)
    flat_g = gates.reshape(-1)
    order = jnp.argsort(flat_e)
    s_e, s_tok, s_g = flat_e[order], flat_tok[order], flat_g[order]
    counts = jnp.bincount(flat_e, length=N_EXPERTS)
    padded = ((counts + MOE_BLOCK - 1) // MOE_BLOCK) * MOE_BLOCK
    start_sorted = jnp.cumsum(counts) - counts
    padded_end = jnp.cumsum(padded)
    start_padded = padded_end - padded
    dest = start_padded[s_e] + (jnp.arange(T * TOP_K, dtype=jnp.int32) - start_sorted[s_e])
    n_blk = -(-(T * TOP_K) // MOE_BLOCK) + N_EXPERTS
    P = n_blk * MOE_BLOCK
    buf_tok = jnp.zeros((P,), jnp.int32).at[dest].set(s_tok)
    buf_gate = jnp.zeros((P,), x2d.dtype).at[dest].set(s_g)
    blk_e = jnp.minimum(jnp.searchsorted(padded_end, jnp.arange(n_blk) * MOE_BLOCK, side='right'),
                        N_EXPERTS - 1).astype(jnp.int32)
    x_blk = x2d[buf_tok].reshape(n_blk, MOE_BLOCK, d)

    def expert_block(a):
        xb, e = a
        return swiglu(xb, w_gu[e], w_down[e])

    y = lax.map(expert_block, (x_blk, blk_e)).reshape(P, d) * buf_gate[:, None]
    return jnp.zeros((T, d), x2d.dtype).at[buf_tok].add(y)


def setup_inputs(seed: int = 0) -> dict:
    key = jax.random.key(seed)
    ks = jax.random.split(key, 20)
    f32 = jnp.float32
    D = D_MODEL

    def nrm(k, shape, scale):
        return jax.random.normal(k, shape, f32) * scale

    return {
        'x': nrm(ks[0], (BATCH, SEQ, D), 1.0),
        'meta_tokens': nrm(ks[1], (N_META, D), 1.0),
        'rel_bias_table': nrm(ks[2], (REL_BUCKETS, SWA_Q_HEADS), 0.5),
        'ln_gain': 1.0 + nrm(ks[3], (DEPTH, 2, D), 0.02),
        'ln_bias': nrm(ks[4], (DEPTH, 2, D), 0.02),
        'gla_w_in': nrm(ks[5], (N_A_LAYERS, D, GLA_IN), D ** -0.5),
        'gla_w_gate2': nrm(ks[6], (N_A_LAYERS, GLA_GATE_RANK, GLA_DK), GLA_GATE_RANK ** -0.5),
        'gla_b_gate': nrm(ks[7], (N_A_LAYERS, GLA_DK), 0.1),
        'gla_norm_gain': 1.0 + nrm(ks[8], (N_A_LAYERS, GLA_DV), 0.02),
        'gla_w_out': nrm(ks[9], (N_A_LAYERS, GLA_DV, D), DN_BETA * GLA_DV ** -0.5),
        'kv_w_shared': nrm(ks[10], (D, 2 * SWA_KV_HEADS * SWA_HEAD_DIM), D ** -0.5),
        'swa_w_q': nrm(ks[11], (N_B_LAYERS, D, SWA_Q_HEADS * SWA_HEAD_DIM), D ** -0.5),
        'swa_sinks': nrm(ks[12], (N_B_LAYERS, SWA_Q_HEADS), 1.0),
        'swa_w_out': nrm(ks[13], (N_B_LAYERS, SWA_Q_HEADS * SWA_HEAD_DIM, D), DN_BETA * D ** -0.5),
        'ffn_w_gate_up': nrm(ks[14], (N_DENSE, D, 2 * FFN_DIM), D ** -0.5),
        'ffn_w_down': nrm(ks[15], (N_DENSE, FFN_DIM, D), DN_BETA * FFN_DIM ** -0.5),
        'moe_w_router': nrm(ks[16], (N_MOE, D, N_EXPERTS), D ** -0.5),
        'moe_w_gate_up': nrm(ks[17], (N_MOE, N_EXPERTS, D, 2 * FFN_DIM), D ** -0.5),
        'moe_w_down': nrm(ks[18], (N_MOE, N_EXPERTS, FFN_DIM, D), DN_BETA * FFN_DIM ** -0.5),
    }


def reference(x, meta_tokens, rel_bias_table, ln_gain, ln_bias, gla_w_in, gla_w_gate2, gla_b_gate,
              gla_norm_gain, gla_w_out, kv_w_shared, swa_w_q, swa_sinks, swa_w_out,
              ffn_w_gate_up, ffn_w_down, moe_w_router, moe_w_gate_up, moe_w_down):
    bsz, seq, d = x.shape
    n_blocks = seq // SWA_BLOCK
    h = jnp.concatenate([jnp.broadcast_to(meta_tokens.astype(x.dtype)[None], (bsz, N_META, d)), x], axis=1)
    band_bias, meta_bias = relative_bias(rel_bias_table, n_blocks)
    shared = None
    for li in range(DEPTH):
        if li < N_A_LAYERS:
            mix = gla_mixer(h, gla_w_in[li], gla_w_gate2[li], gla_b_gate[li], gla_norm_gain[li], gla_w_out[li])
        else:
            jb = li - N_A_LAYERS
            k_meta, v_meta, k_band, v_band = shared
            mix = swa_mixer(h, k_meta, v_meta, k_band, v_band, band_bias, meta_bias,
                            swa_w_q[jb], swa_sinks[jb], swa_w_out[jb])
        h = layer_norm(DN_ALPHA * h + mix, ln_gain[li, 0], ln_bias[li, 0])
        if li % 2 == 0:
            f = swiglu(h, ffn_w_gate_up[li // 2], ffn_w_down[li // 2])
        else:
            f = moe_swiglu(h.reshape(-1, d), moe_w_router[li // 2], moe_w_gate_up[li // 2],
                           moe_w_down[li // 2]).reshape(h.shape)
        h = layer_norm(DN_ALPHA * h + f, ln_gain[li, 1], ln_bias[li, 1])
        if li == N_A_LAYERS - 1:
            shared = shared_kv(h, kv_w_shared, n_blocks)
    return h[:, N_META:]
```

```python
import functools
import math

import jax
import jax.numpy as jnp
from jax import lax
from jax.experimental import pallas as pl
from jax.experimental.pallas import tpu as pltpu

F32 = jnp.float32
BF16 = jnp.bfloat16

D_MODEL = 2048
DEPTH = 4
N_META = 16
N_A_LAYERS = DEPTH // 2
DN_ALPHA = (2 * DEPTH) ** 0.25
LN_EPS = 1e-5

GLA_HEADS = 4
GLA_DK = D_MODEL // 2
GLA_DV = D_MODEL
GLA_DK_HEAD = GLA_DK // GLA_HEADS
GLA_DV_HEAD = GLA_DV // GLA_HEADS
GLA_GATE_RANK = 16
GLA_GATE_TAU = 16.0
GLA_CHUNK = 64
GLA_MAIN = 2 * GLA_DK + 2 * GLA_DV

SWA_HEAD_DIM = 64
SWA_Q_HEADS = D_MODEL // SWA_HEAD_DIM
SWA_GROUP = 8
SWA_KV_HEADS = SWA_Q_HEADS // SWA_GROUP
SWA_WINDOW = 128
SWA_BLOCK = 128

REL_BUCKETS = 32
REL_MAX_DIST = 128

FFN_DIM = 7 * D_MODEL // 2
N_EXPERTS = 8
TOP_K = 2
NEG_INF = -1e9

LANES = 128
VMEM_LIMIT = 56 * 1024 * 1024
ROW_TILE = 512
FFN_TILE = 512
MOE_ROW_TILE = 512
GLA_ROWS = 256


def _params(*sem):
    return pltpu.CompilerParams(dimension_semantics=sem, vmem_limit_bytes=VMEM_LIMIT)


def _layer_norm_rows(y, g, b):
    mu = jnp.mean(y, axis=-1, keepdims=True)
    yc = y - mu
    var = jnp.mean(yc * yc, axis=-1, keepdims=True)
    return yc * lax.rsqrt(var + LN_EPS) * g + b


def _dot(a, b):
    return jnp.dot(a, b, preferred_element_type=F32)


def _dot_nt(a, b):
    return lax.dot_general(a, b, (((1,), (1,)), ((), ())), preferred_element_type=F32)


def _dot_tn(a, b, precision=None):
    return lax.dot_general(a, b, (((0,), (0,)), ((), ())), preferred_element_type=F32,
                           precision=precision)


def _linear_kernel(x_ref, w_ref, o_ref, *, scale):
    acc = _dot(x_ref[...], w_ref[...])
    if scale != 1.0:
        acc = acc * scale
    o_ref[...] = acc.astype(o_ref.dtype)


def linear(x, w, n_out, out_dtype, tn, scale=1.0):
    rows, k = x.shape
    return pl.pallas_call(
        functools.partial(_linear_kernel, scale=scale),
        grid=(pl.cdiv(rows, ROW_TILE), n_out // tn),
        in_specs=[pl.BlockSpec((ROW_TILE, k), lambda i, j: (i, 0)),
                  pl.BlockSpec((k, tn), lambda i, j: (0, j))],
        out_specs=pl.BlockSpec((ROW_TILE, tn), lambda i, j: (i, j)),
        out_shape=jax.ShapeDtypeStruct((rows, n_out), out_dtype),
        compiler_params=_params("parallel", "arbitrary"),
        name="linear",
    )(x, w)


def _heads_linear_kernel(x_ref, w_ref, o_ref):
    acc = _dot(x_ref[...], w_ref[...])
    for c in range(o_ref.shape[0]):
        o_ref[c] = acc[:, c * SWA_HEAD_DIM:(c + 1) * SWA_HEAD_DIM].astype(o_ref.dtype)


def heads_linear(x, w):
    rows, k = x.shape
    n = w.shape[1] // SWA_HEAD_DIM
    return pl.pallas_call(
        _heads_linear_kernel,
        grid=(pl.cdiv(rows, ROW_TILE),),
        in_specs=[pl.BlockSpec((ROW_TILE, k), lambda i: (i, 0)),
                  pl.BlockSpec((k, w.shape[1]), lambda i: (0, 0))],
        out_specs=pl.BlockSpec((n, ROW_TILE, SWA_HEAD_DIM), lambda i: (0, i, 0)),
        out_shape=jax.ShapeDtypeStruct((n, rows, SWA_HEAD_DIM), BF16),
        compiler_params=_params("parallel"),
        name="heads_linear",
    )(x, w)


def _gla_gate_kernel(x_ref, wg_ref, w2_ref, b_ref, o_ref):
    g_low = _dot(x_ref[...], wg_ref[...])
    z = _dot(g_low.astype(BF16), w2_ref[...]) + b_ref[...]
    log_sig = jnp.minimum(z, 0.0) - jnp.log1p(jnp.exp(-jnp.abs(z)))
    o_ref[...] = log_sig / GLA_GATE_TAU


def gla_gate(x, wg_pad, w2_pad, b_gate):
    rows, k = x.shape
    return pl.pallas_call(
        _gla_gate_kernel,
        grid=(pl.cdiv(rows, ROW_TILE),),
        in_specs=[pl.BlockSpec((ROW_TILE, k), lambda i: (i, 0)),
                  pl.BlockSpec((k, LANES), lambda i: (0, 0)),
                  pl.BlockSpec((LANES, GLA_DK), lambda i: (0, 0)),
                  pl.BlockSpec((1, GLA_DK), lambda i: (0, 0))],
        out_specs=pl.BlockSpec((ROW_TILE, GLA_DK), lambda i: (i, 0)),
        out_shape=jax.ShapeDtypeStruct((rows, GLA_DK), F32),
        compiler_params=_params("parallel"),
        name="gla_gate",
    )(x, wg_pad, w2_pad, b_gate)


def _linear_res_ln_kernel(x_ref, w_ref, res_ref, g_ref, b_ref, o_ref, ob_ref):
    y = DN_ALPHA * res_ref[...] + _dot(x_ref[...], w_ref[...])
    o = _layer_norm_rows(y, g_ref[...], b_ref[...])
    o_ref[...] = o
    ob_ref[...] = o.astype(BF16)


def linear_res_ln(x, w, res, g, b):
    rows, k = x.shape
    d = w.shape[1]
    return pl.pallas_call(
        _linear_res_ln_kernel,
        grid=(pl.cdiv(rows, ROW_TILE),),
        in_specs=[pl.BlockSpec((ROW_TILE, k), lambda i: (i, 0)),
                  pl.BlockSpec((k, d), lambda i: (0, 0)),
                  pl.BlockSpec((ROW_TILE, d), lambda i: (i, 0)),
                  pl.BlockSpec((1, d), lambda i: (0, 0)),
                  pl.BlockSpec((1, d), lambda i: (0, 0))],
        out_specs=[pl.BlockSpec((ROW_TILE, d), lambda i: (i, 0)),
                   pl.BlockSpec((ROW_TILE, d), lambda i: (i, 0))],
        out_shape=[jax.ShapeDtypeStruct((rows, d), F32), jax.ShapeDtypeStruct((rows, d), BF16)],
        compiler_params=_params("parallel"),
        name="linear_res_ln",
    )(x, w, res, g, b)


def _swiglu_partial(x, wa, wu, wd):
    a = _dot(x, wa)
    u = _dot(x, wu)
    hidden = (a * jax.nn.sigmoid(a) * u).astype(BF16)
    return _dot(hidden, wd)


def _ffn_kernel(x_ref, wa_ref, wu_ref, wd_ref, res_ref, g_ref, b_ref, o_ref, ob_ref, acc_ref):
    j = pl.program_id(1)
    part = _swiglu_partial(x_ref[...], wa_ref[...], wu_ref[...], wd_ref[...])

    @pl.when(j == 0)
    def _():
        acc_ref[...] = part

    @pl.when(j > 0)
    def _():
        acc_ref[...] += part

    @pl.when(j == pl.num_programs(1) - 1)
    def _():
        o = _layer_norm_rows(DN_ALPHA * res_ref[...] + acc_ref[...], g_ref[...], b_ref[...])
        o_ref[...] = o
        ob_ref[...] = o.astype(BF16)


def ffn_res_ln(xb, res, w_gu, w_down, g, b):
    rows, d = xb.shape
    f = w_down.shape[0]
    nj = f // FFN_TILE
    return pl.pallas_call(
        _ffn_kernel,
        grid=(pl.cdiv(rows, ROW_TILE), nj),
        in_specs=[pl.BlockSpec((ROW_TILE, d), lambda i, j: (i, 0)),
                  pl.BlockSpec((d, FFN_TILE), lambda i, j: (0, j)),
                  pl.BlockSpec((d, FFN_TILE), lambda i, j: (0, nj + j)),
                  pl.BlockSpec((FFN_TILE, d), lambda i, j: (j, 0)),
                  pl.BlockSpec((ROW_TILE, d), lambda i, j: (i, 0)),
                  pl.BlockSpec((1, d), lambda i, j: (0, 0)),
                  pl.BlockSpec((1, d), lambda i, j: (0, 0))],
        out_specs=[pl.BlockSpec((ROW_TILE, d), lambda i, j: (i, 0)),
                   pl.BlockSpec((ROW_TILE, d), lambda i, j: (i, 0))],
        out_shape=[jax.ShapeDtypeStruct((rows, d), F32), jax.ShapeDtypeStruct((rows, d), BF16)],
        scratch_shapes=[pltpu.VMEM((ROW_TILE, d), F32)],
        compiler_params=_params("parallel", "arbitrary"),
        name="ffn_res_ln",
    )(xb, w_gu, w_gu, w_down, res, g, b)


def _router_kernel(x_ref, w_ref, o_ref):
    logits = _dot(x_ref[...], w_ref[...])
    lane = lax.broadcasted_iota(jnp.int32, logits.shape, 1)
    logits = jnp.where(lane < N_EXPERTS, logits, -jnp.inf)
    m1 = jnp.max(logits, axis=-1, keepdims=True)
    i1 = jnp.min(jnp.where(logits == m1, lane, LANES), axis=-1, keepdims=True)
    rest = jnp.where(lane == i1, -jnp.inf, logits)
    m2 = jnp.max(rest, axis=-1, keepdims=True)
    i2 = jnp.min(jnp.where(rest == m2, lane, LANES), axis=-1, keepdims=True)
    e2 = jnp.exp(m2 - m1)
    g1 = 1.0 / (1.0 + e2)
    g2 = e2 / (1.0 + e2)
    out = jnp.where(lane == 0, i1.astype(F32),
                    jnp.where(lane == 1, i2.astype(F32),
                              jnp.where(lane == 2, g1, jnp.where(lane == 3, g2, 0.0))))
    o_ref[...] = out


def router(xb, w_router_pad):
    rows, d = xb.shape
    return pl.pallas_call(
        _router_kernel,
        grid=(pl.cdiv(rows, ROW_TILE),),
        in_specs=[pl.BlockSpec((ROW_TILE, d), lambda i: (i, 0)),
                  pl.BlockSpec((d, LANES), lambda i: (0, 0))],
        out_specs=pl.BlockSpec((ROW_TILE, LANES), lambda i: (i, 0)),
        out_shape=jax.ShapeDtypeStruct((rows, LANES), F32),
        compiler_params=_params("parallel"),
        name="router",
    )(xb, w_router_pad)


def _moe_kernel(blk_e_ref, n_act_ref, x_ref, wa_ref, wu_ref, wd_ref, gate_ref, o_ref):
    del blk_e_ref
    i = pl.program_id(0)
    j = pl.program_id(1)
    active = i < n_act_ref[0]

    @pl.when(active)
    def _():
        part = _swiglu_partial(x_ref[...], wa_ref[...], wu_ref[...], wd_ref[...])

        @pl.when(j == 0)
        def _():
            o_ref[...] = part

        @pl.when(j > 0)
        def _():
            o_ref[...] += part

        @pl.when(j == pl.num_programs(1) - 1)
        def _():
            o_ref[...] = o_ref[...] * gate_ref[...]

    @pl.when(jnp.logical_and(jnp.logical_not(active), j == 0))
    def _():
        o_ref[...] = jnp.zeros_like(o_ref)


def moe_experts(x_blk, gate_blk, blk_e, n_active, w_gu, w_down):
    p_rows, d = x_blk.shape
    f = w_down.shape[1]
    nj = f // FFN_TILE
    n_blk = p_rows // MOE_ROW_TILE

    def jj(i, j, n_act):
        return jnp.where(i < n_act[0], j, nj - 1)

    grid_spec = pltpu.PrefetchScalarGridSpec(
        num_scalar_prefetch=2,
        grid=(n_blk, nj),
        in_specs=[pl.BlockSpec((MOE_ROW_TILE, d), lambda i, j, be, na: (i, 0)),
                  pl.BlockSpec((None, d, FFN_TILE), lambda i, j, be, na: (be[i], 0, jj(i, j, na))),
                  pl.BlockSpec((None, d, FFN_TILE),
                               lambda i, j, be, na: (be[i], 0, nj + jj(i, j, na))),
                  pl.BlockSpec((None, FFN_TILE, d), lambda i, j, be, na: (be[i], jj(i, j, na), 0)),
                  pl.BlockSpec((MOE_ROW_TILE, 1), lambda i, j, be, na: (i, 0))],
        out_specs=pl.BlockSpec((MOE_ROW_TILE, d), lambda i, j, be, na: (i, 0)),
    )
    return pl.pallas_call(
        _moe_kernel,
        grid_spec=grid_spec,
        out_shape=jax.ShapeDtypeStruct((p_rows, d), F32),
        compiler_params=_params("parallel", "arbitrary"),
        name="moe_experts",
    )(blk_e, n_active, x_blk, w_gu, w_gu, w_down, gate_blk)


def _combine_ln_kernel(res_ref, y0_ref, y1_ref, g_ref, b_ref, o_ref, ob_ref):
    y = DN_ALPHA * res_ref[...] + (y0_ref[...] + y1_ref[...])
    o = _layer_norm_rows(y, g_ref[...], b_ref[...])
    o_ref[...] = o
    ob_ref[...] = o.astype(BF16)


def combine_ln(res, y0, y1, g, b):
    rows, d = res.shape
    row_spec = pl.BlockSpec((ROW_TILE, d), lambda i: (i, 0))
    vec_spec = pl.BlockSpec((1, d), lambda i: (0, 0))
    return pl.pallas_call(
        _combine_ln_kernel,
        grid=(pl.cdiv(rows, ROW_TILE),),
        in_specs=[row_spec, row_spec, row_spec, vec_spec, vec_spec],
        out_specs=[row_spec, row_spec],
        out_shape=[jax.ShapeDtypeStruct((rows, d), F32), jax.ShapeDtypeStruct((rows, d), BF16)],
        compiler_params=_params("parallel"),
        name="combine_ln",
    )(res, y0, y1, g, b)


def moe_res_ln(h, hb, w_router_pad, w_gu, w_down, g, b):
    rows, d = h.shape
    route = router(hb, w_router_pad)
    flat_e = route[:, :TOP_K].astype(jnp.int32).reshape(-1)
    flat_g = route[:, TOP_K:2 * TOP_K].reshape(-1)
    n_flat = rows * TOP_K
    n_blk = -(-n_flat // MOE_ROW_TILE) + N_EXPERTS
    p_rows = n_blk * MOE_ROW_TILE
    onehot = (flat_e[:, None] == jnp.arange(N_EXPERTS, dtype=jnp.int32)[None, :]).astype(jnp.int32)
    csum = jnp.cumsum(onehot, axis=0)
    rank = jnp.sum(csum * onehot, axis=1) - 1
    counts = csum[-1]
    padded = ((counts + MOE_ROW_TILE - 1) // MOE_ROW_TILE) * MOE_ROW_TILE
    padded_end = jnp.cumsum(padded)
    start_padded = padded_end - padded
    pos = start_padded[flat_e] + rank
    flat_tok = jnp.arange(n_flat, dtype=jnp.int32) // TOP_K
    buf_tok = jnp.zeros((p_rows,), jnp.int32).at[pos].set(flat_tok)
    buf_gate = jnp.zeros((p_rows,), F32).at[pos].set(flat_g)
    blk_start = jnp.arange(n_blk, dtype=jnp.int32) * MOE_ROW_TILE
    blk_e = jnp.minimum(jnp.searchsorted(padded_end, blk_start, side='right'),
                        N_EXPERTS - 1).astype(jnp.int32)
    n_active = (padded_end[-1] // MOE_ROW_TILE).astype(jnp.int32).reshape(1)
    x_blk = jnp.take(hb, buf_tok, axis=0)
    y = moe_experts(x_blk, buf_gate[:, None], blk_e, n_active, w_gu, w_down)
    pos2 = pos.reshape(rows, TOP_K)
    y0 = jnp.take(y, pos2[:, 0], axis=0)
    y1 = jnp.take(y, pos2[:, 1], axis=0)
    return combine_ln(h, y0, y1, g, b)


def _gla_chunk(q, k, v, r, la, gain, state):
    c = q.shape[0]
    row = lax.broadcasted_iota(jnp.int32, (c, c), 0)
    col = lax.broadcasted_iota(jnp.int32, (c, c), 1)
    causal = row >= col
    tri = causal.astype(F32)
    bcum = jnp.dot(tri, la, preferred_element_type=F32, precision=lax.Precision.HIGHEST)
    q_dec = (q * (GLA_DK_HEAD ** -0.5) * jnp.exp(bcum)).astype(BF16)
    k_inv = (k * jnp.exp(-bcum)).astype(BF16)
    vb = v.astype(BF16)
    att = jnp.where(causal, _dot_nt(q_dec, k_inv), 0.0)
    o = _dot(att.astype(BF16), vb)
    if state is not None:
        o = o + _dot(q_dec, state.astype(BF16))
    b_last = bcum[c - 1:c, :]
    k_state = (k * jnp.exp(b_last - bcum)).astype(BF16)
    upd = _dot_tn(k_state, vb)
    if state is None:
        new_state = upd
    else:
        ones = jnp.ones((c, LANES), F32)
        b_last_col = _dot_tn(la, ones, precision=lax.Precision.HIGHEST)
        decay = jnp.exp(b_last_col)
        decay = jnp.concatenate([decay] * (GLA_DV_HEAD // LANES), axis=1)
        new_state = state * decay + upd
    mu = jnp.mean(o, axis=-1, keepdims=True)
    oc = o - mu
    var = jnp.mean(oc * oc, axis=-1, keepdims=True)
    on = oc * lax.rsqrt(var + LN_EPS) * gain
    return (on * (r * jax.nn.sigmoid(r))).astype(BF16), new_state


def _gla_kernel(qm_ref, km_ref, vm_ref, rm_ref, lam_ref, q_ref, k_ref, v_ref, r_ref, la_ref,
                gain_ref, o_ref, state_ref, *, blocks_per_seq):
    s = pl.program_id(1)
    gain = gain_ref[...]
    n_batch = state_ref.shape[0]

    @pl.when(s == 0)
    def _():
        for bi in range(n_batch):
            sl = slice(bi * N_META, (bi + 1) * N_META)
            og, st = _gla_chunk(qm_ref[sl, :], km_ref[sl, :], vm_ref[sl, :], rm_ref[sl, :],
                                lam_ref[sl, :], gain, None)
            state_ref[bi] = st
            o_ref[sl, :] = og

    @pl.when(s > 0)
    def _():
        bi = (s - 1) // blocks_per_seq
        st = state_ref[bi]
        for ci in range(q_ref.shape[0] // GLA_CHUNK):
            sl = slice(ci * GLA_CHUNK, (ci + 1) * GLA_CHUNK)
            og, st = _gla_chunk(q_ref[sl, :], k_ref[sl, :], v_ref[sl, :], r_ref[sl, :],
                                la_ref[sl, :], gain, st)
            o_ref[sl, :] = og
        state_ref[bi] = st


def gla_mix(qkvr, log_a, norm_gain, n_batch, seq):
    rows = qkvr.shape[0]
    n_real = n_batch * seq
    meta_rows = n_batch * N_META
    blocks_per_seq = seq // GLA_ROWS
    n_steps = 1 + n_real // GLA_ROWS
    meta_blk = n_real // meta_rows
    meta_out_blk = n_real // GLA_ROWS
    dk, dv, nh = GLA_DK_HEAD, GLA_DV_HEAD, GLA_HEADS
    kq, kk, kv_, kr = 0, GLA_DK // dk, 2 * GLA_DK // dv, (2 * GLA_DK + GLA_DV) // dv

    def real_blk(s):
        return jnp.maximum(s - 1, 0)

    def meta_spec(width, col0):
        return pl.BlockSpec((meta_rows, width), lambda h, s: (meta_blk, col0 + h))

    def real_spec(width, col0):
        return pl.BlockSpec((GLA_ROWS, width), lambda h, s: (real_blk(s), col0 + h))

    return pl.pallas_call(
        functools.partial(_gla_kernel, blocks_per_seq=blocks_per_seq),
        grid=(nh, n_steps),
        in_specs=[meta_spec(dk, kq), meta_spec(dk, kk), meta_spec(dv, kv_), meta_spec(dv, kr),
                  meta_spec(dk, 0),
                  real_spec(dk, kq), real_spec(dk, kk), real_spec(dv, kv_), real_spec(dv, kr),
                  real_spec(dk, 0),
                  pl.BlockSpec((1, dv), lambda h, s: (0, h))],
        out_specs=pl.BlockSpec((GLA_ROWS, dv),
                               lambda h, s: (jnp.where(s == 0, meta_out_blk, s - 1), h)),
        out_shape=jax.ShapeDtypeStruct((rows, GLA_DV), BF16),
        scratch_shapes=[pltpu.VMEM((n_batch, dk, dv), F32)],
        compiler_params=_params("parallel", "arbitrary"),
        name="gla_mix",
    )(qkvr, qkvr, qkvr, qkvr, log_a, qkvr, qkvr, qkvr, qkvr, log_a, norm_gain)


def _swa_kernel(sink_ref, q_ref, km_ref, kp_ref, kc_ref, vm_ref, vp_ref, vc_ref,
                bm_ref, bp_ref, bc_ref, o_ref, *, blocks_per_seq):
    kvh = pl.program_id(0)
    s = pl.program_id(1)
    bi = jnp.maximum(s - 1, 0) // blocks_per_seq
    m0 = pl.multiple_of(bi * N_META, N_META)
    km = km_ref[0, pl.ds(m0, N_META), :]
    vm = vm_ref[0, pl.ds(m0, N_META), :]
    kp, kc, vp, vc = kp_ref[0], kc_ref[0], vp_ref[0], vc_ref[0]
    hd = SWA_HEAD_DIM
    for g in range(SWA_GROUP):
        q = q_ref[:, g * hd:(g + 1) * hd]
        sm = _dot_nt(q, km) + bm_ref[0, g]
        sp = _dot_nt(q, kp) + bp_ref[0, g]
        sc = _dot_nt(q, kc) + bc_ref[0, g]
        sink = sink_ref[kvh * SWA_GROUP + g]
        m = jnp.maximum(jnp.maximum(jnp.max(sm, axis=-1, keepdims=True),
                                    jnp.max(sp, axis=-1, keepdims=True)),
                        jnp.maximum(jnp.max(sc, axis=-1, keepdims=True), sink))
        pm, pp, pc = jnp.exp(sm - m), jnp.exp(sp - m), jnp.exp(sc - m)
        denom = (jnp.sum(pm, axis=-1, keepdims=True) + jnp.sum(pp, axis=-1, keepdims=True)
                 + jnp.sum(pc, axis=-1, keepdims=True) + jnp.exp(sink - m))
        o = _dot(pm.astype(BF16), vm) + _dot(pp.astype(BF16), vp) + _dot(pc.astype(BF16), vc)
        o_ref[:, g * hd:(g + 1) * hd] = (o / denom).astype(o_ref.dtype)


def swa_mix(q, kv, sinks, bias_m, bias_p, bias_c, n_batch, seq):
    rows = q.shape[0]
    blocks_per_seq = seq // SWA_BLOCK
    n_real_blk = n_batch * blocks_per_seq
    n_steps = 1 + n_real_blk
    meta_blk = n_real_blk
    assert n_batch * N_META == SWA_BLOCK
    hd, kvn, grp = SWA_HEAD_DIM, SWA_KV_HEADS, SWA_GROUP

    def cur_blk(s):
        return jnp.where(s == 0, meta_blk, s - 1)

    def prev_blk(s):
        return jnp.maximum(s - 2, 0)

    def variant(s):
        return jnp.where(s == 0, 2, jnp.where((s - 1) % blocks_per_seq == 0, 0, 1))

    def kv_spec(head0, blk_fn):
        return pl.BlockSpec((1, SWA_BLOCK, hd), lambda kh, s, sk: (head0 + kh, blk_fn(s), 0))

    def bias_spec(width):
        return pl.BlockSpec((1, grp, SWA_BLOCK, width), lambda kh, s, sk: (variant(s), kh, 0, 0))

    grid_spec = pltpu.PrefetchScalarGridSpec(
        num_scalar_prefetch=1,
        grid=(kvn, n_steps),
        in_specs=[pl.BlockSpec((SWA_BLOCK, grp * hd), lambda kh, s, sk: (cur_blk(s), kh)),
                  kv_spec(0, lambda s: meta_blk), kv_spec(0, prev_blk), kv_spec(0, cur_blk),
                  kv_spec(kvn, lambda s: meta_blk), kv_spec(kvn, prev_blk), kv_spec(kvn, cur_blk),
                  bias_spec(N_META), bias_spec(SWA_BLOCK), bias_spec(SWA_BLOCK)],
        out_specs=pl.BlockSpec((SWA_BLOCK, grp * hd), lambda kh, s, sk: (cur_blk(s), kh)),
    )
    return pl.pallas_call(
        functools.partial(_swa_kernel, blocks_per_seq=blocks_per_seq),
        grid_spec=grid_spec,
        out_shape=jax.ShapeDtypeStruct((rows, kvn * grp * hd), BF16),
        compiler_params=_params("parallel", "arbitrary"),
        name="swa_mix",
    )(sinks, q, kv, kv, kv, kv, kv, kv, bias_m, bias_p, bias_c)


def _t5_bucket(dist):
    exact = REL_BUCKETS // 2
    d = jnp.maximum(dist, 0)
    df = jnp.maximum(d, 1).astype(F32)
    large = exact + (jnp.log(df / exact) / math.log(REL_MAX_DIST / exact)
                     * (REL_BUCKETS - exact)).astype(jnp.int32)
    large = jnp.minimum(large, REL_BUCKETS - 1)
    return jnp.where(d < exact, d, large)


def _swa_bias(table, n_batch):
    tab = table.astype(F32)
    blk = SWA_BLOCK
    i = jnp.arange(blk, dtype=jnp.int32)[:, None]
    j = jnp.arange(blk, dtype=jnp.int32)[None, :]
    m = jnp.arange(N_META, dtype=jnp.int32)[None, :]

    def lookup(dist):
        return jnp.transpose(tab[_t5_bucket(dist)], (2, 0, 1))

    neg = jnp.full((SWA_Q_HEADS, blk, blk), NEG_INF, F32)
    neg_m = jnp.full((SWA_Q_HEADS, blk, N_META), NEG_INF, F32)
    cur = jnp.where((i - j >= 0)[None], lookup(i - j), NEG_INF)
    d_prev = blk + i - j
    prev = jnp.where(((d_prev >= 0) & (d_prev < SWA_WINDOW))[None], lookup(d_prev), NEG_INF)
    meta0 = lookup(N_META + i - m)
    meta1 = lookup(N_META + blk + i - m)
    same_seq = (i // N_META) == (j // N_META)
    dm = (i % N_META) - (j % N_META)
    meta_tile = jnp.where((same_seq & (dm >= 0))[None], lookup(dm), NEG_INF)
    bias_m = jnp.stack([meta0, meta1, neg_m])
    bias_p = jnp.stack([neg, prev, neg])
    bias_c = jnp.stack([cur, cur, meta_tile])
    del n_batch
    return bias_m, bias_p, bias_c


def kernel(x, meta_tokens, rel_bias_table, ln_gain, ln_bias, gla_w_in, gla_w_gate2, gla_b_gate,
           gla_norm_gain, gla_w_out, kv_w_shared, swa_w_q, swa_sinks, swa_w_out,
           ffn_w_gate_up, ffn_w_down, moe_w_router, moe_w_gate_up, moe_w_down):
    bsz, seq, d = x.shape
    n_real = bsz * seq
    h = jnp.concatenate([x.reshape(n_real, d),
                         jnp.broadcast_to(meta_tokens.astype(x.dtype)[None], (bsz, N_META, d))
                         .reshape(bsz * N_META, d)], axis=0)
    hb = h.astype(BF16)
    bias_m, bias_p, bias_c = _swa_bias(rel_bias_table, bsz)
    kv = None
    for li in range(DEPTH):
        g0, b0 = ln_gain[li, 0][None, :], ln_bias[li, 0][None, :]
        g1, b1 = ln_gain[li, 1][None, :], ln_bias[li, 1][None, :]
        if li < N_A_LAYERS:
            w_in = gla_w_in[li].astype(BF16)
            qkvr = linear(hb, w_in, GLA_MAIN, F32, tn=512)
            wg_pad = jnp.pad(w_in[:, GLA_MAIN:], ((0, 0), (0, LANES - GLA_GATE_RANK)))
            w2_pad = jnp.pad(gla_w_gate2[li].astype(BF16), ((0, LANES - GLA_GATE_RANK), (0, 0)))
            log_a = gla_gate(hb, wg_pad, w2_pad, gla_b_gate[li][None, :])
            mix_in = gla_mix(qkvr, log_a, gla_norm_gain[li][None, :], bsz, seq)
            w_out = gla_w_out[li].astype(BF16)
        else:
            jb = li - N_A_LAYERS
            q = linear(hb, swa_w_q[jb].astype(BF16), d, BF16, tn=512, scale=SWA_HEAD_DIM ** -0.5)
            mix_in = swa_mix(q, kv, swa_sinks[jb], bias_m, bias_p, bias_c, bsz, seq)
            w_out = swa_w_out[jb].astype(BF16)
        h, hb = linear_res_ln(mix_in, w_out, h, g0, b0)
        if li % 2 == 0:
            h, hb = ffn_res_ln(hb, h, ffn_w_gate_up[li // 2].astype(BF16),
                               ffn_w_down[li // 2].astype(BF16), g1, b1)
        else:
            w_r = jnp.pad(moe_w_router[li // 2].astype(BF16), ((0, 0), (0, LANES - N_EXPERTS)))
            h, hb = moe_res_ln(h, hb, w_r, moe_w_gate_up[li // 2].astype(BF16),
                               moe_w_down[li // 2].astype(BF16), g1, b1)
        if li == N_A_LAYERS - 1:
            kv = heads_linear(hb, kv_w_shared.astype(BF16))
    return h[:n_real].reshape(bsz, seq, d)
```

```python
import functools
import math

import jax
import jax.numpy as jnp
from jax import lax
from jax.experimental import pallas as pl
from jax.experimental.pallas import tpu as pltpu

F32 = jnp.float32
BF16 = jnp.bfloat16

D_MODEL = 2048
DEPTH = 4
N_META = 16
N_A_LAYERS = DEPTH // 2
DN_ALPHA = (2 * DEPTH) ** 0.25
LN_EPS = 1e-5

GLA_HEADS = 4
GLA_DK = D_MODEL // 2
GLA_DV = D_MODEL
GLA_DK_HEAD = GLA_DK // GLA_HEADS
GLA_DV_HEAD = GLA_DV // GLA_HEADS
GLA_GATE_RANK = 16
GLA_GATE_TAU = 16.0
GLA_CHUNK = 64
GLA_MAIN = 2 * GLA_DK + 2 * GLA_DV

SWA_HEAD_DIM = 64
SWA_Q_HEADS = D_MODEL // SWA_HEAD_DIM
SWA_GROUP = 8
SWA_KV_HEADS = SWA_Q_HEADS // SWA_GROUP
SWA_PAIRS = SWA_GROUP // 2
SWA_WINDOW = 128
SWA_BLOCK = 128

REL_BUCKETS = 32
REL_MAX_DIST = 128

FFN_DIM = 7 * D_MODEL // 2
N_EXPERTS = 8
TOP_K = 2
NEG_INF = -1e9

LANES = 128
VMEM_LIMIT = 56 * 1024 * 1024
ROW_TILE = 512
PROJ_ROW_TILE = 1376
PROJ_COL_TILE = 1024
FFN_TILE = 512
MOE_ROW_TILE = 512
GLA_ROWS = 256


def _params(*sem):
    return pltpu.CompilerParams(dimension_semantics=sem, vmem_limit_bytes=VMEM_LIMIT)


def _layer_norm_rows(y, g, b):
    mu = jnp.mean(y, axis=-1, keepdims=True)
    yc = y - mu
    var = jnp.mean(yc * yc, axis=-1, keepdims=True)
    return yc * lax.rsqrt(var + LN_EPS) * g + b


def _dot(a, b):
    return jnp.dot(a, b, preferred_element_type=F32)


def _dot_nt(a, b):
    return lax.dot_general(a, b, (((1,), (1,)), ((), ())), preferred_element_type=F32)


def _dot_tn(a, b, precision=None):
    return lax.dot_general(a, b, (((0,), (0,)), ((), ())), preferred_element_type=F32,
                           precision=precision)


def _linear_kernel(x_ref, w_ref, o_ref, *, scale):
    acc = _dot(x_ref[...], w_ref[...])
    if scale != 1.0:
        acc = acc * scale
    o_ref[...] = acc.astype(o_ref.dtype)


def linear(x, w, layer, n_out, out_dtype, scale=1.0):
    rows, k = x.shape
    tm = PROJ_ROW_TILE if rows % PROJ_ROW_TILE == 0 else ROW_TILE
    tn = PROJ_COL_TILE
    return pl.pallas_call(
        functools.partial(_linear_kernel, scale=scale),
        grid=(pl.cdiv(rows, tm), n_out // tn),
        in_specs=[pl.BlockSpec((tm, k), lambda i, j: (i, 0)),
                  pl.BlockSpec((None, k, tn), lambda i, j: (layer, 0, j))],
        out_specs=pl.BlockSpec((tm, tn), lambda i, j: (i, j)),
        out_shape=jax.ShapeDtypeStruct((rows, n_out), out_dtype),
        compiler_params=_params("parallel", "arbitrary"),
        name="linear",
    )(x, w)


def _gla_gate_kernel(x_ref, wg_ref, w2_ref, b_ref, o_ref):
    g_low = _dot(x_ref[...], wg_ref[...])
    z = _dot(g_low.astype(BF16), w2_ref[...]) + b_ref[...]
    log_sig = jnp.minimum(z, 0.0) - jnp.log1p(jnp.exp(-jnp.abs(z)))
    o_ref[...] = log_sig / GLA_GATE_TAU


def gla_gate(x, wg_pad, w2_pad, b_gate):
    rows, k = x.shape
    return pl.pallas_call(
        _gla_gate_kernel,
        grid=(pl.cdiv(rows, ROW_TILE),),
        in_specs=[pl.BlockSpec((ROW_TILE, k), lambda i: (i, 0)),
                  pl.BlockSpec((k, LANES), lambda i: (0, 0)),
                  pl.BlockSpec((LANES, GLA_DK), lambda i: (0, 0)),
                  pl.BlockSpec((1, GLA_DK), lambda i: (0, 0))],
        out_specs=pl.BlockSpec((ROW_TILE, GLA_DK), lambda i: (i, 0)),
        out_shape=jax.ShapeDtypeStruct((rows, GLA_DK), F32),
        compiler_params=_params("parallel"),
        name="gla_gate",
    )(x, wg_pad, w2_pad, b_gate)


def _linear_res_ln_kernel(x_ref, w_ref, res_ref, g_ref, b_ref, o_ref, ob_ref):
    y = DN_ALPHA * res_ref[...] + _dot(x_ref[...], w_ref[...])
    o = _layer_norm_rows(y, g_ref[...], b_ref[...])
    o_ref[...] = o
    ob_ref[...] = o.astype(BF16)


def linear_res_ln(x, w, layer, res, g, b):
    rows, k = x.shape
    d = w.shape[-1]
    return pl.pallas_call(
        _linear_res_ln_kernel,
        grid=(pl.cdiv(rows, ROW_TILE),),
        in_specs=[pl.BlockSpec((ROW_TILE, k), lambda i: (i, 0)),
                  pl.BlockSpec((None, k, d), lambda i: (layer, 0, 0)),
                  pl.BlockSpec((ROW_TILE, d), lambda i: (i, 0)),
                  pl.BlockSpec((1, d), lambda i: (0, 0)),
                  pl.BlockSpec((1, d), lambda i: (0, 0))],
        out_specs=[pl.BlockSpec((ROW_TILE, d), lambda i: (i, 0)),
                   pl.BlockSpec((ROW_TILE, d), lambda i: (i, 0))],
        out_shape=[jax.ShapeDtypeStruct((rows, d), F32), jax.ShapeDtypeStruct((rows, d), BF16)],
        compiler_params=_params("parallel"),
        name="linear_res_ln",
    )(x, w, res, g, b)


def _swiglu_partial(x, wa, wu, wd):
    a = _dot(x, wa)
    u = _dot(x, wu)
    hidden = (a * jax.nn.sigmoid(a) * u).astype(BF16)
    return _dot(hidden, wd)


def _ffn_kernel(x_ref, wa_ref, wu_ref, wd_ref, res_ref, g_ref, b_ref, o_ref, ob_ref, acc_ref):
    j = pl.program_id(1)

    @pl.when(j == 0)
    def _():
        acc_ref[...] = jnp.zeros_like(acc_ref)

    acc_ref[...] += _swiglu_partial(x_ref[...], wa_ref[...], wu_ref[...], wd_ref[...])

    @pl.when(j == pl.num_programs(1) - 1)
    def _():
        o = _layer_norm_rows(DN_ALPHA * res_ref[...] + acc_ref[...], g_ref[...], b_ref[...])
        o_ref[...] = o
        ob_ref[...] = o.astype(BF16)


def ffn_res_ln(xb, res, w_gu, w_down, layer, g, b):
    rows, d = xb.shape
    f = w_down.shape[-2]
    nj = f // FFN_TILE
    return pl.pallas_call(
        _ffn_kernel,
        grid=(pl.cdiv(rows, ROW_TILE), nj),
        in_specs=[pl.BlockSpec((ROW_TILE, d), lambda i, j: (i, 0)),
                  pl.BlockSpec((None, d, FFN_TILE), lambda i, j: (layer, 0, j)),
                  pl.BlockSpec((None, d, FFN_TILE), lambda i, j: (layer, 0, nj + j)),
                  pl.BlockSpec((None, FFN_TILE, d), lambda i, j: (layer, j, 0)),
                  pl.BlockSpec((ROW_TILE, d), lambda i, j: (i, 0)),
                  pl.BlockSpec((1, d), lambda i, j: (0, 0)),
                  pl.BlockSpec((1, d), lambda i, j: (0, 0))],
        out_specs=[pl.BlockSpec((ROW_TILE, d), lambda i, j: (i, 0)),
                   pl.BlockSpec((ROW_TILE, d), lambda i, j: (i, 0))],
        out_shape=[jax.ShapeDtypeStruct((rows, d), F32), jax.ShapeDtypeStruct((rows, d), BF16)],
        scratch_shapes=[pltpu.VMEM((ROW_TILE, d), F32)],
        compiler_params=_params("parallel", "arbitrary"),
        name="ffn_res_ln",
    )(xb, w_gu, w_gu, w_down, res, g, b)


def _router_kernel(x_ref, w_ref, o_ref):
    logits = _dot(x_ref[...], w_ref[...])
    lane = lax.broadcasted_iota(jnp.int32, logits.shape, 1)
    lane_f = lane.astype(F32)
    logits = jnp.where(lane < N_EXPERTS, logits, -jnp.inf)
    m1 = jnp.max(logits, axis=-1, keepdims=True)
    i1 = jnp.min(jnp.where(logits == m1, lane_f, float(LANES)), axis=-1, keepdims=True)
    rest = jnp.where(lane_f == i1, -jnp.inf, logits)
    m2 = jnp.max(rest, axis=-1, keepdims=True)
    i2 = jnp.min(jnp.where(rest == m2, lane_f, float(LANES)), axis=-1, keepdims=True)
    e2 = jnp.exp(m2 - m1)
    g1 = 1.0 / (1.0 + e2)
    g2 = e2 / (1.0 + e2)
    o_ref[...] = jnp.where(lane == 0, i1, jnp.where(lane == 1, i2,
                           jnp.where(lane == 2, g1, jnp.where(lane == 3, g2, 0.0))))


def router(xb, w_router_pad):
    rows, d = xb.shape
    return pl.pallas_call(
        _router_kernel,
        grid=(pl.cdiv(rows, ROW_TILE),),
        in_specs=[pl.BlockSpec((ROW_TILE, d), lambda i: (i, 0)),
                  pl.BlockSpec((d, LANES), lambda i: (0, 0))],
        out_specs=pl.BlockSpec((ROW_TILE, LANES), lambda i: (i, 0)),
        out_shape=jax.ShapeDtypeStruct((rows, LANES), F32),
        compiler_params=_params("parallel"),
        name="router",
    )(xb, w_router_pad)


def _moe_kernel(blk_e_ref, n_act_ref, x_ref, wa_ref, wu_ref, wd_ref, gate_ref, o_ref):
    del blk_e_ref
    i = pl.program_id(0)
    j = pl.program_id(1)

    @pl.when(j == 0)
    def _():
        o_ref[...] = jnp.zeros_like(o_ref)

    @pl.when(i < n_act_ref[0])
    def _():
        o_ref[...] += _swiglu_partial(x_ref[...], wa_ref[...], wu_ref[...], wd_ref[...])

        @pl.when(j == pl.num_programs(1) - 1)
        def _():
            o_ref[...] = o_ref[...] * gate_ref[...]


def moe_experts(x_blk, gate_blk, blk_e, n_active, w_gu, w_down, layer):
    p_rows, d = x_blk.shape
    f = w_down.shape[-2]
    nj = f // FFN_TILE
    n_blk = p_rows // MOE_ROW_TILE

    def jj(i, j, n_act):
        return jnp.where(i < n_act[0], j, nj - 1)

    grid_spec = pltpu.PrefetchScalarGridSpec(
        num_scalar_prefetch=2,
        grid=(n_blk, nj),
        in_specs=[pl.BlockSpec((MOE_ROW_TILE, d), lambda i, j, be, na: (i, 0)),
                  pl.BlockSpec((None, None, d, FFN_TILE),
                               lambda i, j, be, na: (layer, be[i], 0, jj(i, j, na))),
                  pl.BlockSpec((None, None, d, FFN_TILE),
                               lambda i, j, be, na: (layer, be[i], 0, nj + jj(i, j, na))),
                  pl.BlockSpec((None, None, FFN_TILE, d),
                               lambda i, j, be, na: (layer, be[i], jj(i, j, na), 0)),
                  pl.BlockSpec((MOE_ROW_TILE, 1), lambda i, j, be, na: (i, 0))],
        out_specs=pl.BlockSpec((MOE_ROW_TILE, d), lambda i, j, be, na: (i, 0)),
    )
    return pl.pallas_call(
        _moe_kernel,
        grid_spec=grid_spec,
        out_shape=jax.ShapeDtypeStruct((p_rows, d), F32),
        compiler_params=_params("parallel", "arbitrary"),
        name="moe_experts",
    )(blk_e, n_active, x_blk, w_gu, w_gu, w_down, gate_blk)


def _combine_ln_kernel(res_ref, y0_ref, y1_ref, g_ref, b_ref, o_ref, ob_ref):
    y = DN_ALPHA * res_ref[...] + (y0_ref[...] + y1_ref[...])
    o = _layer_norm_rows(y, g_ref[...], b_ref[...])
    o_ref[...] = o
    ob_ref[...] = o.astype(BF16)


def combine_ln(res, y0, y1, g, b, out_rows):
    d = res.shape[1]
    row_spec = pl.BlockSpec((ROW_TILE, d), lambda i: (i, 0))
    vec_spec = pl.BlockSpec((1, d), lambda i: (0, 0))
    return pl.pallas_call(
        _combine_ln_kernel,
        grid=(pl.cdiv(out_rows, ROW_TILE),),
        in_specs=[row_spec, row_spec, row_spec, vec_spec, vec_spec],
        out_specs=[row_spec, row_spec],
        out_shape=[jax.ShapeDtypeStruct((out_rows, d), F32),
                   jax.ShapeDtypeStruct((out_rows, d), BF16)],
        compiler_params=_params("parallel"),
        name="combine_ln",
    )(res, y0, y1, g, b)


def moe_res_ln(h, hb, w_router_pad, w_gu, w_down, layer, g, b, out_rows):
    rows, d = h.shape
    route = router(hb, w_router_pad)
    flat_e = route[:, :TOP_K].astype(jnp.int32).reshape(-1)
    flat_g = route[:, TOP_K:2 * TOP_K].reshape(-1)
    n_flat = rows * TOP_K
    n_blk = -(-n_flat // MOE_ROW_TILE) + N_EXPERTS
    p_rows = n_blk * MOE_ROW_TILE
    onehot = (flat_e[:, None] == jnp.arange(N_EXPERTS, dtype=jnp.int32)[None, :]).astype(jnp.int32)
    csum = jnp.cumsum(onehot, axis=0)
    rank = jnp.sum(csum * onehot, axis=1) - 1
    counts = csum[-1]
    padded = ((counts + MOE_ROW_TILE - 1) // MOE_ROW_TILE) * MOE_ROW_TILE
    padded_end = jnp.cumsum(padded)
    start_padded = padded_end - padded
    pos = start_padded[flat_e] + rank
    flat_tok = jnp.arange(n_flat, dtype=jnp.int32) // TOP_K
    buf_tok = jnp.zeros((p_rows,), jnp.int32).at[pos].set(flat_tok)
    buf_gate = jnp.zeros((p_rows,), F32).at[pos].set(flat_g)
    blk_start = jnp.arange(n_blk, dtype=jnp.int32) * MOE_ROW_TILE
    blk_e = jnp.minimum(jnp.searchsorted(padded_end, blk_start, side='right'),
                        N_EXPERTS - 1).astype(jnp.int32)
    n_active = (padded_end[-1] // MOE_ROW_TILE).astype(jnp.int32).reshape(1)
    x_blk = hb.at[buf_tok].get(mode="promise_in_bounds")
    y = moe_experts(x_blk, buf_gate[:, None], blk_e, n_active, w_gu, w_down, layer)
    pos2 = pos.reshape(rows, TOP_K)
    y0 = y.at[pos2[:, 0]].get(mode="promise_in_bounds")
    y1 = y.at[pos2[:, 1]].get(mode="promise_in_bounds")
    return combine_ln(h, y0, y1, g, b, out_rows)


def _gla_chunk(q, k, v, r, la, gain, state):
    c = q.shape[0]
    row = lax.broadcasted_iota(jnp.int32, (c, c), 0)
    col = lax.broadcasted_iota(jnp.int32, (c, c), 1)
    causal = row >= col
    tri = causal.astype(F32)
    bcum = jnp.dot(tri, la, preferred_element_type=F32, precision=lax.Precision.HIGHEST)
    q_dec = (q * (GLA_DK_HEAD ** -0.5) * jnp.exp(bcum)).astype(BF16)
    k_inv = (k * jnp.exp(-bcum)).astype(BF16)
    vb = v.astype(BF16)
    att = jnp.where(causal, _dot_nt(q_dec, k_inv), 0.0)
    o = _dot(att.astype(BF16), vb)
    if state is not None:
        o = o + _dot(q_dec, state.astype(BF16))
    b_last = bcum[c - 1:c, :]
    k_state = (k * jnp.exp(b_last - bcum)).astype(BF16)
    upd = _dot_tn(k_state, vb)
    if state is None:
        new_state = upd
    else:
        ones = jnp.ones((c, LANES), F32)
        b_last_col = _dot_tn(la, ones, precision=lax.Precision.HIGHEST)
        decay = jnp.exp(b_last_col)
        decay = jnp.concatenate([decay] * (GLA_DV_HEAD // LANES), axis=1)
        new_state = state * decay + upd
    mu = jnp.mean(o, axis=-1, keepdims=True)
    oc = o - mu
    var = jnp.mean(oc * oc, axis=-1, keepdims=True)
    on = oc * lax.rsqrt(var + LN_EPS) * gain
    return (on * (r * jax.nn.sigmoid(r))).astype(BF16), new_state


def _gla_kernel(qm_ref, km_ref, vm_ref, rm_ref, lam_ref, q_ref, k_ref, v_ref, r_ref, la_ref,
                gain_ref, o_ref, state_ref, *, blocks_per_seq):
    s = pl.program_id(1)
    gain = gain_ref[...]
    n_batch = state_ref.shape[0]

    @pl.when(s == 0)
    def _():
        for bi in range(n_batch):
            sl = slice(bi * N_META, (bi + 1) * N_META)
            og, st = _gla_chunk(qm_ref[sl, :], km_ref[sl, :], vm_ref[sl, :], rm_ref[sl, :],
                                lam_ref[sl, :], gain, None)
            state_ref[bi] = st
            o_ref[sl, :] = og

    @pl.when(s > 0)
    def _():
        bi = (s - 1) // blocks_per_seq
        st = state_ref[bi]
        for ci in range(q_ref.shape[0] // GLA_CHUNK):
            sl = slice(ci * GLA_CHUNK, (ci + 1) * GLA_CHUNK)
            og, st = _gla_chunk(q_ref[sl, :], k_ref[sl, :], v_ref[sl, :], r_ref[sl, :],
                                la_ref[sl, :], gain, st)
            o_ref[sl, :] = og
        state_ref[bi] = st


def gla_mix(qkvr, log_a, norm_gain, n_batch, seq):
    rows = qkvr.shape[0]
    n_real = n_batch * seq
    meta_rows = n_batch * N_META
    blocks_per_seq = seq // GLA_ROWS
    n_steps = 1 + n_real // GLA_ROWS
    meta_blk = n_real // meta_rows
    meta_out_blk = n_real // GLA_ROWS
    dk, dv, nh = GLA_DK_HEAD, GLA_DV_HEAD, GLA_HEADS
    kq, kk, kv_, kr = 0, GLA_DK // dk, 2 * GLA_DK // dv, (2 * GLA_DK + GLA_DV) // dv

    def real_blk(s):
        return jnp.maximum(s - 1, 0)

    def meta_spec(width, col0):
        return pl.BlockSpec((meta_rows, width), lambda h, s: (meta_blk, col0 + h))

    def real_spec(width, col0):
        return pl.BlockSpec((GLA_ROWS, width), lambda h, s: (real_blk(s), col0 + h))

    return pl.pallas_call(
        functools.partial(_gla_kernel, blocks_per_seq=blocks_per_seq),
        grid=(nh, n_steps),
        in_specs=[meta_spec(dk, kq), meta_spec(dk, kk), meta_spec(dv, kv_), meta_spec(dv, kr),
                  meta_spec(dk, 0),
                  real_spec(dk, kq), real_spec(dk, kk), real_spec(dv, kv_), real_spec(dv, kr),
                  real_spec(dk, 0),
                  pl.BlockSpec((1, dv), lambda h, s: (0, h))],
        out_specs=pl.BlockSpec((GLA_ROWS, dv),
                               lambda h, s: (jnp.where(s == 0, meta_out_blk, s - 1), h)),
        out_shape=jax.ShapeDtypeStruct((rows, GLA_DV), BF16),
        scratch_shapes=[pltpu.VMEM((n_batch, dk, dv), F32)],
        compiler_params=_params("parallel", "arbitrary"),
        name="gla_mix",
    )(qkvr, qkvr, qkvr, qkvr, log_a, qkvr, qkvr, qkvr, qkvr, log_a, norm_gain)


def _block_diag2(x2):
    lane = lax.broadcasted_iota(jnp.int32, x2.shape, 1)
    zero = jnp.zeros_like(x2)
    return jnp.concatenate([jnp.where(lane < SWA_HEAD_DIM, x2, zero),
                            jnp.where(lane >= SWA_HEAD_DIM, x2, zero)], axis=0)


def _swa_kernel(q_ref, km_ref, kp_ref, kc_ref, vm_ref, vp_ref, vc_ref,
                bm_ref, bp_ref, bc_ref, o_ref, *, blocks_per_seq):
    s = pl.program_id(1)
    bi = jnp.maximum(s - 1, 0) // blocks_per_seq
    m0 = pl.multiple_of(bi * N_META, N_META)
    pad = jnp.zeros((SWA_BLOCK - N_META, LANES), BF16)
    k_m = jnp.concatenate([km_ref[pl.ds(m0, N_META), :], pad], axis=0)
    v_m = jnp.concatenate([vm_ref[pl.ds(m0, N_META), :], pad], axis=0)
    q = q_ref[...]
    qp = jnp.concatenate([q[:, p * LANES:(p + 1) * LANES] for p in range(SWA_PAIRS)], axis=0)

    s_m = _dot_nt(qp, _block_diag2(k_m)) + bm_ref[0, 0]
    s_p = _dot_nt(qp, _block_diag2(kp_ref[...])) + bp_ref[0, 0]
    s_c = _dot_nt(qp, _block_diag2(kc_ref[...])) + bc_ref[0, 0]
    top = jnp.maximum(jnp.maximum(s_m, s_p), s_c)
    m_even = jnp.max(top[:, :LANES], axis=-1, keepdims=True)
    m_odd = jnp.max(top[:, LANES:], axis=-1, keepdims=True)
    lane2 = lax.broadcasted_iota(jnp.int32, top.shape, 1)
    m_full = jnp.where(lane2 < LANES, m_even, m_odd)
    p_m, p_p, p_c = jnp.exp(s_m - m_full), jnp.exp(s_p - m_full), jnp.exp(s_c - m_full)
    p_sum = p_m + p_p + p_c
    l_even = jnp.sum(p_sum[:, :LANES], axis=-1, keepdims=True)
    l_odd = jnp.sum(p_sum[:, LANES:], axis=-1, keepdims=True)
    o = (_dot(p_m.astype(BF16), _block_diag2(v_m))
         + _dot(p_p.astype(BF16), _block_diag2(vp_ref[...]))
         + _dot(p_c.astype(BF16), _block_diag2(vc_ref[...])))
    lane1 = lax.broadcasted_iota(jnp.int32, o.shape, 1)
    o = o / jnp.where(lane1 < SWA_HEAD_DIM, l_even, l_odd)
    for p in range(SWA_PAIRS):
        o_ref[:, p * LANES:(p + 1) * LANES] = o[p * SWA_BLOCK:(p + 1) * SWA_BLOCK].astype(o_ref.dtype)


def swa_mix(q, kv2, bias_m, bias_p, bias_c, n_batch, seq):
    rows = q.shape[0]
    blocks_per_seq = seq // SWA_BLOCK
    n_real_blk = n_batch * blocks_per_seq
    n_steps = 1 + n_real_blk
    meta_blk = n_real_blk
    assert n_batch * N_META == SWA_BLOCK
    kvn, qw = SWA_KV_HEADS, SWA_GROUP * SWA_HEAD_DIM

    def cur_blk(s):
        return jnp.where(s == 0, meta_blk, s - 1)

    def prev_blk(s):
        return jnp.maximum(s - 2, 0)

    def variant(s):
        return jnp.where(s == 0, 2, jnp.where((s - 1) % blocks_per_seq == 0, 0, 1))

    def kv_spec(head0, blk_fn):
        return pl.BlockSpec((SWA_BLOCK, LANES), lambda kh, s: (blk_fn(s), head0 + kh))

    bias_spec = pl.BlockSpec((1, 1, SWA_PAIRS * SWA_BLOCK, 2 * SWA_BLOCK),
                             lambda kh, s: (variant(s), kh, 0, 0))
    return pl.pallas_call(
        functools.partial(_swa_kernel, blocks_per_seq=blocks_per_seq),
        grid=(kvn, n_steps),
        in_specs=[pl.BlockSpec((SWA_BLOCK, qw), lambda kh, s: (cur_blk(s), kh)),
                  kv_spec(0, lambda s: meta_blk), kv_spec(0, prev_blk), kv_spec(0, cur_blk),
                  kv_spec(kvn, lambda s: meta_blk), kv_spec(kvn, prev_blk), kv_spec(kvn, cur_blk),
                  bias_spec, bias_spec, bias_spec],
        out_specs=pl.BlockSpec((SWA_BLOCK, qw), lambda kh, s: (cur_blk(s), kh)),
        out_shape=jax.ShapeDtypeStruct((rows, kvn * qw), BF16),
        compiler_params=_params("parallel", "arbitrary"),
        name="swa_mix",
    )(q, kv2, kv2, kv2, kv2, kv2, kv2, bias_m, bias_p, bias_c)


def _t5_bucket(dist):
    exact = REL_BUCKETS // 2
    d = jnp.maximum(dist, 0)
    df = jnp.maximum(d, 1).astype(F32)
    large = exact + (jnp.log(df / exact) / math.log(REL_MAX_DIST / exact)
                     * (REL_BUCKETS - exact)).astype(jnp.int32)
    large = jnp.minimum(large, REL_BUCKETS - 1)
    return jnp.where(d < exact, d, large)


def _pair_layout(bias):
    nv = bias.shape[0]
    b = bias.reshape(nv, SWA_KV_HEADS, SWA_PAIRS, 2, SWA_BLOCK, SWA_BLOCK)
    b = jnp.transpose(b, (0, 1, 2, 4, 3, 5))
    return b.reshape(nv, SWA_KV_HEADS, SWA_PAIRS * SWA_BLOCK, 2 * SWA_BLOCK)


def _swa_bias_static(table):
    tab = table.astype(F32)
    blk = SWA_BLOCK
    i = jnp.arange(blk, dtype=jnp.int32)[:, None]
    j = jnp.arange(blk, dtype=jnp.int32)[None, :]
    m = jnp.arange(N_META, dtype=jnp.int32)[None, :]

    def lookup(dist):
        return jnp.transpose(tab[_t5_bucket(dist)], (2, 0, 1))

    neg = jnp.full((SWA_Q_HEADS, blk, blk), NEG_INF, F32)
    cur = jnp.where((i - j >= 0)[None], lookup(i - j), NEG_INF)
    d_prev = blk + i - j
    prev = jnp.where(((d_prev >= 0) & (d_prev < SWA_WINDOW))[None], lookup(d_prev), NEG_INF)
    meta0 = lookup(N_META + i - m)
    meta1 = lookup(N_META + blk + i - m)
    same_seq = (i // N_META) == (j // N_META)
    dm = (i % N_META) - (j % N_META)
    meta_tile = jnp.where((same_seq & (dm >= 0))[None], lookup(dm), NEG_INF)
    bias_p = _pair_layout(jnp.stack([neg, prev, neg]))
    bias_c = _pair_layout(jnp.stack([cur, cur, meta_tile]))
    return bias_p, bias_c, jnp.stack([meta0, meta1])


def _swa_bias_meta(meta01, sinks):
    nh, blk = SWA_Q_HEADS, SWA_BLOCK
    sink_col = jnp.broadcast_to(sinks.astype(F32)[None, :, None, None], (3, nh, blk, 1))
    meta = jnp.concatenate([meta01, jnp.full((1, nh, blk, N_META), NEG_INF, F32)], axis=0)
    rest = jnp.full((3, nh, blk, blk - N_META - 1), NEG_INF, F32)
    return _pair_layout(jnp.concatenate([meta, sink_col, rest], axis=-1))


def kernel(x, meta_tokens, rel_bias_table, ln_gain, ln_bias, gla_w_in, gla_w_gate2, gla_b_gate,
           gla_norm_gain, gla_w_out, kv_w_shared, swa_w_q, swa_sinks, swa_w_out,
           ffn_w_gate_up, ffn_w_down, moe_w_router, moe_w_gate_up, moe_w_down):
    bsz, seq, d = x.shape
    n_real = bsz * seq
    h = jnp.concatenate([x.reshape(n_real, d),
                         jnp.broadcast_to(meta_tokens.astype(x.dtype)[None], (bsz, N_META, d))
                         .reshape(bsz * N_META, d)], axis=0)
    rows = h.shape[0]
    hb = h.astype(BF16)
    bias_p, bias_c, meta01 = _swa_bias_static(rel_bias_table)

    w_in_b = gla_w_in.astype(BF16)
    w_gla_out_b = gla_w_out.astype(BF16)
    w_q_b = swa_w_q.astype(BF16)
    w_swa_out_b = swa_w_out.astype(BF16)
    w_ffn_gu_b = ffn_w_gate_up.astype(BF16)
    w_ffn_down_b = ffn_w_down.astype(BF16)
    w_moe_gu_b = moe_w_gate_up.astype(BF16)
    w_moe_down_b = moe_w_down.astype(BF16)
    hd = SWA_HEAD_DIM
    w_kv = kv_w_shared.reshape(d, 2 * SWA_KV_HEADS, 1, hd)
    w_kv2 = jnp.broadcast_to(w_kv, (d, 2 * SWA_KV_HEADS, 2, hd)).reshape(1, d, 4 * SWA_KV_HEADS * hd)
    w_kv2 = w_kv2.astype(BF16)

    kv2 = None
    for li in range(DEPTH):
        g0, b0 = ln_gain[li, 0][None, :], ln_bias[li, 0][None, :]
        g1, b1 = ln_gain[li, 1][None, :], ln_bias[li, 1][None, :]
        if li < N_A_LAYERS:
            qkvr = linear(hb, w_in_b, li, GLA_MAIN, F32)
            wg_pad = jnp.pad(w_in_b[li, :, GLA_MAIN:], ((0, 0), (0, LANES - GLA_GATE_RANK)))
            w2_pad = jnp.pad(gla_w_gate2[li].astype(BF16), ((0, LANES - GLA_GATE_RANK), (0, 0)))
            log_a = gla_gate(hb, wg_pad, w2_pad, gla_b_gate[li][None, :])
            mix_in = gla_mix(qkvr, log_a, gla_norm_gain[li][None, :], bsz, seq)
            h, hb = linear_res_ln(mix_in, w_gla_out_b, li, h, g0, b0)
        else:
            jb = li - N_A_LAYERS
            q = linear(hb, w_q_b, jb, d, BF16, scale=hd ** -0.5)
            bias_m = _swa_bias_meta(meta01, swa_sinks[jb])
            mix_in = swa_mix(q, kv2, bias_m, bias_p, bias_c, bsz, seq)
            h, hb = linear_res_ln(mix_in, w_swa_out_b, jb, h, g0, b0)
        if li % 2 == 0:
            h, hb = ffn_res_ln(hb, h, w_ffn_gu_b, w_ffn_down_b, li // 2, g1, b1)
        else:
            w_r = jnp.pad(moe_w_router[li // 2].astype(BF16), ((0, 0), (0, LANES - N_EXPERTS)))
            out_rows = n_real if li == DEPTH - 1 else rows
            h, hb = moe_res_ln(h, hb, w_r, w_moe_gu_b, w_moe_down_b, li // 2, g1, b1, out_rows)
        if li == N_A_LAYERS - 1:
            kv2 = linear(hb, w_kv2, 0, w_kv2.shape[-1], BF16)
    return h.reshape(bsz, seq, d)
```

```python
import functools
import math

import jax
import jax.numpy as jnp
from jax import lax
from jax.experimental import pallas as pl
from jax.experimental.pallas import tpu as pltpu

F32 = jnp.float32
BF16 = jnp.bfloat16

D_MODEL = 2048
DEPTH = 4
N_META = 16
N_A_LAYERS = DEPTH // 2
DN_ALPHA = (2 * DEPTH) ** 0.25
LN_EPS = 1e-5

GLA_HEADS = 4
GLA_DK = D_MODEL // 2
GLA_DV = D_MODEL
GLA_DK_HEAD = GLA_DK // GLA_HEADS
GLA_DV_HEAD = GLA_DV // GLA_HEADS
GLA_GATE_RANK = 16
GLA_GATE_TAU = 16.0
GLA_CHUNK = 64
GLA_MAIN = 2 * GLA_DK + 2 * GLA_DV

SWA_HEAD_DIM = 64
SWA_Q_HEADS = D_MODEL // SWA_HEAD_DIM
SWA_GROUP = 8
SWA_KV_HEADS = SWA_Q_HEADS // SWA_GROUP
SWA_PAIRS = SWA_GROUP // 2
SWA_WINDOW = 128
SWA_BLOCK = 128

REL_BUCKETS = 32
REL_MAX_DIST = 128

FFN_DIM = 7 * D_MODEL // 2
N_EXPERTS = 8
TOP_K = 2
NEG_INF = -1e9

LANES = 128
VMEM_LIMIT = 56 * 1024 * 1024
ROW_TILE = 512
PROJ_ROW_TILE = 1376
PROJ_COL_TILE = 1024
FFN_TILE = 512
MOE_ROW_TILE = 1024
GLA_ROWS = 256


def _params(*sem):
    return pltpu.CompilerParams(dimension_semantics=sem, vmem_limit_bytes=VMEM_LIMIT)


def _layer_norm_rows(y, g, b):
    mu = jnp.mean(y, axis=-1, keepdims=True)
    yc = y - mu
    var = jnp.mean(yc * yc, axis=-1, keepdims=True)
    return yc * lax.rsqrt(var + LN_EPS) * g + b


def _dot(a, b):
    return jnp.dot(a, b, preferred_element_type=F32)


def _dot_nt(a, b):
    return lax.dot_general(a, b, (((1,), (1,)), ((), ())), preferred_element_type=F32)


def _dot_tn(a, b, precision=None):
    return lax.dot_general(a, b, (((0,), (0,)), ((), ())), preferred_element_type=F32,
                           precision=precision)


def _linear_kernel(x_ref, w_ref, o_ref, *, scale):
    acc = _dot(x_ref[...], w_ref[...])
    if scale != 1.0:
        acc = acc * scale
    o_ref[...] = acc.astype(o_ref.dtype)


def linear(x, w, layer, n_out, out_dtype, scale=1.0):
    rows, k = x.shape
    tm = PROJ_ROW_TILE if rows % PROJ_ROW_TILE == 0 else ROW_TILE
    tn = PROJ_COL_TILE
    return pl.pallas_call(
        functools.partial(_linear_kernel, scale=scale),
        grid=(pl.cdiv(rows, tm), n_out // tn),
        in_specs=[pl.BlockSpec((tm, k), lambda i, j: (i, 0)),
                  pl.BlockSpec((None, k, tn), lambda i, j: (layer, 0, j))],
        out_specs=pl.BlockSpec((tm, tn), lambda i, j: (i, j)),
        out_shape=jax.ShapeDtypeStruct((rows, n_out), out_dtype),
        compiler_params=_params("parallel", "arbitrary"),
        name="linear",
    )(x, w)


def _gla_gate_kernel(x_ref, wg_ref, w2_ref, b_ref, o_ref):
    g_low = _dot(x_ref[...], wg_ref[...])
    z = _dot(g_low.astype(BF16), w2_ref[...]) + b_ref[...]
    log_sig = jnp.minimum(z, 0.0) - jnp.log1p(jnp.exp(-jnp.abs(z)))
    o_ref[...] = log_sig / GLA_GATE_TAU


def gla_gate(x, wg_pad, w2_pad, b_gate):
    rows, k = x.shape
    return pl.pallas_call(
        _gla_gate_kernel,
        grid=(pl.cdiv(rows, ROW_TILE),),
        in_specs=[pl.BlockSpec((ROW_TILE, k), lambda i: (i, 0)),
                  pl.BlockSpec((k, LANES), lambda i: (0, 0)),
                  pl.BlockSpec((LANES, GLA_DK), lambda i: (0, 0)),
                  pl.BlockSpec((1, GLA_DK), lambda i: (0, 0))],
        out_specs=pl.BlockSpec((ROW_TILE, GLA_DK), lambda i: (i, 0)),
        out_shape=jax.ShapeDtypeStruct((rows, GLA_DK), F32),
        compiler_params=_params("parallel"),
        name="gla_gate",
    )(x, wg_pad, w2_pad, b_gate)


def _linear_res_ln_kernel(x_ref, w_ref, res_ref, g_ref, b_ref, o_ref, ob_ref):
    y = DN_ALPHA * res_ref[...] + _dot(x_ref[...], w_ref[...])
    o = _layer_norm_rows(y, g_ref[...], b_ref[...])
    o_ref[...] = o
    ob_ref[...] = o.astype(BF16)


def linear_res_ln(x, w, layer, res, g, b):
    rows, k = x.shape
    d = w.shape[-1]
    return pl.pallas_call(
        _linear_res_ln_kernel,
        grid=(pl.cdiv(rows, ROW_TILE),),
        in_specs=[pl.BlockSpec((ROW_TILE, k), lambda i: (i, 0)),
                  pl.BlockSpec((None, k, d), lambda i: (layer, 0, 0)),
                  pl.BlockSpec((ROW_TILE, d), lambda i: (i, 0)),
                  pl.BlockSpec((1, d), lambda i: (0, 0)),
                  pl.BlockSpec((1, d), lambda i: (0, 0))],
        out_specs=[pl.BlockSpec((ROW_TILE, d), lambda i: (i, 0)),
                   pl.BlockSpec((ROW_TILE, d), lambda i: (i, 0))],
        out_shape=[jax.ShapeDtypeStruct((rows, d), F32), jax.ShapeDtypeStruct((rows, d), BF16)],
        compiler_params=_params("parallel"),
        name="linear_res_ln",
    )(x, w, res, g, b)


def _swiglu_partial(x, wa, wu, wd):
    a = _dot(x, wa)
    u = _dot(x, wu)
    hidden = (a * jax.nn.sigmoid(a) * u).astype(BF16)
    return _dot(hidden, wd)


def _ffn_kernel(x_ref, wa_ref, wu_ref, wd_ref, res_ref, g_ref, b_ref, o_ref, ob_ref, acc_ref):
    j = pl.program_id(1)

    @pl.when(j == 0)
    def _():
        acc_ref[...] = jnp.zeros_like(acc_ref)

    acc_ref[...] += _swiglu_partial(x_ref[...], wa_ref[...], wu_ref[...], wd_ref[...])

    @pl.when(j == pl.num_programs(1) - 1)
    def _():
        o = _layer_norm_rows(DN_ALPHA * res_ref[...] + acc_ref[...], g_ref[...], b_ref[...])
        o_ref[...] = o
        ob_ref[...] = o.astype(BF16)


def ffn_res_ln(xb, res, w_gu, w_down, layer, g, b):
    rows, d = xb.shape
    f = w_down.shape[-2]
    nj = f // FFN_TILE
    return pl.pallas_call(
        _ffn_kernel,
        grid=(pl.cdiv(rows, ROW_TILE), nj),
        in_specs=[pl.BlockSpec((ROW_TILE, d), lambda i, j: (i, 0)),
                  pl.BlockSpec((None, d, FFN_TILE), lambda i, j: (layer, 0, j)),
                  pl.BlockSpec((None, d, FFN_TILE), lambda i, j: (layer, 0, nj + j)),
                  pl.BlockSpec((None, FFN_TILE, d), lambda i, j: (layer, j, 0)),
                  pl.BlockSpec((ROW_TILE, d), lambda i, j: (i, 0)),
                  pl.BlockSpec((1, d), lambda i, j: (0, 0)),
                  pl.BlockSpec((1, d), lambda i, j: (0, 0))],
        out_specs=[pl.BlockSpec((ROW_TILE, d), lambda i, j: (i, 0)),
                   pl.BlockSpec((ROW_TILE, d), lambda i, j: (i, 0))],
        out_shape=[jax.ShapeDtypeStruct((rows, d), F32), jax.ShapeDtypeStruct((rows, d), BF16)],
        scratch_shapes=[pltpu.VMEM((ROW_TILE, d), F32)],
        compiler_params=_params("parallel", "arbitrary"),
        name="ffn_res_ln",
    )(xb, w_gu, w_gu, w_down, res, g, b)


def _router_kernel(x_ref, w_ref, o_ref):
    logits = _dot(x_ref[...], w_ref[...])
    lane = lax.broadcasted_iota(jnp.int32, logits.shape, 1)
    lane_f = lane.astype(F32)
    logits = jnp.where(lane < N_EXPERTS, logits, -jnp.inf)
    m1 = jnp.max(logits, axis=-1, keepdims=True)
    i1 = jnp.min(jnp.where(logits == m1, lane_f, float(LANES)), axis=-1, keepdims=True)
    rest = jnp.where(lane_f == i1, -jnp.inf, logits)
    m2 = jnp.max(rest, axis=-1, keepdims=True)
    i2 = jnp.min(jnp.where(rest == m2, lane_f, float(LANES)), axis=-1, keepdims=True)
    e2 = jnp.exp(m2 - m1)
    g1 = 1.0 / (1.0 + e2)
    g2 = e2 / (1.0 + e2)
    o_ref[...] = jnp.where(lane == 0, i1, jnp.where(lane == 1, i2,
                           jnp.where(lane == 2, g1, jnp.where(lane == 3, g2, 0.0))))


def router(xb, w_router_pad):
    rows, d = xb.shape
    return pl.pallas_call(
        _router_kernel,
        grid=(pl.cdiv(rows, ROW_TILE),),
        in_specs=[pl.BlockSpec((ROW_TILE, d), lambda i: (i, 0)),
                  pl.BlockSpec((d, LANES), lambda i: (0, 0))],
        out_specs=pl.BlockSpec((ROW_TILE, LANES), lambda i: (i, 0)),
        out_shape=jax.ShapeDtypeStruct((rows, LANES), F32),
        compiler_params=_params("parallel"),
        name="router",
    )(xb, w_router_pad)


def _moe_kernel(blk_e_ref, n_half_ref, x_ref, wa_ref, wu_ref, wd_ref, o_ref):
    del blk_e_ref
    i = pl.program_id(0)
    j = pl.program_id(1)
    n_half = n_half_ref[i]
    half = MOE_ROW_TILE // 2

    @pl.when(j == 0)
    def _():
        o_ref[...] = jnp.zeros_like(o_ref)

    @pl.when(n_half > 0)
    def _():
        wa = wa_ref[...].astype(BF16)
        wu = wu_ref[...].astype(BF16)
        wd = wd_ref[...].astype(BF16)
        o_ref[:half, :] += _swiglu_partial(x_ref[:half, :], wa, wu, wd)

        @pl.when(n_half > 1)
        def _():
            o_ref[half:, :] += _swiglu_partial(x_ref[half:, :], wa, wu, wd)


def moe_experts(x_blk, blk_e, n_half, w_gu, w_down, layer):
    p_rows, d = x_blk.shape
    f = w_down.shape[-2]
    nj = f // FFN_TILE
    n_blk = p_rows // MOE_ROW_TILE

    def jj(i, j, nh):
        return jnp.where(nh[i] > 0, j, nj - 1)

    once = pl.Buffered(1)
    grid_spec = pltpu.PrefetchScalarGridSpec(
        num_scalar_prefetch=2,
        grid=(n_blk, nj),
        in_specs=[pl.BlockSpec((MOE_ROW_TILE, d), lambda i, j, be, nh: (i, 0), pipeline_mode=once),
                  pl.BlockSpec((None, None, d, FFN_TILE),
                               lambda i, j, be, nh: (layer, be[i], 0, jj(i, j, nh))),
                  pl.BlockSpec((None, None, d, FFN_TILE),
                               lambda i, j, be, nh: (layer, be[i], 0, nj + jj(i, j, nh))),
                  pl.BlockSpec((None, None, FFN_TILE, d),
                               lambda i, j, be, nh: (layer, be[i], jj(i, j, nh), 0))],
        out_specs=pl.BlockSpec((MOE_ROW_TILE, d), lambda i, j, be, nh: (i, 0), pipeline_mode=once),
    )
    return pl.pallas_call(
        _moe_kernel,
        grid_spec=grid_spec,
        out_shape=jax.ShapeDtypeStruct((p_rows, d), F32),
        compiler_params=_params("parallel", "arbitrary"),
        name="moe_experts",
    )(blk_e, n_half, x_blk, w_gu, w_gu, w_down)


def _combine_ln_kernel(res_ref, y0_ref, y1_ref, gate_ref, g_ref, b_ref, o_ref, ob_ref):
    gate = gate_ref[...]
    moe = y0_ref[...] * gate[:, 2:3] + y1_ref[...] * gate[:, 3:4]
    o = _layer_norm_rows(DN_ALPHA * res_ref[...] + moe, g_ref[...], b_ref[...])
    o_ref[...] = o
    ob_ref[...] = o.astype(BF16)


def combine_ln(res, y0, y1, route, g, b, out_rows):
    d = res.shape[1]
    row_spec = pl.BlockSpec((ROW_TILE, d), lambda i: (i, 0))
    vec_spec = pl.BlockSpec((1, d), lambda i: (0, 0))
    return pl.pallas_call(
        _combine_ln_kernel,
        grid=(pl.cdiv(out_rows, ROW_TILE),),
        in_specs=[row_spec, row_spec, row_spec, pl.BlockSpec((ROW_TILE, LANES), lambda i: (i, 0)),
                  vec_spec, vec_spec],
        out_specs=[row_spec, row_spec],
        out_shape=[jax.ShapeDtypeStruct((out_rows, d), F32),
                   jax.ShapeDtypeStruct((out_rows, d), BF16)],
        compiler_params=_params("parallel"),
        name="combine_ln",
    )(res, y0, y1, route, g, b)


def moe_res_ln(h, hb, w_router_pad, w_gu, w_down, layer, g, b, out_rows):
    rows, d = h.shape
    route = router(hb, w_router_pad)
    flat_e = route[:, :TOP_K].astype(jnp.int32).reshape(-1)
    n_flat = rows * TOP_K
    n_blk = -(-n_flat // MOE_ROW_TILE) + N_EXPERTS
    p_rows = n_blk * MOE_ROW_TILE
    onehot = (flat_e[:, None] == jnp.arange(N_EXPERTS, dtype=jnp.int32)[None, :]).astype(jnp.int32)
    csum = jnp.cumsum(onehot, axis=0)
    rank = jnp.sum(csum * onehot, axis=1) - 1
    counts = csum[-1]
    padded = ((counts + MOE_ROW_TILE - 1) // MOE_ROW_TILE) * MOE_ROW_TILE
    padded_end = jnp.cumsum(padded)
    start_padded = padded_end - padded
    pos = start_padded[flat_e] + rank
    flat_tok = jnp.arange(n_flat, dtype=jnp.int32) // TOP_K
    buf_tok = jnp.zeros((p_rows,), jnp.int32).at[pos].set(flat_tok)
    blk_start = jnp.arange(n_blk, dtype=jnp.int32) * MOE_ROW_TILE
    blk_e = jnp.minimum(jnp.searchsorted(padded_end, blk_start, side='right'),
                        N_EXPERTS - 1).astype(jnp.int32)
    valid = jnp.clip(counts[blk_e] - (blk_start - start_padded[blk_e]), 0, MOE_ROW_TILE)
    valid = jnp.where(blk_start < padded_end[-1], valid, 0)
    half = MOE_ROW_TILE // 2
    n_half = ((valid + half - 1) // half).astype(jnp.int32)
    x_blk = hb.at[buf_tok].get(mode="promise_in_bounds")
    y = moe_experts(x_blk, blk_e, n_half, w_gu, w_down, layer)
    pos2 = pos.reshape(rows, TOP_K)
    y0 = y.at[pos2[:, 0]].get(mode="promise_in_bounds")
    y1 = y.at[pos2[:, 1]].get(mode="promise_in_bounds")
    return combine_ln(h, y0, y1, route, g, b, out_rows)


def _head_norm_gate(o, r, gain):
    mu = jnp.mean(o, axis=-1, keepdims=True)
    oc = o - mu
    var = jnp.mean(oc * oc, axis=-1, keepdims=True)
    on = oc * lax.rsqrt(var + LN_EPS) * gain
    return (on * (r * jax.nn.sigmoid(r))).astype(BF16)


def _masked_cumsum(mask, x):
    m = mask.astype(BF16)
    hi = x.astype(BF16)
    r1 = x - hi.astype(F32)
    mid = r1.astype(BF16)
    lo = (r1 - mid.astype(F32)).astype(BF16)
    return _dot(m, hi) + _dot(m, mid) + _dot(m, lo)


def _gla_first_chunk(q, k, v, r, la, gain):
    c = q.shape[0]
    row = lax.broadcasted_iota(jnp.int32, (c, c), 0)
    col = lax.broadcasted_iota(jnp.int32, (c, c), 1)
    causal = row >= col
    bcum = _masked_cumsum(causal, la)
    q_dec = (q * (GLA_DK_HEAD ** -0.5) * jnp.exp(bcum)).astype(BF16)
    k_inv = (k * jnp.exp(-bcum)).astype(BF16)
    vb = v.astype(BF16)
    att = jnp.where(causal, _dot_nt(q_dec, k_inv), 0.0)
    o = _dot(att.astype(BF16), vb)
    k_state = (k * jnp.exp(bcum[c - 1:c, :] - bcum)).astype(BF16)
    return _head_norm_gate(o, r, gain), _dot_tn(k_state, vb)


def _gla_block(q, k, v, r, la, gain, state):
    n = q.shape[0]
    c = GLA_CHUNK
    row = lax.broadcasted_iota(jnp.int32, (n, n), 0)
    col = lax.broadcasted_iota(jnp.int32, (n, n), 1)
    causal = (row >= col) & ((row // c) == (col // c))
    bcum = _masked_cumsum(causal, la)
    q_dec = (q * (GLA_DK_HEAD ** -0.5) * jnp.exp(bcum)).astype(BF16)
    k_inv = (k * jnp.exp(-bcum)).astype(BF16)
    vb = v.astype(BF16)
    att = jnp.where(causal, _dot_nt(q_dec, k_inv), 0.0)
    o_intra = _dot(att.astype(BF16), vb)
    b_last = [bcum[(ci + 1) * c - 1:(ci + 1) * c, :] for ci in range(n // c)]
    b_tot = jnp.concatenate([jnp.broadcast_to(bl, (c, bl.shape[1])) for bl in b_last], axis=0)
    k_state = (k * jnp.exp(b_tot - bcum)).astype(BF16)
    o_inter = []
    for ci in range(n // c):
        sl = slice(ci * c, (ci + 1) * c)
        o_inter.append(_dot(q_dec[sl], state.astype(BF16)))
        decay = jnp.exp(jnp.transpose(jnp.broadcast_to(b_last[ci], (LANES, b_tot.shape[1]))))
        decay = jnp.concatenate([decay] * (GLA_DV_HEAD // LANES), axis=1)
        state = state * decay + _dot_tn(k_state[sl], vb[sl])
    o = o_intra + jnp.concatenate(o_inter, axis=0)
    return _head_norm_gate(o, r, gain), state


def _gla_kernel(qm_ref, km_ref, vm_ref, rm_ref, lam_ref, q_ref, k_ref, v_ref, r_ref, la_ref,
                gain_ref, o_ref, state_ref, *, blocks_per_seq):
    s = pl.program_id(1)
    gain = gain_ref[...]
    n_batch = state_ref.shape[0]

    @pl.when(s == 0)
    def _():
        for bi in range(n_batch):
            sl = slice(bi * N_META, (bi + 1) * N_META)
            og, st = _gla_first_chunk(qm_ref[sl, :], km_ref[sl, :], vm_ref[sl, :], rm_ref[sl, :],
                                      lam_ref[sl, :], gain)
            state_ref[bi] = st
            o_ref[sl, :] = og

    @pl.when(s > 0)
    def _():
        bi = (s - 1) // blocks_per_seq
        og, st = _gla_block(q_ref[...], k_ref[...], v_ref[...], r_ref[...], la_ref[...], gain,
                            state_ref[bi])
        o_ref[...] = og
        state_ref[bi] = st


def gla_mix(qkvr, log_a, norm_gain, n_batch, seq):
    rows = qkvr.shape[0]
    n_real = n_batch * seq
    meta_rows = n_batch * N_META
    blocks_per_seq = seq // GLA_ROWS
    n_steps = 1 + n_real // GLA_ROWS
    meta_blk = n_real // meta_rows
    meta_out_blk = n_real // GLA_ROWS
    dk, dv, nh = GLA_DK_HEAD, GLA_DV_HEAD, GLA_HEADS
    kq, kk, kv_, kr = 0, GLA_DK // dk, 2 * GLA_DK // dv, (2 * GLA_DK + GLA_DV) // dv

    def real_blk(s):
        return jnp.maximum(s - 1, 0)

    def meta_spec(width, col0):
        return pl.BlockSpec((meta_rows, width), lambda h, s: (meta_blk, col0 + h))

    def real_spec(width, col0):
        return pl.BlockSpec((GLA_ROWS, width), lambda h, s: (real_blk(s), col0 + h))

    return pl.pallas_call(
        functools.partial(_gla_kernel, blocks_per_seq=blocks_per_seq),
        grid=(nh, n_steps),
        in_specs=[meta_spec(dk, kq), meta_spec(dk, kk), meta_spec(dv, kv_), meta_spec(dv, kr),
                  meta_spec(dk, 0),
                  real_spec(dk, kq), real_spec(dk, kk), real_spec(dv, kv_), real_spec(dv, kr),
                  real_spec(dk, 0),
                  pl.BlockSpec((1, dv), lambda h, s: (0, h))],
        out_specs=pl.BlockSpec((GLA_ROWS, dv),
                               lambda h, s: (jnp.where(s == 0, meta_out_blk, s - 1), h)),
        out_shape=jax.ShapeDtypeStruct((rows, GLA_DV), BF16),
        scratch_shapes=[pltpu.VMEM((n_batch, dk, dv), F32)],
        compiler_params=_params("parallel", "arbitrary"),
        name="gla_mix",
    )(qkvr, qkvr, qkvr, qkvr, log_a, qkvr, qkvr, qkvr, qkvr, log_a, norm_gain)


def _block_diag2(x2):
    lane = lax.broadcasted_iota(jnp.int32, x2.shape, 1)
    zero = jnp.zeros_like(x2)
    return jnp.concatenate([jnp.where(lane < SWA_HEAD_DIM, x2, zero),
                            jnp.where(lane >= SWA_HEAD_DIM, x2, zero)], axis=0)


def _swa_kernel(q_ref, km_ref, kp_ref, kc_ref, vm_ref, vp_ref, vc_ref,
                bm_ref, bp_ref, bc_ref, o_ref, *, blocks_per_seq):
    s = pl.program_id(1)
    bi = jnp.maximum(s - 1, 0) // blocks_per_seq
    m0 = pl.multiple_of(bi * N_META, N_META)
    pad = jnp.zeros((SWA_BLOCK - N_META, LANES), BF16)
    k_m = jnp.concatenate([km_ref[pl.ds(m0, N_META), :], pad], axis=0)
    v_m = jnp.concatenate([vm_ref[pl.ds(m0, N_META), :], pad], axis=0)
    q = q_ref[...]
    qp = jnp.concatenate([q[:, p * LANES:(p + 1) * LANES] for p in range(SWA_PAIRS)], axis=0)

    s_m = _dot_nt(qp, _block_diag2(k_m)) + bm_ref[0, 0]
    s_p = _dot_nt(qp, _block_diag2(kp_ref[...])) + bp_ref[0, 0]
    s_c = _dot_nt(qp, _block_diag2(kc_ref[...])) + bc_ref[0, 0]
    top = jnp.maximum(jnp.maximum(s_m, s_p), s_c)
    m_even = jnp.max(top[:, :LANES], axis=-1, keepdims=True)
    m_odd = jnp.max(top[:, LANES:], axis=-1, keepdims=True)
    lane2 = lax.broadcasted_iota(jnp.int32, top.shape, 1)
    m_full = jnp.where(lane2 < LANES, m_even, m_odd)
    p_m, p_p, p_c = jnp.exp(s_m - m_full), jnp.exp(s_p - m_full), jnp.exp(s_c - m_full)
    p_sum = p_m + p_p + p_c
    l_even = jnp.sum(p_sum[:, :LANES], axis=-1, keepdims=True)
    l_odd = jnp.sum(p_sum[:, LANES:], axis=-1, keepdims=True)
    o = (_dot(p_m.astype(BF16), _block_diag2(v_m))
         + _dot(p_p.astype(BF16), _block_diag2(vp_ref[...]))
         + _dot(p_c.astype(BF16), _block_diag2(vc_ref[...])))
    lane1 = lax.broadcasted_iota(jnp.int32, o.shape, 1)
    o = o / jnp.where(lane1 < SWA_HEAD_DIM, l_even, l_odd)
    for p in range(SWA_PAIRS):
        o_ref[:, p * LANES:(p + 1) * LANES] = o[p * SWA_BLOCK:(p + 1) * SWA_BLOCK].astype(o_ref.dtype)


def swa_mix(q, kv2, bias_m, bias_p, bias_c, n_batch, seq):
    rows = q.shape[0]
    blocks_per_seq = seq // SWA_BLOCK
    n_real_blk = n_batch * blocks_per_seq
    n_steps = 1 + n_real_blk
    meta_blk = n_real_blk
    assert n_batch * N_META == SWA_BLOCK
    kvn, qw = SWA_KV_HEADS, SWA_GROUP * SWA_HEAD_DIM

    def cur_blk(s):
        return jnp.where(s == 0, meta_blk, s - 1)

    def prev_blk(s):
        return jnp.maximum(s - 2, 0)

    def variant(s):
        return jnp.where(s == 0, 2, jnp.where((s - 1) % blocks_per_seq == 0, 0, 1))

    def kv_spec(head0, blk_fn):
        return pl.BlockSpec((SWA_BLOCK, LANES), lambda kh, s: (blk_fn(s), head0 + kh))

    bias_spec = pl.BlockSpec((1, 1, SWA_PAIRS * SWA_BLOCK, 2 * SWA_BLOCK),
                             lambda kh, s: (variant(s), kh, 0, 0))
    return pl.pallas_call(
        functools.partial(_swa_kernel, blocks_per_seq=blocks_per_seq),
        grid=(kvn, n_steps),
        in_specs=[pl.BlockSpec((SWA_BLOCK, qw), lambda kh, s: (cur_blk(s), kh)),
                  kv_spec(0, lambda s: meta_blk), kv_spec(0, prev_blk), kv_spec(0, cur_blk),
                  kv_spec(kvn, lambda s: meta_blk), kv_spec(kvn, prev_blk), kv_spec(kvn, cur_blk),
                  bias_spec, bias_spec, bias_spec],
        out_specs=pl.BlockSpec((SWA_BLOCK, qw), lambda kh, s: (cur_blk(s), kh)),
        out_shape=jax.ShapeDtypeStruct((rows, kvn * qw), BF16),
        compiler_params=_params("parallel", "arbitrary"),
        name="swa_mix",
    )(q, kv2, kv2, kv2, kv2, kv2, kv2, bias_m, bias_p, bias_c)


def _t5_bucket(dist):
    exact = REL_BUCKETS // 2
    d = jnp.maximum(dist, 0)
    df = jnp.maximum(d, 1).astype(F32)
    large = exact + (jnp.log(df / exact) / math.log(REL_MAX_DIST / exact)
                     * (REL_BUCKETS - exact)).astype(jnp.int32)
    large = jnp.minimum(large, REL_BUCKETS - 1)
    return jnp.where(d < exact, d, large)


def _pair_layout(bias):
    nv = bias.shape[0]
    b = bias.reshape(nv, SWA_KV_HEADS, SWA_PAIRS, 2, SWA_BLOCK, SWA_BLOCK)
    b = jnp.transpose(b, (0, 1, 2, 4, 3, 5))
    return b.reshape(nv, SWA_KV_HEADS, SWA_PAIRS * SWA_BLOCK, 2 * SWA_BLOCK)


def _swa_bias_static(table):
    tab = table.astype(F32)
    blk = SWA_BLOCK
    i = jnp.arange(blk, dtype=jnp.int32)[:, None]
    j = jnp.arange(blk, dtype=jnp.int32)[None, :]
    m = jnp.arange(N_META, dtype=jnp.int32)[None, :]

    def lookup(dist):
        return jnp.transpose(tab[_t5_bucket(dist)], (2, 0, 1))

    neg = jnp.full((SWA_Q_HEADS, blk, blk), NEG_INF, F32)
    cur = jnp.where((i - j >= 0)[None], lookup(i - j), NEG_INF)
    d_prev = blk + i - j
    prev = jnp.where(((d_prev >= 0) & (d_prev < SWA_WINDOW))[None], lookup(d_prev), NEG_INF)
    meta0 = lookup(N_META + i - m)
    meta1 = lookup(N_META + blk + i - m)
    same_seq = (i // N_META) == (j // N_META)
    dm = (i % N_META) - (j % N_META)
    meta_tile = jnp.where((same_seq & (dm >= 0))[None], lookup(dm), NEG_INF)
    bias_p = _pair_layout(jnp.stack([neg, prev, neg]))
    bias_c = _pair_layout(jnp.stack([cur, cur, meta_tile]))
    return bias_p, bias_c, jnp.stack([meta0, meta1])


def _swa_bias_meta(meta01, sinks):
    nh, blk = SWA_Q_HEADS, SWA_BLOCK
    sink_col = jnp.broadcast_to(sinks.astype(F32)[None, :, None, None], (3, nh, blk, 1))
    meta = jnp.concatenate([meta01, jnp.full((1, nh, blk, N_META), NEG_INF, F32)], axis=0)
    rest = jnp.full((3, nh, blk, blk - N_META - 1), NEG_INF, F32)
    return _pair_layout(jnp.concatenate([meta, sink_col, rest], axis=-1))


def kernel(x, meta_tokens, rel_bias_table, ln_gain, ln_bias, gla_w_in, gla_w_gate2, gla_b_gate,
           gla_norm_gain, gla_w_out, kv_w_shared, swa_w_q, swa_sinks, swa_w_out,
           ffn_w_gate_up, ffn_w_down, moe_w_router, moe_w_gate_up, moe_w_down):
    bsz, seq, d = x.shape
    n_real = bsz * seq
    h = jnp.concatenate([x.reshape(n_real, d),
                         jnp.broadcast_to(meta_tokens.astype(x.dtype)[None], (bsz, N_META, d))
                         .reshape(bsz * N_META, d)], axis=0)
    rows = h.shape[0]
    hb = h.astype(BF16)
    bias_p, bias_c, meta01 = _swa_bias_static(rel_bias_table)

    w_in_b = gla_w_in.astype(BF16)
    w_gla_out_b = gla_w_out.astype(BF16)
    w_q_b = swa_w_q.astype(BF16)
    w_swa_out_b = swa_w_out.astype(BF16)
    w_ffn_gu_b = ffn_w_gate_up.astype(BF16)
    w_ffn_down_b = ffn_w_down.astype(BF16)
    hd = SWA_HEAD_DIM
    w_kv = kv_w_shared.reshape(d, 2 * SWA_KV_HEADS, 1, hd)
    w_kv2 = jnp.broadcast_to(w_kv, (d, 2 * SWA_KV_HEADS, 2, hd)).reshape(1, d, 4 * SWA_KV_HEADS * hd)
    w_kv2 = w_kv2.astype(BF16)

    kv2 = None
    for li in range(DEPTH):
        g0, b0 = ln_gain[li, 0][None, :], ln_bias[li, 0][None, :]
        g1, b1 = ln_gain[li, 1][None, :], ln_bias[li, 1][None, :]
        if li < N_A_LAYERS:
            qkvr = linear(hb, w_in_b, li, GLA_MAIN, F32)
            wg_pad = jnp.pad(w_in_b[li, :, GLA_MAIN:], ((0, 0), (0, LANES - GLA_GATE_RANK)))
            w2_pad = jnp.pad(gla_w_gate2[li].astype(BF16), ((0, LANES - GLA_GATE_RANK), (0, 0)))
            log_a = gla_gate(hb, wg_pad, w2_pad, gla_b_gate[li][None, :])
            mix_in = gla_mix(qkvr, log_a, gla_norm_gain[li][None, :], bsz, seq)
            h, hb = linear_res_ln(mix_in, w_gla_out_b, li, h, g0, b0)
        else:
            jb = li - N_A_LAYERS
            q = linear(hb, w_q_b, jb, d, BF16, scale=hd ** -0.5)
            bias_m = _swa_bias_meta(meta01, swa_sinks[jb])
            mix_in = swa_mix(q, kv2, bias_m, bias_p, bias_c, bsz, seq)
            h, hb = linear_res_ln(mix_in, w_swa_out_b, jb, h, g0, b0)
        if li % 2 == 0:
            h, hb = ffn_res_ln(hb, h, w_ffn_gu_b, w_ffn_down_b, li // 2, g1, b1)
        else:
            w_r = jnp.pad(moe_w_router[li // 2].astype(BF16), ((0, 0), (0, LANES - N_EXPERTS)))
            out_rows = n_real if li == DEPTH - 1 else rows
            h, hb = moe_res_ln(h, hb, w_r, moe_w_gate_up, moe_w_down, li // 2, g1, b1, out_rows)
        if li == N_A_LAYERS - 1:
            kv2 = linear(hb, w_kv2, 0, w_kv2.shape[-1], BF16)
    return h.reshape(bsz, seq, d)
```

```python
import functools
import math

import jax
import jax.numpy as jnp
from jax import lax
from jax.experimental import pallas as pl
from jax.experimental.pallas import tpu as pltpu

F32 = jnp.float32
BF16 = jnp.bfloat16

D_MODEL = 2048
DEPTH = 4
N_META = 16
N_A_LAYERS = DEPTH // 2
DN_ALPHA = (2 * DEPTH) ** 0.25
LN_EPS = 1e-5

GLA_HEADS = 4
GLA_DK = D_MODEL // 2
GLA_DV = D_MODEL
GLA_DK_HEAD = GLA_DK // GLA_HEADS
GLA_DV_HEAD = GLA_DV // GLA_HEADS
GLA_GATE_RANK = 16
GLA_GATE_TAU = 16.0
GLA_CHUNK = 64
GLA_MAIN = 2 * GLA_DK + 2 * GLA_DV

SWA_HEAD_DIM = 64
SWA_Q_HEADS = D_MODEL // SWA_HEAD_DIM
SWA_GROUP = 8
SWA_KV_HEADS = SWA_Q_HEADS // SWA_GROUP
SWA_PAIRS = SWA_GROUP // 2
SWA_WINDOW = 128
SWA_BLOCK = 128

REL_BUCKETS = 32
REL_MAX_DIST = 128

FFN_DIM = 7 * D_MODEL // 2
N_EXPERTS = 8
TOP_K = 2
NEG_INF = -1e9

LANES = 128
VMEM_LIMIT = 56 * 1024 * 1024
ROW_TILE = 512
PROJ_ROW_TILE = 1376
PROJ_COL_TILE = 1024
FFN_TILE = 512
MOE_ROW_TILE = 1024
GLA_ROWS = 256


def _params(*sem):
    return pltpu.CompilerParams(dimension_semantics=sem, vmem_limit_bytes=VMEM_LIMIT)


def _layer_norm_rows(y, g, b):
    mu = jnp.mean(y, axis=-1, keepdims=True)
    yc = y - mu
    var = jnp.mean(yc * yc, axis=-1, keepdims=True)
    return yc * lax.rsqrt(var + LN_EPS) * g + b


def _dot(a, b):
    return jnp.dot(a, b, preferred_element_type=F32)


def _dot_nt(a, b):
    return lax.dot_general(a, b, (((1,), (1,)), ((), ())), preferred_element_type=F32)


def _dot_tn(a, b, precision=None):
    return lax.dot_general(a, b, (((0,), (0,)), ((), ())), preferred_element_type=F32,
                           precision=precision)


def _linear_kernel(x_ref, w_ref, o_ref, *, scale):
    acc = _dot(x_ref[...], w_ref[...])
    if scale != 1.0:
        acc = acc * scale
    o_ref[...] = acc.astype(o_ref.dtype)


def linear(x, w, layer, n_out, out_dtype, scale=1.0):
    rows, k = x.shape
    tm = PROJ_ROW_TILE if rows % PROJ_ROW_TILE == 0 else ROW_TILE
    tn = PROJ_COL_TILE
    return pl.pallas_call(
        functools.partial(_linear_kernel, scale=scale),
        grid=(pl.cdiv(rows, tm), n_out // tn),
        in_specs=[pl.BlockSpec((tm, k), lambda i, j: (i, 0)),
                  pl.BlockSpec((None, k, tn), lambda i, j: (layer, 0, j))],
        out_specs=pl.BlockSpec((tm, tn), lambda i, j: (i, j)),
        out_shape=jax.ShapeDtypeStruct((rows, n_out), out_dtype),
        compiler_params=_params("parallel", "arbitrary"),
        name="linear",
    )(x, w)


def _gla_gate_kernel(x_ref, wg_ref, w2_ref, b_ref, o_ref):
    g_low = _dot(x_ref[...], wg_ref[...])
    z = _dot(g_low.astype(BF16), w2_ref[...]) + b_ref[...]
    log_sig = jnp.minimum(z, 0.0) - jnp.log1p(jnp.exp(-jnp.abs(z)))
    o_ref[...] = log_sig / GLA_GATE_TAU


def gla_gate(x, wg_pad, w2_pad, b_gate):
    rows, k = x.shape
    return pl.pallas_call(
        _gla_gate_kernel,
        grid=(pl.cdiv(rows, ROW_TILE),),
        in_specs=[pl.BlockSpec((ROW_TILE, k), lambda i: (i, 0)),
                  pl.BlockSpec((k, LANES), lambda i: (0, 0)),
                  pl.BlockSpec((LANES, GLA_DK), lambda i: (0, 0)),
                  pl.BlockSpec((1, GLA_DK), lambda i: (0, 0))],
        out_specs=pl.BlockSpec((ROW_TILE, GLA_DK), lambda i: (i, 0)),
        out_shape=jax.ShapeDtypeStruct((rows, GLA_DK), F32),
        compiler_params=_params("parallel"),
        name="gla_gate",
    )(x, wg_pad, w2_pad, b_gate)


def _linear_res_ln_kernel(x_ref, w_ref, res_ref, g_ref, b_ref, o_ref, ob_ref):
    y = DN_ALPHA * res_ref[...] + _dot(x_ref[...], w_ref[...])
    o = _layer_norm_rows(y, g_ref[...], b_ref[...])
    o_ref[...] = o
    ob_ref[...] = o.astype(BF16)


def linear_res_ln(x, w, layer, res, g, b):
    rows, k = x.shape
    d = w.shape[-1]
    return pl.pallas_call(
        _linear_res_ln_kernel,
        grid=(pl.cdiv(rows, ROW_TILE),),
        in_specs=[pl.BlockSpec((ROW_TILE, k), lambda i: (i, 0)),
                  pl.BlockSpec((None, k, d), lambda i: (layer, 0, 0)),
                  pl.BlockSpec((ROW_TILE, d), lambda i: (i, 0)),
                  pl.BlockSpec((1, d), lambda i: (0, 0)),
                  pl.BlockSpec((1, d), lambda i: (0, 0))],
        out_specs=[pl.BlockSpec((ROW_TILE, d), lambda i: (i, 0)),
                   pl.BlockSpec((ROW_TILE, d), lambda i: (i, 0))],
        out_shape=[jax.ShapeDtypeStruct((rows, d), F32), jax.ShapeDtypeStruct((rows, d), BF16)],
        compiler_params=_params("parallel"),
        name="linear_res_ln",
    )(x, w, res, g, b)


def _swiglu_partial(x, wa, wu, wd):
    a = _dot(x, wa)
    u = _dot(x, wu)
    hidden = (a * jax.nn.sigmoid(a) * u).astype(BF16)
    return _dot(hidden, wd)


def _ffn_kernel(x_ref, wa_ref, wu_ref, wd_ref, res_ref, g_ref, b_ref, o_ref, ob_ref, acc_ref):
    j = pl.program_id(1)

    @pl.when(j == 0)
    def _():
        acc_ref[...] = jnp.zeros_like(acc_ref)

    acc_ref[...] += _swiglu_partial(x_ref[...], wa_ref[...], wu_ref[...], wd_ref[...])

    @pl.when(j == pl.num_programs(1) - 1)
    def _():
        o = _layer_norm_rows(DN_ALPHA * res_ref[...] + acc_ref[...], g_ref[...], b_ref[...])
        o_ref[...] = o
        ob_ref[...] = o.astype(BF16)


def ffn_res_ln(xb, res, w_gu, w_down, layer, g, b):
    rows, d = xb.shape
    f = w_down.shape[-2]
    nj = f // FFN_TILE
    return pl.pallas_call(
        _ffn_kernel,
        grid=(pl.cdiv(rows, ROW_TILE), nj),
        in_specs=[pl.BlockSpec((ROW_TILE, d), lambda i, j: (i, 0)),
                  pl.BlockSpec((None, d, FFN_TILE), lambda i, j: (layer, 0, j)),
                  pl.BlockSpec((None, d, FFN_TILE), lambda i, j: (layer, 0, nj + j)),
                  pl.BlockSpec((None, FFN_TILE, d), lambda i, j: (layer, j, 0)),
                  pl.BlockSpec((ROW_TILE, d), lambda i, j: (i, 0)),
                  pl.BlockSpec((1, d), lambda i, j: (0, 0)),
                  pl.BlockSpec((1, d), lambda i, j: (0, 0))],
        out_specs=[pl.BlockSpec((ROW_TILE, d), lambda i, j: (i, 0)),
                   pl.BlockSpec((ROW_TILE, d), lambda i, j: (i, 0))],
        out_shape=[jax.ShapeDtypeStruct((rows, d), F32), jax.ShapeDtypeStruct((rows, d), BF16)],
        scratch_shapes=[pltpu.VMEM((ROW_TILE, d), F32)],
        compiler_params=_params("parallel", "arbitrary"),
        name="ffn_res_ln",
    )(xb, w_gu, w_gu, w_down, res, g, b)


def _router_kernel(x_ref, w_ref, o_ref):
    logits = _dot(x_ref[...], w_ref[...])
    lane = lax.broadcasted_iota(jnp.int32, logits.shape, 1)
    lane_f = lane.astype(F32)
    logits = jnp.where(lane < N_EXPERTS, logits, -jnp.inf)
    m1 = jnp.max(logits, axis=-1, keepdims=True)
    i1 = jnp.min(jnp.where(logits == m1, lane_f, float(LANES)), axis=-1, keepdims=True)
    rest = jnp.where(lane_f == i1, -jnp.inf, logits)
    m2 = jnp.max(rest, axis=-1, keepdims=True)
    i2 = jnp.min(jnp.where(rest == m2, lane_f, float(LANES)), axis=-1, keepdims=True)
    e2 = jnp.exp(m2 - m1)
    g1 = 1.0 / (1.0 + e2)
    g2 = e2 / (1.0 + e2)
    o_ref[...] = jnp.where(lane == 0, i1, jnp.where(lane == 1, i2,
                           jnp.where(lane == 2, g1, jnp.where(lane == 3, g2, 0.0))))


def router(xb, w_router_pad):
    rows, d = xb.shape
    return pl.pallas_call(
        _router_kernel,
        grid=(pl.cdiv(rows, ROW_TILE),),
        in_specs=[pl.BlockSpec((ROW_TILE, d), lambda i: (i, 0)),
                  pl.BlockSpec((d, LANES), lambda i: (0, 0))],
        out_specs=pl.BlockSpec((ROW_TILE, LANES), lambda i: (i, 0)),
        out_shape=jax.ShapeDtypeStruct((rows, LANES), F32),
        compiler_params=_params("parallel"),
        name="router",
    )(xb, w_router_pad)


def _moe_kernel(blk_e_ref, n_half_ref, tok_ref, h_hbm, wa_ref, wu_ref, wd_ref, o_ref,
                stage_ref, x_ref, sem):
    del blk_e_ref
    i = pl.program_id(0)
    j = pl.program_id(1)
    n_half = n_half_ref[i]
    half = MOE_ROW_TILE // 2

    def gather_half(hf):
        base = i * MOE_ROW_TILE + hf * half

        def row_copy(src_row, dst_row):
            return pltpu.make_async_copy(h_hbm.at[pl.ds(src_row, 1), :],
                                         stage_ref.at[pl.ds(dst_row, 1), :], sem)

        def issue(r, carry):
            row_copy(tok_ref[base + r], r).start()
            return carry

        lax.fori_loop(0, half, issue, 0, unroll=8)
        pltpu.make_async_copy(h_hbm.at[pl.ds(0, half), :], stage_ref, sem).wait()
        x_ref[hf * half:(hf + 1) * half, :] = stage_ref[...].astype(BF16)

    @pl.when(j == 0)
    def _():
        o_ref[...] = jnp.zeros_like(o_ref)

        @pl.when(n_half > 0)
        def _():
            gather_half(0)

        @pl.when(n_half > 1)
        def _():
            gather_half(1)

    @pl.when(n_half > 0)
    def _():
        wa = wa_ref[...].astype(BF16)
        wu = wu_ref[...].astype(BF16)
        wd = wd_ref[...].astype(BF16)
        o_ref[:half, :] += _swiglu_partial(x_ref[:half, :], wa, wu, wd)

        @pl.when(n_half > 1)
        def _():
            o_ref[half:, :] += _swiglu_partial(x_ref[half:, :], wa, wu, wd)


def moe_experts(h, buf_tok, blk_e, n_half, w_gu, w_down, layer):
    d = h.shape[1]
    p_rows = buf_tok.shape[0]
    f = w_down.shape[-2]
    nj = f // FFN_TILE
    n_blk = p_rows // MOE_ROW_TILE

    def jj(i, j, nh):
        return jnp.where(nh[i] > 0, j, nj - 1)

    grid_spec = pltpu.PrefetchScalarGridSpec(
        num_scalar_prefetch=3,
        grid=(n_blk, nj),
        in_specs=[pl.BlockSpec(memory_space=pl.ANY),
                  pl.BlockSpec((None, None, d, FFN_TILE),
                               lambda i, j, be, nh, tk: (layer, be[i], 0, jj(i, j, nh))),
                  pl.BlockSpec((None, None, d, FFN_TILE),
                               lambda i, j, be, nh, tk: (layer, be[i], 0, nj + jj(i, j, nh))),
                  pl.BlockSpec((None, None, FFN_TILE, d),
                               lambda i, j, be, nh, tk: (layer, be[i], jj(i, j, nh), 0))],
        out_specs=pl.BlockSpec((MOE_ROW_TILE, d), lambda i, j, be, nh, tk: (i, 0),
                               pipeline_mode=pl.Buffered(1)),
        scratch_shapes=[pltpu.VMEM((MOE_ROW_TILE // 2, d), F32),
                        pltpu.VMEM((MOE_ROW_TILE, d), BF16),
                        pltpu.SemaphoreType.DMA(())],
    )
    return pl.pallas_call(
        _moe_kernel,
        grid_spec=grid_spec,
        out_shape=jax.ShapeDtypeStruct((p_rows, d), F32),
        compiler_params=_params("arbitrary", "arbitrary"),
        name="moe_experts",
    )(blk_e, n_half, buf_tok, h, w_gu, w_gu, w_down)


def _combine_ln_kernel(res_ref, y0_ref, y1_ref, gate_ref, g_ref, b_ref, o_ref, ob_ref):
    gate = gate_ref[...]
    moe = y0_ref[...] * gate[:, 2:3] + y1_ref[...] * gate[:, 3:4]
    o = _layer_norm_rows(DN_ALPHA * res_ref[...] + moe, g_ref[...], b_ref[...])
    o_ref[...] = o
    ob_ref[...] = o.astype(BF16)


def combine_ln(res, y0, y1, route, g, b, out_rows):
    d = res.shape[1]
    row_spec = pl.BlockSpec((ROW_TILE, d), lambda i: (i, 0))
    vec_spec = pl.BlockSpec((1, d), lambda i: (0, 0))
    return pl.pallas_call(
        _combine_ln_kernel,
        grid=(pl.cdiv(out_rows, ROW_TILE),),
        in_specs=[row_spec, row_spec, row_spec, pl.BlockSpec((ROW_TILE, LANES), lambda i: (i, 0)),
                  vec_spec, vec_spec],
        out_specs=[row_spec, row_spec],
        out_shape=[jax.ShapeDtypeStruct((out_rows, d), F32),
                   jax.ShapeDtypeStruct((out_rows, d), BF16)],
        compiler_params=_params("parallel"),
        name="combine_ln",
    )(res, y0, y1, route, g, b)


def moe_res_ln(h, hb, w_router_pad, w_gu, w_down, layer, g, b, out_rows):
    rows, d = h.shape
    route = router(hb, w_router_pad)
    flat_e = route[:, :TOP_K].astype(jnp.int32).reshape(-1)
    n_flat = rows * TOP_K
    n_blk = -(-n_flat // MOE_ROW_TILE) + N_EXPERTS
    p_rows = n_blk * MOE_ROW_TILE
    onehot = (flat_e[:, None] == jnp.arange(N_EXPERTS, dtype=jnp.int32)[None, :]).astype(jnp.int32)
    csum = jnp.cumsum(onehot, axis=0)
    rank = jnp.sum(csum * onehot, axis=1) - 1
    counts = csum[-1]
    padded = ((counts + MOE_ROW_TILE - 1) // MOE_ROW_TILE) * MOE_ROW_TILE
    padded_end = jnp.cumsum(padded)
    start_padded = padded_end - padded
    pos = start_padded[flat_e] + rank
    flat_tok = jnp.arange(n_flat, dtype=jnp.int32) // TOP_K
    buf_tok = jnp.zeros((p_rows,), jnp.int32).at[pos].set(flat_tok)
    blk_start = jnp.arange(n_blk, dtype=jnp.int32) * MOE_ROW_TILE
    blk_e = jnp.minimum(jnp.searchsorted(padded_end, blk_start, side='right'),
                        N_EXPERTS - 1).astype(jnp.int32)
    valid = jnp.clip(counts[blk_e] - (blk_start - start_padded[blk_e]), 0, MOE_ROW_TILE)
    valid = jnp.where(blk_start < padded_end[-1], valid, 0)
    half = MOE_ROW_TILE // 2
    n_half = ((valid + half - 1) // half).astype(jnp.int32)
    y = moe_experts(h, buf_tok, blk_e, n_half, w_gu, w_down, layer)
    pos2 = pos.reshape(rows, TOP_K)
    y0 = y.at[pos2[:, 0]].get(mode="promise_in_bounds")
    y1 = y.at[pos2[:, 1]].get(mode="promise_in_bounds")
    return combine_ln(h, y0, y1, route, g, b, out_rows)


def _head_norm_gate(o, r, gain):
    mu = jnp.mean(o, axis=-1, keepdims=True)
    oc = o - mu
    var = jnp.mean(oc * oc, axis=-1, keepdims=True)
    on = oc * lax.rsqrt(var + LN_EPS) * gain
    return (on * (r * jax.nn.sigmoid(r))).astype(BF16)


def _masked_cumsum(mask, x):
    m = mask.astype(BF16)
    hi = x.astype(BF16)
    r1 = x - hi.astype(F32)
    mid = r1.astype(BF16)
    lo = (r1 - mid.astype(F32)).astype(BF16)
    return _dot(m, hi) + _dot(m, mid) + _dot(m, lo)


def _gla_first_chunk(q, k, v, r, la, gain):
    c = q.shape[0]
    row = lax.broadcasted_iota(jnp.int32, (c, c), 0)
    col = lax.broadcasted_iota(jnp.int32, (c, c), 1)
    causal = row >= col
    bcum = _masked_cumsum(causal, la)
    q_dec = (q * (GLA_DK_HEAD ** -0.5) * jnp.exp(bcum)).astype(BF16)
    k_inv = (k * jnp.exp(-bcum)).astype(BF16)
    vb = v.astype(BF16)
    att = jnp.where(causal, _dot_nt(q_dec, k_inv), 0.0)
    o = _dot(att.astype(BF16), vb)
    k_state = (k * jnp.exp(bcum[c - 1:c, :] - bcum)).astype(BF16)
    return _head_norm_gate(o, r, gain), _dot_tn(k_state, vb)


def _gla_block(q, k, v, r, la, gain, state):
    n = q.shape[0]
    c = GLA_CHUNK
    row = lax.broadcasted_iota(jnp.int32, (n, n), 0)
    col = lax.broadcasted_iota(jnp.int32, (n, n), 1)
    causal = (row >= col) & ((row // c) == (col // c))
    bcum = _masked_cumsum(causal, la)
    q_dec = (q * (GLA_DK_HEAD ** -0.5) * jnp.exp(bcum)).astype(BF16)
    k_inv = (k * jnp.exp(-bcum)).astype(BF16)
    vb = v.astype(BF16)
    att = jnp.where(causal, _dot_nt(q_dec, k_inv), 0.0)
    o_intra = _dot(att.astype(BF16), vb)
    b_last = [bcum[(ci + 1) * c - 1:(ci + 1) * c, :] for ci in range(n // c)]
    b_tot = jnp.concatenate([jnp.broadcast_to(bl, (c, bl.shape[1])) for bl in b_last], axis=0)
    k_state = (k * jnp.exp(b_tot - bcum)).astype(BF16)
    o_inter = []
    for ci in range(n // c):
        sl = slice(ci * c, (ci + 1) * c)
        o_inter.append(_dot(q_dec[sl], state.astype(BF16)))
        decay = jnp.exp(jnp.transpose(jnp.broadcast_to(b_last[ci], (LANES, b_tot.shape[1]))))
        decay = jnp.concatenate([decay] * (GLA_DV_HEAD // LANES), axis=1)
        state = state * decay + _dot_tn(k_state[sl], vb[sl])
    o = o_intra + jnp.concatenate(o_inter, axis=0)
    return _head_norm_gate(o, r, gain), state


def _gla_kernel(qm_ref, km_ref, vm_ref, rm_ref, lam_ref, q_ref, k_ref, v_ref, r_ref, la_ref,
                gain_ref, o_ref, state_ref, *, blocks_per_seq):
    s = pl.program_id(1)
    gain = gain_ref[...]
    n_batch = state_ref.shape[0]

    @pl.when(s == 0)
    def _():
        for bi in range(n_batch):
            sl = slice(bi * N_META, (bi + 1) * N_META)
            og, st = _gla_first_chunk(qm_ref[sl, :], km_ref[sl, :], vm_ref[sl, :], rm_ref[sl, :],
                                      lam_ref[sl, :], gain)
            state_ref[bi] = st
            o_ref[sl, :] = og

    @pl.when(s > 0)
    def _():
        bi = (s - 1) // blocks_per_seq
        og, st = _gla_block(q_ref[...], k_ref[...], v_ref[...], r_ref[...], la_ref[...], gain,
                            state_ref[bi])
        o_ref[...] = og
        state_ref[bi] = st


def gla_mix(qkvr, log_a, norm_gain, n_batch, seq):
    rows = qkvr.shape[0]
    n_real = n_batch * seq
    meta_rows = n_batch * N_META
    blocks_per_seq = seq // GLA_ROWS
    n_steps = 1 + n_real // GLA_ROWS
    meta_blk = n_real // meta_rows
    meta_out_blk = n_real // GLA_ROWS
    dk, dv, nh = GLA_DK_HEAD, GLA_DV_HEAD, GLA_HEADS
    kq, kk, kv_, kr = 0, GLA_DK // dk, 2 * GLA_DK // dv, (2 * GLA_DK + GLA_DV) // dv

    def real_blk(s):
        return jnp.maximum(s - 1, 0)

    def meta_spec(width, col0):
        return pl.BlockSpec((meta_rows, width), lambda h, s: (meta_blk, col0 + h))

    def real_spec(width, col0):
        return pl.BlockSpec((GLA_ROWS, width), lambda h, s: (real_blk(s), col0 + h))

    return pl.pallas_call(
        functools.partial(_gla_kernel, blocks_per_seq=blocks_per_seq),
        grid=(nh, n_steps),
        in_specs=[meta_spec(dk, kq), meta_spec(dk, kk), meta_spec(dv, kv_), meta_spec(dv, kr),
                  meta_spec(dk, 0),
                  real_spec(dk, kq), real_spec(dk, kk), real_spec(dv, kv_), real_spec(dv, kr),
                  real_spec(dk, 0),
                  pl.BlockSpec((1, dv), lambda h, s: (0, h))],
        out_specs=pl.BlockSpec((GLA_ROWS, dv),
                               lambda h, s: (jnp.where(s == 0, meta_out_blk, s - 1), h)),
        out_shape=jax.ShapeDtypeStruct((rows, GLA_DV), BF16),
        scratch_shapes=[pltpu.VMEM((n_batch, dk, dv), F32)],
        compiler_params=_params("parallel", "arbitrary"),
        name="gla_mix",
    )(qkvr, qkvr, qkvr, qkvr, log_a, qkvr, qkvr, qkvr, qkvr, log_a, norm_gain)


def _block_diag2(x2):
    lane = lax.broadcasted_iota(jnp.int32, x2.shape, 1)
    zero = jnp.zeros_like(x2)
    return jnp.concatenate([jnp.where(lane < SWA_HEAD_DIM, x2, zero),
                            jnp.where(lane >= SWA_HEAD_DIM, x2, zero)], axis=0)


def _swa_kernel(q_ref, km_ref, kp_ref, kc_ref, vm_ref, vp_ref, vc_ref,
                bm_ref, bp_ref, bc_ref, o_ref, *, blocks_per_seq):
    s = pl.program_id(1)
    bi = jnp.maximum(s - 1, 0) // blocks_per_seq
    m0 = pl.multiple_of(bi * N_META, N_META)
    pad = jnp.zeros((SWA_BLOCK - N_META, LANES), BF16)
    k_m = jnp.concatenate([km_ref[pl.ds(m0, N_META), :], pad], axis=0)
    v_m = jnp.concatenate([vm_ref[pl.ds(m0, N_META), :], pad], axis=0)
    q = q_ref[...]
    qp = jnp.concatenate([q[:, p * LANES:(p + 1) * LANES] for p in range(SWA_PAIRS)], axis=0)

    s_m = _dot_nt(qp, _block_diag2(k_m)) + bm_ref[0, 0]
    s_p = _dot_nt(qp, _block_diag2(kp_ref[...])) + bp_ref[0, 0]
    s_c = _dot_nt(qp, _block_diag2(kc_ref[...])) + bc_ref[0, 0]
    top = jnp.maximum(jnp.maximum(s_m, s_p), s_c)
    m_even = jnp.max(top[:, :LANES], axis=-1, keepdims=True)
    m_odd = jnp.max(top[:, LANES:], axis=-1, keepdims=True)
    lane2 = lax.broadcasted_iota(jnp.int32, top.shape, 1)
    m_full = jnp.where(lane2 < LANES, m_even, m_odd)
    p_m, p_p, p_c = jnp.exp(s_m - m_full), jnp.exp(s_p - m_full), jnp.exp(s_c - m_full)
    p_sum = p_m + p_p + p_c
    l_even = jnp.sum(p_sum[:, :LANES], axis=-1, keepdims=True)
    l_odd = jnp.sum(p_sum[:, LANES:], axis=-1, keepdims=True)
    o = (_dot(p_m.astype(BF16), _block_diag2(v_m))
         + _dot(p_p.astype(BF16), _block_diag2(vp_ref[...]))
         + _dot(p_c.astype(BF16), _block_diag2(vc_ref[...])))
    lane1 = lax.broadcasted_iota(jnp.int32, o.shape, 1)
    o = o / jnp.where(lane1 < SWA_HEAD_DIM, l_even, l_odd)
    for p in range(SWA_PAIRS):
        o_ref[:, p * LANES:(p + 1) * LANES] = o[p * SWA_BLOCK:(p + 1) * SWA_BLOCK].astype(o_ref.dtype)


def swa_mix(q, kv2, bias_m, bias_p, bias_c, n_batch, seq):
    rows = q.shape[0]
    blocks_per_seq = seq // SWA_BLOCK
    n_real_blk = n_batch * blocks_per_seq
    n_steps = 1 + n_real_blk
    meta_blk = n_real_blk
    assert n_batch * N_META == SWA_BLOCK
    kvn, qw = SWA_KV_HEADS, SWA_GROUP * SWA_HEAD_DIM

    def cur_blk(s):
        return jnp.where(s == 0, meta_blk, s - 1)

    def prev_blk(s):
        return jnp.maximum(s - 2, 0)

    def variant(s):
        return jnp.where(s == 0, 2, jnp.where((s - 1) % blocks_per_seq == 0, 0, 1))

    def kv_spec(head0, blk_fn):
        return pl.BlockSpec((SWA_BLOCK, LANES), lambda kh, s: (blk_fn(s), head0 + kh))

    bias_spec = pl.BlockSpec((1, 1, SWA_PAIRS * SWA_BLOCK, 2 * SWA_BLOCK),
                             lambda kh, s: (variant(s), kh, 0, 0))
    return pl.pallas_call(
        functools.partial(_swa_kernel, blocks_per_seq=blocks_per_seq),
        grid=(kvn, n_steps),
        in_specs=[pl.BlockSpec((SWA_BLOCK, qw), lambda kh, s: (cur_blk(s), kh)),
                  kv_spec(0, lambda s: meta_blk), kv_spec(0, prev_blk), kv_spec(0, cur_blk),
                  kv_spec(kvn, lambda s: meta_blk), kv_spec(kvn, prev_blk), kv_spec(kvn, cur_blk),
                  bias_spec, bias_spec, bias_spec],
        out_specs=pl.BlockSpec((SWA_BLOCK, qw), lambda kh, s: (cur_blk(s), kh)),
        out_shape=jax.ShapeDtypeStruct((rows, kvn * qw), BF16),
        compiler_params=_params("parallel", "arbitrary"),
        name="swa_mix",
    )(q, kv2, kv2, kv2, kv2, kv2, kv2, bias_m, bias_p, bias_c)


def _t5_bucket(dist):
    exact = REL_BUCKETS // 2
    d = jnp.maximum(dist, 0)
    df = jnp.maximum(d, 1).astype(F32)
    large = exact + (jnp.log(df / exact) / math.log(REL_MAX_DIST / exact)
                     * (REL_BUCKETS - exact)).astype(jnp.int32)
    large = jnp.minimum(large, REL_BUCKETS - 1)
    return jnp.where(d < exact, d, large)


def _pair_layout(bias):
    nv = bias.shape[0]
    b = bias.reshape(nv, SWA_KV_HEADS, SWA_PAIRS, 2, SWA_BLOCK, SWA_BLOCK)
    b = jnp.transpose(b, (0, 1, 2, 4, 3, 5))
    return b.reshape(nv, SWA_KV_HEADS, SWA_PAIRS * SWA_BLOCK, 2 * SWA_BLOCK)


def _swa_bias_static(table):
    tab = table.astype(F32)
    blk = SWA_BLOCK
    i = jnp.arange(blk, dtype=jnp.int32)[:, None]
    j = jnp.arange(blk, dtype=jnp.int32)[None, :]
    m = jnp.arange(N_META, dtype=jnp.int32)[None, :]

    def lookup(dist):
        return jnp.transpose(tab[_t5_bucket(dist)], (2, 0, 1))

    neg = jnp.full((SWA_Q_HEADS, blk, blk), NEG_INF, F32)
    cur = jnp.where((i - j >= 0)[None], lookup(i - j), NEG_INF)
    d_prev = blk + i - j
    prev = jnp.where(((d_prev >= 0) & (d_prev < SWA_WINDOW))[None], lookup(d_prev), NEG_INF)
    meta0 = lookup(N_META + i - m)
    meta1 = lookup(N_META + blk + i - m)
    same_seq = (i // N_META) == (j // N_META)
    dm = (i % N_META) - (j % N_META)
    meta_tile = jnp.where((same_seq & (dm >= 0))[None], lookup(dm), NEG_INF)
    bias_p = _pair_layout(jnp.stack([neg, prev, neg]))
    bias_c = _pair_layout(jnp.stack([cur, cur, meta_tile]))
    return bias_p, bias_c, jnp.stack([meta0, meta1])


def _swa_bias_meta(meta01, sinks):
    nh, blk = SWA_Q_HEADS, SWA_BLOCK
    sink_col = jnp.broadcast_to(sinks.astype(F32)[None, :, None, None], (3, nh, blk, 1))
    meta = jnp.concatenate([meta01, jnp.full((1, nh, blk, N_META), NEG_INF, F32)], axis=0)
    rest = jnp.full((3, nh, blk, blk - N_META - 1), NEG_INF, F32)
    return _pair_layout(jnp.concatenate([meta, sink_col, rest], axis=-1))


def kernel(x, meta_tokens, rel_bias_table, ln_gain, ln_bias, gla_w_in, gla_w_gate2, gla_b_gate,
           gla_norm_gain, gla_w_out, kv_w_shared, swa_w_q, swa_sinks, swa_w_out,
           ffn_w_gate_up, ffn_w_down, moe_w_router, moe_w_gate_up, moe_w_down):
    bsz, seq, d = x.shape
    n_real = bsz * seq
    h = jnp.concatenate([x.reshape(n_real, d),
                         jnp.broadcast_to(meta_tokens.astype(x.dtype)[None], (bsz, N_META, d))
                         .reshape(bsz * N_META, d)], axis=0)
    rows = h.shape[0]
    hb = h.astype(BF16)
    bias_p, bias_c, meta01 = _swa_bias_static(rel_bias_table)

    w_in_b = gla_w_in.astype(BF16)
    w_gla_out_b = gla_w_out.astype(BF16)
    w_q_b = swa_w_q.astype(BF16)
    w_swa_out_b = swa_w_out.astype(BF16)
    w_ffn_gu_b = ffn_w_gate_up.astype(BF16)
    w_ffn_down_b = ffn_w_down.astype(BF16)
    hd = SWA_HEAD_DIM
    w_kv = kv_w_shared.reshape(d, 2 * SWA_KV_HEADS, 1, hd)
    w_kv2 = jnp.broadcast_to(w_kv, (d, 2 * SWA_KV_HEADS, 2, hd)).reshape(1, d, 4 * SWA_KV_HEADS * hd)
    w_kv2 = w_kv2.astype(BF16)

    kv2 = None
    for li in range(DEPTH):
        g0, b0 = ln_gain[li, 0][None, :], ln_bias[li, 0][None, :]
        g1, b1 = ln_gain[li, 1][None, :], ln_bias[li, 1][None, :]
        if li < N_A_LAYERS:
            qkvr = linear(hb, w_in_b, li, GLA_MAIN, F32)
            wg_pad = jnp.pad(w_in_b[li, :, GLA_MAIN:], ((0, 0), (0, LANES - GLA_GATE_RANK)))
            w2_pad = jnp.pad(gla_w_gate2[li].astype(BF16), ((0, LANES - GLA_GATE_RANK), (0, 0)))
            log_a = gla_gate(hb, wg_pad, w2_pad, gla_b_gate[li][None, :])
            mix_in = gla_mix(qkvr, log_a, gla_norm_gain[li][None, :], bsz, seq)
            h, hb = linear_res_ln(mix_in, w_gla_out_b, li, h, g0, b0)
        else:
            jb = li - N_A_LAYERS
            q = linear(hb, w_q_b, jb, d, BF16, scale=hd ** -0.5)
            bias_m = _swa_bias_meta(meta01, swa_sinks[jb])
            mix_in = swa_mix(q, kv2, bias_m, bias_p, bias_c, bsz, seq)
            h, hb = linear_res_ln(mix_in, w_swa_out_b, jb, h, g0, b0)
        if li % 2 == 0:
            h, hb = ffn_res_ln(hb, h, w_ffn_gu_b, w_ffn_down_b, li // 2, g1, b1)
        else:
            w_r = jnp.pad(moe_w_router[li // 2].astype(BF16), ((0, 0), (0, LANES - N_EXPERTS)))
            out_rows = n_real if li == DEPTH - 1 else rows
            h, hb = moe_res_ln(h, hb, w_r, moe_w_gate_up, moe_w_down, li // 2, g1, b1, out_rows)
        if li == N_A_LAYERS - 1:
            kv2 = linear(hb, w_kv2, 0, w_kv2.shape[-1], BF16)
    return h.reshape(bsz, seq, d)
```

```python
import functools
import math

import jax
import jax.numpy as jnp
from jax import lax
from jax.experimental import pallas as pl
from jax.experimental.pallas import tpu as pltpu

F32 = jnp.float32
BF16 = jnp.bfloat16

D_MODEL = 2048
DEPTH = 4
N_META = 16
N_A_LAYERS = DEPTH // 2
DN_ALPHA = (2 * DEPTH) ** 0.25
LN_EPS = 1e-5

GLA_HEADS = 4
GLA_DK = D_MODEL // 2
GLA_DV = D_MODEL
GLA_DK_HEAD = GLA_DK // GLA_HEADS
GLA_DV_HEAD = GLA_DV // GLA_HEADS
GLA_GATE_RANK = 16
GLA_GATE_TAU = 16.0
GLA_CHUNK = 64
GLA_MAIN = 2 * GLA_DK + 2 * GLA_DV

SWA_HEAD_DIM = 64
SWA_Q_HEADS = D_MODEL // SWA_HEAD_DIM
SWA_GROUP = 8
SWA_KV_HEADS = SWA_Q_HEADS // SWA_GROUP
SWA_PAIRS = SWA_GROUP // 2
SWA_WINDOW = 128
SWA_BLOCK = 128

REL_BUCKETS = 32
REL_MAX_DIST = 128

FFN_DIM = 7 * D_MODEL // 2
N_EXPERTS = 8
TOP_K = 2
NEG_INF = -1e9

LANES = 128
VMEM_LIMIT = 56 * 1024 * 1024
ROW_TILE = 512
PROJ_ROW_TILE = 1376
PROJ_COL_TILE = 1024
FFN_TILE = 512
MOE_ROW_TILE = 1024
MOE_GATHER_ROWS = 80
GLA_ROWS = 256


def _params(*sem):
    return pltpu.CompilerParams(dimension_semantics=sem, vmem_limit_bytes=VMEM_LIMIT)


def _layer_norm_rows(y, g, b):
    mu = jnp.mean(y, axis=-1, keepdims=True)
    yc = y - mu
    var = jnp.mean(yc * yc, axis=-1, keepdims=True)
    return yc * lax.rsqrt(var + LN_EPS) * g + b


def _dot(a, b):
    return jnp.dot(a, b, preferred_element_type=F32)


def _dot_nt(a, b):
    return lax.dot_general(a, b, (((1,), (1,)), ((), ())), preferred_element_type=F32)


def _dot_tn(a, b, precision=None):
    return lax.dot_general(a, b, (((0,), (0,)), ((), ())), preferred_element_type=F32,
                           precision=precision)


def _linear_kernel(x_ref, w_ref, o_ref, *, scale):
    acc = _dot(x_ref[...], w_ref[...])
    if scale != 1.0:
        acc = acc * scale
    o_ref[...] = acc.astype(o_ref.dtype)


def linear(x, w, layer, n_out, out_dtype, scale=1.0):
    rows, k = x.shape
    tm = PROJ_ROW_TILE if rows % PROJ_ROW_TILE == 0 else ROW_TILE
    tn = PROJ_COL_TILE
    return pl.pallas_call(
        functools.partial(_linear_kernel, scale=scale),
        grid=(pl.cdiv(rows, tm), n_out // tn),
        in_specs=[pl.BlockSpec((tm, k), lambda i, j: (i, 0)),
                  pl.BlockSpec((None, k, tn), lambda i, j: (layer, 0, j))],
        out_specs=pl.BlockSpec((tm, tn), lambda i, j: (i, j)),
        out_shape=jax.ShapeDtypeStruct((rows, n_out), out_dtype),
        compiler_params=_params("parallel", "arbitrary"),
        name="linear",
    )(x, w)


def _gla_gate_kernel(x_ref, wg_ref, w2_ref, b_ref, o_ref):
    g_low = _dot(x_ref[...], wg_ref[...])
    z = _dot(g_low.astype(BF16), w2_ref[...]) + b_ref[...]
    log_sig = jnp.minimum(z, 0.0) - jnp.log1p(jnp.exp(-jnp.abs(z)))
    o_ref[...] = log_sig / GLA_GATE_TAU


def gla_gate(x, wg_pad, w2_pad, b_gate):
    rows, k = x.shape
    return pl.pallas_call(
        _gla_gate_kernel,
        grid=(pl.cdiv(rows, ROW_TILE),),
        in_specs=[pl.BlockSpec((ROW_TILE, k), lambda i: (i, 0)),
                  pl.BlockSpec((k, LANES), lambda i: (0, 0)),
                  pl.BlockSpec((LANES, GLA_DK), lambda i: (0, 0)),
                  pl.BlockSpec((1, GLA_DK), lambda i: (0, 0))],
        out_specs=pl.BlockSpec((ROW_TILE, GLA_DK), lambda i: (i, 0)),
        out_shape=jax.ShapeDtypeStruct((rows, GLA_DK), F32),
        compiler_params=_params("parallel"),
        name="gla_gate",
    )(x, wg_pad, w2_pad, b_gate)


def _linear_res_ln_kernel(x_ref, w_ref, res_ref, g_ref, b_ref, o_ref, ob_ref):
    half = x_ref.shape[0] // 2
    for hf in range(2):
        sl = slice(hf * half, (hf + 1) * half)
        y = DN_ALPHA * res_ref[sl, :] + _dot(x_ref[sl, :], w_ref[...])
        o = _layer_norm_rows(y, g_ref[...], b_ref[...])
        o_ref[sl, :] = o
        ob_ref[sl, :] = o.astype(BF16)


def linear_res_ln(x, w, layer, res, g, b):
    rows, k = x.shape
    d = w.shape[-1]
    return pl.pallas_call(
        _linear_res_ln_kernel,
        grid=(pl.cdiv(rows, ROW_TILE),),
        in_specs=[pl.BlockSpec((ROW_TILE, k), lambda i: (i, 0)),
                  pl.BlockSpec((None, k, d), lambda i: (layer, 0, 0)),
                  pl.BlockSpec((ROW_TILE, d), lambda i: (i, 0)),
                  pl.BlockSpec((1, d), lambda i: (0, 0)),
                  pl.BlockSpec((1, d), lambda i: (0, 0))],
        out_specs=[pl.BlockSpec((ROW_TILE, d), lambda i: (i, 0)),
                   pl.BlockSpec((ROW_TILE, d), lambda i: (i, 0))],
        out_shape=[jax.ShapeDtypeStruct((rows, d), F32), jax.ShapeDtypeStruct((rows, d), BF16)],
        compiler_params=_params("parallel"),
        name="linear_res_ln",
    )(x, w, res, g, b)


def _swiglu_partial(x, wa, wu, wd):
    a = _dot(x, wa)
    u = _dot(x, wu)
    hidden = (a * jax.nn.sigmoid(a) * u).astype(BF16)
    return _dot(hidden, wd)


def _ffn_kernel(x_ref, wa_ref, wu_ref, wd_ref, res_ref, g_ref, b_ref, o_ref, ob_ref, acc_ref):
    j = pl.program_id(1)

    @pl.when(j == 0)
    def _():
        acc_ref[...] = jnp.zeros_like(acc_ref)

    acc_ref[...] += _swiglu_partial(x_ref[...], wa_ref[...], wu_ref[...], wd_ref[...])

    @pl.when(j == pl.num_programs(1) - 1)
    def _():
        o = _layer_norm_rows(DN_ALPHA * res_ref[...] + acc_ref[...], g_ref[...], b_ref[...])
        o_ref[...] = o
        ob_ref[...] = o.astype(BF16)


def ffn_res_ln(xb, res, w_gu, w_down, layer, g, b):
    rows, d = xb.shape
    f = w_down.shape[-2]
    nj = f // FFN_TILE
    return pl.pallas_call(
        _ffn_kernel,
        grid=(pl.cdiv(rows, ROW_TILE), nj),
        in_specs=[pl.BlockSpec((ROW_TILE, d), lambda i, j: (i, 0)),
                  pl.BlockSpec((None, d, FFN_TILE), lambda i, j: (layer, 0, j)),
                  pl.BlockSpec((None, d, FFN_TILE), lambda i, j: (layer, 0, nj + j)),
                  pl.BlockSpec((None, FFN_TILE, d), lambda i, j: (layer, j, 0)),
                  pl.BlockSpec((ROW_TILE, d), lambda i, j: (i, 0)),
                  pl.BlockSpec((1, d), lambda i, j: (0, 0)),
                  pl.BlockSpec((1, d), lambda i, j: (0, 0))],
        out_specs=[pl.BlockSpec((ROW_TILE, d), lambda i, j: (i, 0)),
                   pl.BlockSpec((ROW_TILE, d), lambda i, j: (i, 0))],
        out_shape=[jax.ShapeDtypeStruct((rows, d), F32), jax.ShapeDtypeStruct((rows, d), BF16)],
        scratch_shapes=[pltpu.VMEM((ROW_TILE, d), F32)],
        compiler_params=_params("parallel", "arbitrary"),
        name="ffn_res_ln",
    )(xb, w_gu, w_gu, w_down, res, g, b)


def _router_kernel(x_ref, w_ref, o_ref):
    logits = _dot(x_ref[...], w_ref[...])
    lane = lax.broadcasted_iota(jnp.int32, logits.shape, 1)
    lane_f = lane.astype(F32)
    logits = jnp.where(lane < N_EXPERTS, logits, -jnp.inf)
    m1 = jnp.max(logits, axis=-1, keepdims=True)
    i1 = jnp.min(jnp.where(logits == m1, lane_f, float(LANES)), axis=-1, keepdims=True)
    rest = jnp.where(lane_f == i1, -jnp.inf, logits)
    m2 = jnp.max(rest, axis=-1, keepdims=True)
    i2 = jnp.min(jnp.where(rest == m2, lane_f, float(LANES)), axis=-1, keepdims=True)
    e2 = jnp.exp(m2 - m1)
    g1 = 1.0 / (1.0 + e2)
    g2 = e2 / (1.0 + e2)
    o_ref[...] = jnp.where(lane == 0, i1, jnp.where(lane == 1, i2,
                           jnp.where(lane == 2, g1, jnp.where(lane == 3, g2, 0.0))))


def router(xb, w_router_pad):
    rows, d = xb.shape
    return pl.pallas_call(
        _router_kernel,
        grid=(pl.cdiv(rows, ROW_TILE),),
        in_specs=[pl.BlockSpec((ROW_TILE, d), lambda i: (i, 0)),
                  pl.BlockSpec((d, LANES), lambda i: (0, 0))],
        out_specs=pl.BlockSpec((ROW_TILE, LANES), lambda i: (i, 0)),
        out_shape=jax.ShapeDtypeStruct((rows, LANES), F32),
        compiler_params=_params("parallel"),
        name="router",
    )(xb, w_router_pad)


def _moe_kernel(blk_e_ref, n_half_ref, tok_ref, h_hbm, wa_ref, wu_ref, wd_ref, o_ref,
                land_ref, x_ref, sem):
    del blk_e_ref
    i = pl.program_id(0)
    j = pl.program_id(1)
    n_blk = pl.num_programs(0)
    nj = pl.num_programs(1)
    n_half = n_half_ref[i]
    half = MOE_ROW_TILE // 2
    land_rows = land_ref.shape[0]

    def row_copy(slot, dst_row):
        return pltpu.make_async_copy(h_hbm.at[pl.ds(tok_ref[slot], 1), :],
                                     land_ref.at[pl.ds(dst_row, 1), :], sem)

    def wait_landing():
        pltpu.make_async_copy(h_hbm.at[pl.ds(0, land_rows), :], land_ref, sem).wait()

    @pl.when(j == 0)
    def _():
        o_ref[...] = jnp.zeros_like(o_ref)

        @pl.when(i == 0)
        def _():
            def issue(r, carry):
                row_copy(r, r).start()
                return carry
            lax.fori_loop(0, land_rows, issue, 0, unroll=8)

        @pl.when(jnp.logical_or(i == 0, n_half_ref[jnp.maximum(i - 1, 0)] > 0))
        def _():
            wait_landing()

        for hf in range(2):
            @pl.when(n_half > hf)
            def _():
                sl = slice(hf * half, (hf + 1) * half)
                x_ref[sl, :] = land_ref[sl, :].astype(BF16)

    @pl.when(n_half > 0)
    def _():
        next_base = jnp.minimum(i + 1, n_blk - 1) * MOE_ROW_TILE
        r0 = j * MOE_GATHER_ROWS
        for k in range(MOE_GATHER_ROWS):
            row_copy(next_base + r0 + k, r0 + k).start()
        wa = wa_ref[...].astype(BF16)
        wu = wu_ref[...].astype(BF16)
        wd = wd_ref[...].astype(BF16)
        o_ref[:half, :] += _swiglu_partial(x_ref[:half, :], wa, wu, wd)

        @pl.when(n_half > 1)
        def _():
            o_ref[half:, :] += _swiglu_partial(x_ref[half:, :], wa, wu, wd)

    @pl.when(jnp.logical_and(jnp.logical_and(i == n_blk - 1, j == nj - 1), n_half > 0))
    def _():
        wait_landing()


def moe_experts(h, buf_tok, blk_e, n_half, w_gu, w_down, layer):
    d = h.shape[1]
    p_rows = buf_tok.shape[0]
    f = w_down.shape[-2]
    nj = f // FFN_TILE
    n_blk = p_rows // MOE_ROW_TILE
    assert nj * MOE_GATHER_ROWS >= MOE_ROW_TILE
    buf_tok = jnp.pad(buf_tok, (0, nj * MOE_GATHER_ROWS - MOE_ROW_TILE))

    def jj(i, j, nh):
        return jnp.where(nh[i] > 0, j, nj - 1)

    grid_spec = pltpu.PrefetchScalarGridSpec(
        num_scalar_prefetch=3,
        grid=(n_blk, nj),
        in_specs=[pl.BlockSpec(memory_space=pl.ANY),
                  pl.BlockSpec((None, None, d, FFN_TILE),
                               lambda i, j, be, nh, tk: (layer, be[i], 0, jj(i, j, nh))),
                  pl.BlockSpec((None, None, d, FFN_TILE),
                               lambda i, j, be, nh, tk: (layer, be[i], 0, nj + jj(i, j, nh))),
                  pl.BlockSpec((None, None, FFN_TILE, d),
                               lambda i, j, be, nh, tk: (layer, be[i], jj(i, j, nh), 0))],
        out_specs=pl.BlockSpec((MOE_ROW_TILE, d), lambda i, j, be, nh, tk: (i, 0),
                               pipeline_mode=pl.Buffered(1)),
        scratch_shapes=[pltpu.VMEM((nj * MOE_GATHER_ROWS, d), F32),
                        pltpu.VMEM((MOE_ROW_TILE, d), BF16),
                        pltpu.SemaphoreType.DMA(())],
    )
    return pl.pallas_call(
        _moe_kernel,
        grid_spec=grid_spec,
        out_shape=jax.ShapeDtypeStruct((p_rows, d), F32),
        compiler_params=_params("arbitrary", "arbitrary"),
        name="moe_experts",
    )(blk_e, n_half, buf_tok, h, w_gu, w_gu, w_down)


def _combine_ln_kernel(res_ref, y0_ref, y1_ref, gate_ref, g_ref, b_ref, o_ref, ob_ref):
    gate = gate_ref[...]
    moe = y0_ref[...] * gate[:, 2:3] + y1_ref[...] * gate[:, 3:4]
    o = _layer_norm_rows(DN_ALPHA * res_ref[...] + moe, g_ref[...], b_ref[...])
    o_ref[...] = o
    ob_ref[...] = o.astype(BF16)


def combine_ln(res, y0, y1, route, g, b, out_rows):
    d = res.shape[1]
    row_spec = pl.BlockSpec((ROW_TILE, d), lambda i: (i, 0))
    vec_spec = pl.BlockSpec((1, d), lambda i: (0, 0))
    return pl.pallas_call(
        _combine_ln_kernel,
        grid=(pl.cdiv(out_rows, ROW_TILE),),
        in_specs=[row_spec, row_spec, row_spec, pl.BlockSpec((ROW_TILE, LANES), lambda i: (i, 0)),
                  vec_spec, vec_spec],
        out_specs=[row_spec, row_spec],
        out_shape=[jax.ShapeDtypeStruct((out_rows, d), F32),
                   jax.ShapeDtypeStruct((out_rows, d), BF16)],
        compiler_params=_params("parallel"),
        name="combine_ln",
    )(res, y0, y1, route, g, b)


def moe_res_ln(h, hb, w_router_pad, w_gu, w_down, layer, g, b, out_rows):
    rows, d = h.shape
    route = router(hb, w_router_pad)
    flat_e = route[:, :TOP_K].astype(jnp.int32).reshape(-1)
    n_flat = rows * TOP_K
    n_blk = -(-n_flat // MOE_ROW_TILE) + N_EXPERTS
    p_rows = n_blk * MOE_ROW_TILE
    onehot = (flat_e[:, None] == jnp.arange(N_EXPERTS, dtype=jnp.int32)[None, :]).astype(jnp.int32)
    csum = jnp.cumsum(onehot, axis=0)
    rank = jnp.sum(csum * onehot, axis=1) - 1
    counts = csum[-1]
    padded = ((counts + MOE_ROW_TILE - 1) // MOE_ROW_TILE) * MOE_ROW_TILE
    padded_end = jnp.cumsum(padded)
    start_padded = padded_end - padded
    pos = start_padded[flat_e] + rank
    flat_tok = jnp.arange(n_flat, dtype=jnp.int32) // TOP_K
    buf_tok = jnp.zeros((p_rows,), jnp.int32).at[pos].set(flat_tok)
    blk_start = jnp.arange(n_blk, dtype=jnp.int32) * MOE_ROW_TILE
    blk_e = jnp.minimum(jnp.searchsorted(padded_end, blk_start, side='right'),
                        N_EXPERTS - 1).astype(jnp.int32)
    valid = jnp.clip(counts[blk_e] - (blk_start - start_padded[blk_e]), 0, MOE_ROW_TILE)
    valid = jnp.where(blk_start < padded_end[-1], valid, 0)
    half = MOE_ROW_TILE // 2
    n_half = ((valid + half - 1) // half).astype(jnp.int32)
    y = moe_experts(h, buf_tok, blk_e, n_half, w_gu, w_down, layer)
    pos2 = pos.reshape(rows, TOP_K)
    y0 = y.at[pos2[:, 0]].get(mode="promise_in_bounds")
    y1 = y.at[pos2[:, 1]].get(mode="promise_in_bounds")
    return combine_ln(h, y0, y1, route, g, b, out_rows)


def _head_norm_gate(o, r, gain):
    mu = jnp.mean(o, axis=-1, keepdims=True)
    oc = o - mu
    var = jnp.mean(oc * oc, axis=-1, keepdims=True)
    on = oc * lax.rsqrt(var + LN_EPS) * gain
    return (on * (r * jax.nn.sigmoid(r))).astype(BF16)


def _masked_cumsum(mask, x):
    m = mask.astype(BF16)
    hi = x.astype(BF16)
    r1 = x - hi.astype(F32)
    mid = r1.astype(BF16)
    lo = (r1 - mid.astype(F32)).astype(BF16)
    return _dot(m, hi) + _dot(m, mid) + _dot(m, lo)


def _gla_first_chunk(q, k, v, r, la, gain):
    c = q.shape[0]
    row = lax.broadcasted_iota(jnp.int32, (c, c), 0)
    col = lax.broadcasted_iota(jnp.int32, (c, c), 1)
    causal = row >= col
    bcum = _masked_cumsum(causal, la)
    q_dec = (q * (GLA_DK_HEAD ** -0.5) * jnp.exp(bcum)).astype(BF16)
    k_inv = (k * jnp.exp(-bcum)).astype(BF16)
    vb = v.astype(BF16)
    att = jnp.where(causal, _dot_nt(q_dec, k_inv), 0.0)
    o = _dot(att.astype(BF16), vb)
    k_state = (k * jnp.exp(bcum[c - 1:c, :] - bcum)).astype(BF16)
    return _head_norm_gate(o, r, gain), _dot_tn(k_state, vb)


def _gla_block(q, k, v, r, la, gain, state):
    n = q.shape[0]
    c = GLA_CHUNK
    row = lax.broadcasted_iota(jnp.int32, (n, n), 0)
    col = lax.broadcasted_iota(jnp.int32, (n, n), 1)
    causal = (row >= col) & ((row // c) == (col // c))
    bcum = _masked_cumsum(causal, la)
    q_dec = (q * (GLA_DK_HEAD ** -0.5) * jnp.exp(bcum)).astype(BF16)
    k_inv = (k * jnp.exp(-bcum)).astype(BF16)
    vb = v.astype(BF16)
    att = jnp.where(causal, _dot_nt(q_dec, k_inv), 0.0)
    o_intra = _dot(att.astype(BF16), vb)
    b_last = [bcum[(ci + 1) * c - 1:(ci + 1) * c, :] for ci in range(n // c)]
    b_tot = jnp.concatenate([jnp.broadcast_to(bl, (c, bl.shape[1])) for bl in b_last], axis=0)
    k_state = (k * jnp.exp(b_tot - bcum)).astype(BF16)
    o_inter = []
    for ci in range(n // c):
        sl = slice(ci * c, (ci + 1) * c)
        o_inter.append(_dot(q_dec[sl], state.astype(BF16)))
        decay = jnp.exp(jnp.transpose(jnp.broadcast_to(b_last[ci], (LANES, b_tot.shape[1]))))
        decay = jnp.concatenate([decay] * (GLA_DV_HEAD // LANES), axis=1)
        state = state * decay + _dot_tn(k_state[sl], vb[sl])
    o = o_intra + jnp.concatenate(o_inter, axis=0)
    return _head_norm_gate(o, r, gain), state


def _gla_kernel(qm_ref, km_ref, vm_ref, rm_ref, lam_ref, q_ref, k_ref, v_ref, r_ref, la_ref,
                gain_ref, o_ref, state_ref, *, blocks_per_seq):
    s = pl.program_id(1)
    gain = gain_ref[...]
    n_batch = state_ref.shape[0]

    @pl.when(s == 0)
    def _():
        for bi in range(n_batch):
            sl = slice(bi * N_META, (bi + 1) * N_META)
            og, st = _gla_first_chunk(qm_ref[sl, :], km_ref[sl, :], vm_ref[sl, :], rm_ref[sl, :],
                                      lam_ref[sl, :], gain)
            state_ref[bi] = st
            o_ref[sl, :] = og

    @pl.when(s > 0)
    def _():
        bi = (s - 1) // blocks_per_seq
        og, st = _gla_block(q_ref[...], k_ref[...], v_ref[...], r_ref[...], la_ref[...], gain,
                            state_ref[bi])
        o_ref[...] = og
        state_ref[bi] = st


def gla_mix(qkvr, log_a, norm_gain, n_batch, seq):
    rows = qkvr.shape[0]
    n_real = n_batch * seq
    meta_rows = n_batch * N_META
    blocks_per_seq = seq // GLA_ROWS
    n_steps = 1 + n_real // GLA_ROWS
    meta_blk = n_real // meta_rows
    meta_out_blk = n_real // GLA_ROWS
    dk, dv, nh = GLA_DK_HEAD, GLA_DV_HEAD, GLA_HEADS
    kq, kk, kv_, kr = 0, GLA_DK // dk, 2 * GLA_DK // dv, (2 * GLA_DK + GLA_DV) // dv

    def real_blk(s):
        return jnp.maximum(s - 1, 0)

    def meta_spec(width, col0):
        return pl.BlockSpec((meta_rows, width), lambda h, s: (meta_blk, col0 + h))

    def real_spec(width, col0):
        return pl.BlockSpec((GLA_ROWS, width), lambda h, s: (real_blk(s), col0 + h))

    return pl.pallas_call(
        functools.partial(_gla_kernel, blocks_per_seq=blocks_per_seq),
        grid=(nh, n_steps),
        in_specs=[meta_spec(dk, kq), meta_spec(dk, kk), meta_spec(dv, kv_), meta_spec(dv, kr),
                  meta_spec(dk, 0),
                  real_spec(dk, kq), real_spec(dk, kk), real_spec(dv, kv_), real_spec(dv, kr),
                  real_spec(dk, 0),
                  pl.BlockSpec((1, dv), lambda h, s: (0, h))],
        out_specs=pl.BlockSpec((GLA_ROWS, dv),
                               lambda h, s: (jnp.where(s == 0, meta_out_blk, s - 1), h)),
        out_shape=jax.ShapeDtypeStruct((rows, GLA_DV), BF16),
        scratch_shapes=[pltpu.VMEM((n_batch, dk, dv), F32)],
        compiler_params=_params("parallel", "arbitrary"),
        name="gla_mix",
    )(qkvr, qkvr, qkvr, qkvr, log_a, qkvr, qkvr, qkvr, qkvr, log_a, norm_gain)


def _block_diag2(x2):
    lane = lax.broadcasted_iota(jnp.int32, x2.shape, 1)
    zero = jnp.zeros_like(x2)
    return jnp.concatenate([jnp.where(lane < SWA_HEAD_DIM, x2, zero),
                            jnp.where(lane >= SWA_HEAD_DIM, x2, zero)], axis=0)


def _swa_kernel(q_ref, km_ref, kp_ref, kc_ref, vm_ref, vp_ref, vc_ref,
                bm_ref, bp_ref, bc_ref, o_ref, *, blocks_per_seq):
    s = pl.program_id(1)
    bi = jnp.maximum(s - 1, 0) // blocks_per_seq
    m0 = pl.multiple_of(bi * N_META, N_META)
    pad = jnp.zeros((SWA_BLOCK - N_META, LANES), BF16)
    k_m = jnp.concatenate([km_ref[pl.ds(m0, N_META), :], pad], axis=0)
    v_m = jnp.concatenate([vm_ref[pl.ds(m0, N_META), :], pad], axis=0)
    q = q_ref[...]
    qp = jnp.concatenate([q[:, p * LANES:(p + 1) * LANES] for p in range(SWA_PAIRS)], axis=0)

    s_m = _dot_nt(qp, _block_diag2(k_m)) + bm_ref[0, 0]
    s_p = _dot_nt(qp, _block_diag2(kp_ref[...])) + bp_ref[0, 0]
    s_c = _dot_nt(qp, _block_diag2(kc_ref[...])) + bc_ref[0, 0]
    top = jnp.maximum(jnp.maximum(s_m, s_p), s_c)
    m_even = jnp.max(top[:, :LANES], axis=-1, keepdims=True)
    m_odd = jnp.max(top[:, LANES:], axis=-1, keepdims=True)
    lane2 = lax.broadcasted_iota(jnp.int32, top.shape, 1)
    m_full = jnp.where(lane2 < LANES, m_even, m_odd)
    p_m, p_p, p_c = jnp.exp(s_m - m_full), jnp.exp(s_p - m_full), jnp.exp(s_c - m_full)
    p_sum = p_m + p_p + p_c
    l_even = jnp.sum(p_sum[:, :LANES], axis=-1, keepdims=True)
    l_odd = jnp.sum(p_sum[:, LANES:], axis=-1, keepdims=True)
    o = (_dot(p_m.astype(BF16), _block_diag2(v_m))
         + _dot(p_p.astype(BF16), _block_diag2(vp_ref[...]))
         + _dot(p_c.astype(BF16), _block_diag2(vc_ref[...])))
    lane1 = lax.broadcasted_iota(jnp.int32, o.shape, 1)
    o = o / jnp.where(lane1 < SWA_HEAD_DIM, l_even, l_odd)
    for p in range(SWA_PAIRS):
        o_ref[:, p * LANES:(p + 1) * LANES] = o[p * SWA_BLOCK:(p + 1) * SWA_BLOCK].astype(o_ref.dtype)


def swa_mix(q, kv2, bias_m, bias_p, bias_c, n_batch, seq):
    rows = q.shape[0]
    blocks_per_seq = seq // SWA_BLOCK
    n_real_blk = n_batch * blocks_per_seq
    n_steps = 1 + n_real_blk
    meta_blk = n_real_blk
    assert n_batch * N_META == SWA_BLOCK
    kvn, qw = SWA_KV_HEADS, SWA_GROUP * SWA_HEAD_DIM

    def cur_blk(s):
        return jnp.where(s == 0, meta_blk, s - 1)

    def prev_blk(s):
        return jnp.maximum(s - 2, 0)

    def variant(s):
        return jnp.where(s == 0, 2, jnp.where((s - 1) % blocks_per_seq == 0, 0, 1))

    def kv_spec(head0, blk_fn):
        return pl.BlockSpec((SWA_BLOCK, LANES), lambda kh, s: (blk_fn(s), head0 + kh))

    bias_spec = pl.BlockSpec((1, 1, SWA_PAIRS * SWA_BLOCK, 2 * SWA_BLOCK),
                             lambda kh, s: (variant(s), kh, 0, 0))
    return pl.pallas_call(
        functools.partial(_swa_kernel, blocks_per_seq=blocks_per_seq),
        grid=(kvn, n_steps),
        in_specs=[pl.BlockSpec((SWA_BLOCK, qw), lambda kh, s: (cur_blk(s), kh)),
                  kv_spec(0, lambda s: meta_blk), kv_spec(0, prev_blk), kv_spec(0, cur_blk),
                  kv_spec(kvn, lambda s: meta_blk), kv_spec(kvn, prev_blk), kv_spec(kvn, cur_blk),
                  bias_spec, bias_spec, bias_spec],
        out_specs=pl.BlockSpec((SWA_BLOCK, qw), lambda kh, s: (cur_blk(s), kh)),
        out_shape=jax.ShapeDtypeStruct((rows, kvn * qw), BF16),
        compiler_params=_params("parallel", "arbitrary"),
        name="swa_mix",
    )(q, kv2, kv2, kv2, kv2, kv2, kv2, bias_m, bias_p, bias_c)


def _t5_bucket(dist):
    exact = REL_BUCKETS // 2
    d = jnp.maximum(dist, 0)
    df = jnp.maximum(d, 1).astype(F32)
    large = exact + (jnp.log(df / exact) / math.log(REL_MAX_DIST / exact)
                     * (REL_BUCKETS - exact)).astype(jnp.int32)
    large = jnp.minimum(large, REL_BUCKETS - 1)
    return jnp.where(d < exact, d, large)


def _pair_layout(bias):
    nv = bias.shape[0]
    b = bias.reshape(nv, SWA_KV_HEADS, SWA_PAIRS, 2, SWA_BLOCK, SWA_BLOCK)
    b = jnp.transpose(b, (0, 1, 2, 4, 3, 5))
    return b.reshape(nv, SWA_KV_HEADS, SWA_PAIRS * SWA_BLOCK, 2 * SWA_BLOCK)


def _swa_bias_static(table):
    tab = table.astype(F32)
    blk = SWA_BLOCK
    i = jnp.arange(blk, dtype=jnp.int32)[:, None]
    j = jnp.arange(blk, dtype=jnp.int32)[None, :]
    m = jnp.arange(N_META, dtype=jnp.int32)[None, :]

    def lookup(dist):
        onehot = (_t5_bucket(dist)[..., None] == jnp.arange(REL_BUCKETS, dtype=jnp.int32)).astype(F32)
        return jnp.einsum('qsb,bh->hqs', onehot, tab, precision=lax.Precision.HIGHEST)

    neg = jnp.full((SWA_Q_HEADS, blk, blk), NEG_INF, F32)
    cur = jnp.where((i - j >= 0)[None], lookup(i - j), NEG_INF)
    d_prev = blk + i - j
    prev = jnp.where(((d_prev >= 0) & (d_prev < SWA_WINDOW))[None], lookup(d_prev), NEG_INF)
    meta0 = lookup(N_META + i - m)
    meta1 = lookup(N_META + blk + i - m)
    same_seq = (i // N_META) == (j // N_META)
    dm = (i % N_META) - (j % N_META)
    meta_tile = jnp.where((same_seq & (dm >= 0))[None], lookup(dm), NEG_INF)
    bias_p = _pair_layout(jnp.stack([neg, prev, neg]))
    bias_c = _pair_layout(jnp.stack([cur, cur, meta_tile]))
    return bias_p, bias_c, jnp.stack([meta0, meta1])


def _swa_bias_meta(meta01, sinks):
    nh, blk = SWA_Q_HEADS, SWA_BLOCK
    sink_col = jnp.broadcast_to(sinks.astype(F32)[None, :, None, None], (3, nh, blk, 1))
    meta = jnp.concatenate([meta01, jnp.full((1, nh, blk, N_META), NEG_INF, F32)], axis=0)
    rest = jnp.full((3, nh, blk, blk - N_META - 1), NEG_INF, F32)
    return _pair_layout(jnp.concatenate([meta, sink_col, rest], axis=-1))


def kernel(x, meta_tokens, rel_bias_table, ln_gain, ln_bias, gla_w_in, gla_w_gate2, gla_b_gate,
           gla_norm_gain, gla_w_out, kv_w_shared, swa_w_q, swa_sinks, swa_w_out,
           ffn_w_gate_up, ffn_w_down, moe_w_router, moe_w_gate_up, moe_w_down):
    bsz, seq, d = x.shape
    n_real = bsz * seq
    h = jnp.concatenate([x.reshape(n_real, d),
                         jnp.broadcast_to(meta_tokens.astype(x.dtype)[None], (bsz, N_META, d))
                         .reshape(bsz * N_META, d)], axis=0)
    rows = h.shape[0]
    hb = h.astype(BF16)
    bias_p, bias_c, meta01 = _swa_bias_static(rel_bias_table)

    w_in_b = gla_w_in.astype(BF16)
    w_gla_out_b = gla_w_out.astype(BF16)
    w_q_b = swa_w_q.astype(BF16)
    w_swa_out_b = swa_w_out.astype(BF16)
    w_ffn_gu_b = ffn_w_gate_up.astype(BF16)
    w_ffn_down_b = ffn_w_down.astype(BF16)
    hd = SWA_HEAD_DIM
    w_kv = kv_w_shared.reshape(d, 2 * SWA_KV_HEADS, 1, hd)
    w_kv2 = jnp.broadcast_to(w_kv, (d, 2 * SWA_KV_HEADS, 2, hd)).reshape(1, d, 4 * SWA_KV_HEADS * hd)
    w_kv2 = w_kv2.astype(BF16)

    kv2 = None
    for li in range(DEPTH):
        g0, b0 = ln_gain[li, 0][None, :], ln_bias[li, 0][None, :]
        g1, b1 = ln_gain[li, 1][None, :], ln_bias[li, 1][None, :]
        if li < N_A_LAYERS:
            qkvr = linear(hb, w_in_b, li, GLA_MAIN, F32)
            wg_pad = jnp.pad(w_in_b[li, :, GLA_MAIN:], ((0, 0), (0, LANES - GLA_GATE_RANK)))
            w2_pad = jnp.pad(gla_w_gate2[li].astype(BF16), ((0, LANES - GLA_GATE_RANK), (0, 0)))
            log_a = gla_gate(hb, wg_pad, w2_pad, gla_b_gate[li][None, :])
            mix_in = gla_mix(qkvr, log_a, gla_norm_gain[li][None, :], bsz, seq)
            h, hb = linear_res_ln(mix_in, w_gla_out_b, li, h, g0, b0)
        else:
            jb = li - N_A_LAYERS
            q = linear(hb, w_q_b, jb, d, BF16, scale=hd ** -0.5)
            bias_m = _swa_bias_meta(meta01, swa_sinks[jb])
            mix_in = swa_mix(q, kv2, bias_m, bias_p, bias_c, bsz, seq)
            h, hb = linear_res_ln(mix_in, w_swa_out_b, jb, h, g0, b0)
        if li % 2 == 0:
            h, hb = ffn_res_ln(hb, h, w_ffn_gu_b, w_ffn_down_b, li // 2, g1, b1)
        else:
            w_r = jnp.pad(moe_w_router[li // 2].astype(BF16), ((0, 0), (0, LANES - N_EXPERTS)))
            out_rows = n_real if li == DEPTH - 1 else rows
            h, hb = moe_res_ln(h, hb, w_r, moe_w_gate_up, moe_w_down, li // 2, g1, b1, out_rows)
        if li == N_A_LAYERS - 1:
            kv2 = linear(hb, w_kv2, 0, w_kv2.shape[-1], BF16)
    return h.reshape(bsz, seq, d)
```

```python
import functools
import math

import jax
import jax.numpy as jnp
from jax import lax
from jax.experimental import pallas as pl
from jax.experimental.pallas import tpu as pltpu

F32 = jnp.float32
BF16 = jnp.bfloat16

D_MODEL = 2048
DEPTH = 4
N_META = 16
N_A_LAYERS = DEPTH // 2
DN_ALPHA = (2 * DEPTH) ** 0.25
LN_EPS = 1e-5

GLA_HEADS = 4
GLA_DK = D_MODEL // 2
GLA_DV = D_MODEL
GLA_DK_HEAD = GLA_DK // GLA_HEADS
GLA_DV_HEAD = GLA_DV // GLA_HEADS
GLA_GATE_RANK = 16
GLA_GATE_TAU = 16.0
GLA_CHUNK = 64
GLA_MAIN = 2 * GLA_DK + 2 * GLA_DV

SWA_HEAD_DIM = 64
SWA_Q_HEADS = D_MODEL // SWA_HEAD_DIM
SWA_GROUP = 8
SWA_KV_HEADS = SWA_Q_HEADS // SWA_GROUP
SWA_PAIRS = SWA_GROUP // 2
SWA_PAIRS_PER_GROUP = 1
SWA_WINDOW = 128
SWA_BLOCK = 128

REL_BUCKETS = 32
REL_MAX_DIST = 128

FFN_DIM = 7 * D_MODEL // 2
N_EXPERTS = 8
TOP_K = 2
NEG_INF = -1e9

LANES = 128
VMEM_LIMIT = 56 * 1024 * 1024
ROW_TILE = 512
PROJ_ROW_TILE = 1376
PROJ_COL_TILE = 1024
FFN_TILE = 512
MOE_ROW_TILE = 1024
MOE_GATHER_ROWS = 80
GLA_ROWS = 512
GLA_SUB_ROWS = 256


def _params(*sem):
    return pltpu.CompilerParams(dimension_semantics=sem, vmem_limit_bytes=VMEM_LIMIT)


def _layer_norm_rows(y, g, b):
    mu = jnp.mean(y, axis=-1, keepdims=True)
    yc = y - mu
    var = jnp.mean(yc * yc, axis=-1, keepdims=True)
    return yc * lax.rsqrt(var + LN_EPS) * g + b


def _dot(a, b):
    return jnp.dot(a, b, preferred_element_type=F32)


def _dot_nt(a, b):
    return lax.dot_general(a, b, (((1,), (1,)), ((), ())), preferred_element_type=F32)


def _dot_tn(a, b, precision=None):
    return lax.dot_general(a, b, (((0,), (0,)), ((), ())), preferred_element_type=F32,
                           precision=precision)


def _linear_kernel(x_ref, w_ref, o_ref, *, scale):
    acc = _dot(x_ref[...], w_ref[...])
    if scale != 1.0:
        acc = acc * scale
    o_ref[...] = acc.astype(o_ref.dtype)


def linear(x, w, layer, n_out, out_dtype, scale=1.0):
    rows, k = x.shape
    tm = PROJ_ROW_TILE if rows % PROJ_ROW_TILE == 0 else ROW_TILE
    tn = PROJ_COL_TILE
    return pl.pallas_call(
        functools.partial(_linear_kernel, scale=scale),
        grid=(pl.cdiv(rows, tm), n_out // tn),
        in_specs=[pl.BlockSpec((tm, k), lambda i, j: (i, 0)),
                  pl.BlockSpec((None, k, tn), lambda i, j: (layer, 0, j))],
        out_specs=pl.BlockSpec((tm, tn), lambda i, j: (i, j)),
        out_shape=jax.ShapeDtypeStruct((rows, n_out), out_dtype),
        compiler_params=_params("parallel", "arbitrary"),
        name="linear",
    )(x, w)


def _gla_gate_kernel(x_ref, wg_ref, w2_ref, b_ref, o_ref):
    g_low = _dot(x_ref[...], wg_ref[...])
    z = _dot(g_low.astype(BF16), w2_ref[...]) + b_ref[...]
    log_sig = jnp.minimum(z, 0.0) - jnp.log1p(jnp.exp(-jnp.abs(z)))
    o_ref[...] = log_sig / GLA_GATE_TAU


def gla_gate(x, wg_pad, w2_pad, b_gate):
    rows, k = x.shape
    return pl.pallas_call(
        _gla_gate_kernel,
        grid=(pl.cdiv(rows, ROW_TILE),),
        in_specs=[pl.BlockSpec((ROW_TILE, k), lambda i: (i, 0)),
                  pl.BlockSpec((k, LANES), lambda i: (0, 0)),
                  pl.BlockSpec((LANES, GLA_DK), lambda i: (0, 0)),
                  pl.BlockSpec((1, GLA_DK), lambda i: (0, 0))],
        out_specs=pl.BlockSpec((ROW_TILE, GLA_DK), lambda i: (i, 0)),
        out_shape=jax.ShapeDtypeStruct((rows, GLA_DK), F32),
        compiler_params=_params("parallel"),
        name="gla_gate",
    )(x, wg_pad, w2_pad, b_gate)


def _linear_res_ln_kernel(x_ref, w_ref, res_ref, g_ref, b_ref, o_ref, ob_ref):
    n_part = 4
    part = x_ref.shape[0] // n_part
    parts = [slice(k * part, (k + 1) * part) for k in range(n_part)]
    ys = [DN_ALPHA * res_ref[sl, :] + _dot(x_ref[sl, :], w_ref[...]) for sl in parts]
    for sl, y in zip(parts, ys):
        o = _layer_norm_rows(y, g_ref[...], b_ref[...])
        o_ref[sl, :] = o
        ob_ref[sl, :] = o.astype(BF16)


def linear_res_ln(x, w, layer, res, g, b):
    rows, k = x.shape
    d = w.shape[-1]
    return pl.pallas_call(
        _linear_res_ln_kernel,
        grid=(pl.cdiv(rows, ROW_TILE),),
        in_specs=[pl.BlockSpec((ROW_TILE, k), lambda i: (i, 0)),
                  pl.BlockSpec((None, k, d), lambda i: (layer, 0, 0)),
                  pl.BlockSpec((ROW_TILE, d), lambda i: (i, 0)),
                  pl.BlockSpec((1, d), lambda i: (0, 0)),
                  pl.BlockSpec((1, d), lambda i: (0, 0))],
        out_specs=[pl.BlockSpec((ROW_TILE, d), lambda i: (i, 0)),
                   pl.BlockSpec((ROW_TILE, d), lambda i: (i, 0))],
        out_shape=[jax.ShapeDtypeStruct((rows, d), F32), jax.ShapeDtypeStruct((rows, d), BF16)],
        compiler_params=_params("parallel"),
        name="linear_res_ln",
    )(x, w, res, g, b)


def _swiglu_partial(x, wa, wu, wd):
    a = _dot(x, wa)
    u = _dot(x, wu)
    hidden = (a * jax.nn.sigmoid(a) * u).astype(BF16)
    return _dot(hidden, wd)


def _ffn_kernel(x_ref, wa_ref, wu_ref, wd_ref, res_ref, g_ref, b_ref, o_ref, ob_ref, acc_ref):
    j = pl.program_id(1)

    @pl.when(j == 0)
    def _():
        acc_ref[...] = jnp.zeros_like(acc_ref)

    acc_ref[...] += _swiglu_partial(x_ref[...], wa_ref[...], wu_ref[...], wd_ref[...])

    @pl.when(j == pl.num_programs(1) - 1)
    def _():
        o = _layer_norm_rows(DN_ALPHA * res_ref[...] + acc_ref[...], g_ref[...], b_ref[...])
        o_ref[...] = o
        ob_ref[...] = o.astype(BF16)


def ffn_res_ln(xb, res, w_gu, w_down, layer, g, b):
    rows, d = xb.shape
    f = w_down.shape[-2]
    nj = f // FFN_TILE
    return pl.pallas_call(
        _ffn_kernel,
        grid=(pl.cdiv(rows, ROW_TILE), nj),
        in_specs=[pl.BlockSpec((ROW_TILE, d), lambda i, j: (i, 0)),
                  pl.BlockSpec((None, d, FFN_TILE), lambda i, j: (layer, 0, j)),
                  pl.BlockSpec((None, d, FFN_TILE), lambda i, j: (layer, 0, nj + j)),
                  pl.BlockSpec((None, FFN_TILE, d), lambda i, j: (layer, j, 0)),
                  pl.BlockSpec((ROW_TILE, d), lambda i, j: (i, 0)),
                  pl.BlockSpec((1, d), lambda i, j: (0, 0)),
                  pl.BlockSpec((1, d), lambda i, j: (0, 0))],
        out_specs=[pl.BlockSpec((ROW_TILE, d), lambda i, j: (i, 0)),
                   pl.BlockSpec((ROW_TILE, d), lambda i, j: (i, 0))],
        out_shape=[jax.ShapeDtypeStruct((rows, d), F32), jax.ShapeDtypeStruct((rows, d), BF16)],
        scratch_shapes=[pltpu.VMEM((ROW_TILE, d), F32)],
        compiler_params=_params("parallel", "arbitrary"),
        name="ffn_res_ln",
    )(xb, w_gu, w_gu, w_down, res, g, b)


def _router_kernel(x_ref, w_ref, o_ref):
    logits = _dot(x_ref[...], w_ref[...])
    lane = lax.broadcasted_iota(jnp.int32, logits.shape, 1)
    lane_f = lane.astype(F32)
    logits = jnp.where(lane < N_EXPERTS, logits, -jnp.inf)
    m1 = jnp.max(logits, axis=-1, keepdims=True)
    i1 = jnp.min(jnp.where(logits == m1, lane_f, float(LANES)), axis=-1, keepdims=True)
    rest = jnp.where(lane_f == i1, -jnp.inf, logits)
    m2 = jnp.max(rest, axis=-1, keepdims=True)
    i2 = jnp.min(jnp.where(rest == m2, lane_f, float(LANES)), axis=-1, keepdims=True)
    e2 = jnp.exp(m2 - m1)
    g1 = 1.0 / (1.0 + e2)
    g2 = e2 / (1.0 + e2)
    o_ref[...] = jnp.where(lane == 0, i1, jnp.where(lane == 1, i2,
                           jnp.where(lane == 2, g1, jnp.where(lane == 3, g2, 0.0))))


def router(xb, w_router_pad):
    rows, d = xb.shape
    return pl.pallas_call(
        _router_kernel,
        grid=(pl.cdiv(rows, ROW_TILE),),
        in_specs=[pl.BlockSpec((ROW_TILE, d), lambda i: (i, 0)),
                  pl.BlockSpec((d, LANES), lambda i: (0, 0))],
        out_specs=pl.BlockSpec((ROW_TILE, LANES), lambda i: (i, 0)),
        out_shape=jax.ShapeDtypeStruct((rows, LANES), F32),
        compiler_params=_params("parallel"),
        name="router",
    )(xb, w_router_pad)


def _moe_kernel(blk_e_ref, n_half_ref, tok_ref, h_hbm, wa_ref, wu_ref, wd_ref, o_ref,
                land_ref, x_ref, sem):
    del blk_e_ref
    i = pl.program_id(0)
    j = pl.program_id(1)
    n_blk = pl.num_programs(0)
    nj = pl.num_programs(1)
    n_half = n_half_ref[i]
    half = MOE_ROW_TILE // 2
    land_rows = land_ref.shape[0]

    def row_copy(slot, dst_row):
        return pltpu.make_async_copy(h_hbm.at[pl.ds(tok_ref[slot], 1), :],
                                     land_ref.at[pl.ds(dst_row, 1), :], sem)

    def wait_landing():
        pltpu.make_async_copy(h_hbm.at[pl.ds(0, land_rows), :], land_ref, sem).wait()

    @pl.when(j == 0)
    def _():
        o_ref[...] = jnp.zeros_like(o_ref)

        @pl.when(i == 0)
        def _():
            def issue(r, carry):
                row_copy(r, r).start()
                return carry
            lax.fori_loop(0, land_rows, issue, 0, unroll=8)

        @pl.when(jnp.logical_or(i == 0, n_half_ref[jnp.maximum(i - 1, 0)] > 0))
        def _():
            wait_landing()

        for hf in range(2):
            @pl.when(n_half > hf)
            def _():
                sl = slice(hf * half, (hf + 1) * half)
                x_ref[sl, :] = land_ref[sl, :].astype(BF16)

    @pl.when(n_half > 0)
    def _():
        next_base = jnp.minimum(i + 1, n_blk - 1) * MOE_ROW_TILE
        r0 = j * MOE_GATHER_ROWS
        for k in range(MOE_GATHER_ROWS):
            row_copy(next_base + r0 + k, r0 + k).start()
        wa = wa_ref[...].astype(BF16)
        wu = wu_ref[...].astype(BF16)
        wd = wd_ref[...].astype(BF16)
        o_ref[:half, :] += _swiglu_partial(x_ref[:half, :], wa, wu, wd)

        @pl.when(n_half > 1)
        def _():
            o_ref[half:, :] += _swiglu_partial(x_ref[half:, :], wa, wu, wd)

    @pl.when(jnp.logical_and(jnp.logical_and(i == n_blk - 1, j == nj - 1), n_half > 0))
    def _():
        wait_landing()


def moe_experts(h, buf_tok, blk_e, n_half, w_gu, w_down, layer):
    d = h.shape[1]
    p_rows = buf_tok.shape[0]
    f = w_down.shape[-2]
    nj = f // FFN_TILE
    n_blk = p_rows // MOE_ROW_TILE
    assert nj * MOE_GATHER_ROWS >= MOE_ROW_TILE
    buf_tok = jnp.pad(buf_tok, (0, nj * MOE_GATHER_ROWS - MOE_ROW_TILE))

    def jj(i, j, nh):
        return jnp.where(nh[i] > 0, j, nj - 1)

    grid_spec = pltpu.PrefetchScalarGridSpec(
        num_scalar_prefetch=3,
        grid=(n_blk, nj),
        in_specs=[pl.BlockSpec(memory_space=pl.ANY),
                  pl.BlockSpec((None, None, d, FFN_TILE),
                               lambda i, j, be, nh, tk: (layer, be[i], 0, jj(i, j, nh))),
                  pl.BlockSpec((None, None, d, FFN_TILE),
                               lambda i, j, be, nh, tk: (layer, be[i], 0, nj + jj(i, j, nh))),
                  pl.BlockSpec((None, None, FFN_TILE, d),
                               lambda i, j, be, nh, tk: (layer, be[i], jj(i, j, nh), 0))],
        out_specs=pl.BlockSpec((MOE_ROW_TILE, d), lambda i, j, be, nh, tk: (i, 0),
                               pipeline_mode=pl.Buffered(1)),
        scratch_shapes=[pltpu.VMEM((nj * MOE_GATHER_ROWS, d), F32),
                        pltpu.VMEM((MOE_ROW_TILE, d), BF16),
                        pltpu.SemaphoreType.DMA(())],
    )
    return pl.pallas_call(
        _moe_kernel,
        grid_spec=grid_spec,
        out_shape=jax.ShapeDtypeStruct((p_rows, d), F32),
        compiler_params=_params("arbitrary", "arbitrary"),
        name="moe_experts",
    )(blk_e, n_half, buf_tok, h, w_gu, w_gu, w_down)


def _combine_ln_kernel(res_ref, y0_ref, y1_ref, gate_ref, g_ref, b_ref, o_ref, ob_ref):
    gate = gate_ref[...]
    moe = y0_ref[...] * gate[:, 2:3] + y1_ref[...] * gate[:, 3:4]
    o = _layer_norm_rows(DN_ALPHA * res_ref[...] + moe, g_ref[...], b_ref[...])
    o_ref[...] = o
    ob_ref[...] = o.astype(BF16)


def combine_ln(res, y0, y1, route, g, b, out_rows):
    d = res.shape[1]
    row_spec = pl.BlockSpec((ROW_TILE, d), lambda i: (i, 0))
    vec_spec = pl.BlockSpec((1, d), lambda i: (0, 0))
    return pl.pallas_call(
        _combine_ln_kernel,
        grid=(pl.cdiv(out_rows, ROW_TILE),),
        in_specs=[row_spec, row_spec, row_spec, pl.BlockSpec((ROW_TILE, LANES), lambda i: (i, 0)),
                  vec_spec, vec_spec],
        out_specs=[row_spec, row_spec],
        out_shape=[jax.ShapeDtypeStruct((out_rows, d), F32),
                   jax.ShapeDtypeStruct((out_rows, d), BF16)],
        compiler_params=_params("parallel"),
        name="combine_ln",
    )(res, y0, y1, route, g, b)


def moe_res_ln(h, hb, w_router_pad, w_gu, w_down, layer, g, b, out_rows):
    rows, d = h.shape
    route = router(hb, w_router_pad)
    flat_e = route[:, :TOP_K].astype(jnp.int32).reshape(-1)
    n_flat = rows * TOP_K
    n_blk = -(-n_flat // MOE_ROW_TILE) + N_EXPERTS
    p_rows = n_blk * MOE_ROW_TILE
    onehot = (flat_e[:, None] == jnp.arange(N_EXPERTS, dtype=jnp.int32)[None, :]).astype(jnp.int32)
    csum = jnp.cumsum(onehot, axis=0)
    rank = jnp.sum(csum * onehot, axis=1) - 1
    counts = csum[-1]
    padded = ((counts + MOE_ROW_TILE - 1) // MOE_ROW_TILE) * MOE_ROW_TILE
    padded_end = jnp.cumsum(padded)
    start_padded = padded_end - padded
    pos = start_padded[flat_e] + rank
    flat_tok = jnp.arange(n_flat, dtype=jnp.int32) // TOP_K
    buf_tok = jnp.zeros((p_rows,), jnp.int32).at[pos].set(flat_tok)
    blk_start = jnp.arange(n_blk, dtype=jnp.int32) * MOE_ROW_TILE
    blk_e = jnp.minimum(jnp.searchsorted(padded_end, blk_start, side='right'),
                        N_EXPERTS - 1).astype(jnp.int32)
    valid = jnp.clip(counts[blk_e] - (blk_start - start_padded[blk_e]), 0, MOE_ROW_TILE)
    valid = jnp.where(blk_start < padded_end[-1], valid, 0)
    half = MOE_ROW_TILE // 2
    n_half = ((valid + half - 1) // half).astype(jnp.int32)
    y = moe_experts(h, buf_tok, blk_e, n_half, w_gu, w_down, layer)
    pos2 = pos.reshape(rows, TOP_K)
    y0 = y.at[pos2[:, 0]].get(mode="promise_in_bounds")
    y1 = y.at[pos2[:, 1]].get(mode="promise_in_bounds")
    return combine_ln(h, y0, y1, route, g, b, out_rows)


def _head_norm_gate(o, r, gain):
    mu = jnp.mean(o, axis=-1, keepdims=True)
    oc = o - mu
    var = jnp.mean(oc * oc, axis=-1, keepdims=True)
    on = oc * lax.rsqrt(var + LN_EPS) * gain
    return (on * (r * jax.nn.sigmoid(r))).astype(BF16)


def _masked_cumsum(mask, x):
    m = mask.astype(BF16)
    hi = x.astype(BF16)
    r1 = x - hi.astype(F32)
    mid = r1.astype(BF16)
    lo = (r1 - mid.astype(F32)).astype(BF16)
    return _dot(m, hi) + _dot(m, mid) + _dot(m, lo)


def _gla_first_chunk(q, k, v, r, la, gain):
    c = q.shape[0]
    row = lax.broadcasted_iota(jnp.int32, (c, c), 0)
    col = lax.broadcasted_iota(jnp.int32, (c, c), 1)
    causal = row >= col
    bcum = _masked_cumsum(causal, la)
    q_dec = (q * (GLA_DK_HEAD ** -0.5) * jnp.exp(bcum)).astype(BF16)
    k_inv = (k * jnp.exp(-bcum)).astype(BF16)
    vb = v.astype(BF16)
    att = jnp.where(causal, _dot_nt(q_dec, k_inv), 0.0)
    o = _dot(att.astype(BF16), vb)
    k_state = (k * jnp.exp(bcum[c - 1:c, :] - bcum)).astype(BF16)
    return _head_norm_gate(o, r, gain), _dot_tn(k_state, vb)


def _gla_intra(q, k, v, la):
    n = q.shape[0]
    c = GLA_CHUNK
    row = lax.broadcasted_iota(jnp.int32, (n, n), 0)
    col = lax.broadcasted_iota(jnp.int32, (n, n), 1)
    causal = (row >= col) & ((row // c) == (col // c))
    bcum = _masked_cumsum(causal, la)
    q_dec = (q * (GLA_DK_HEAD ** -0.5) * jnp.exp(bcum)).astype(BF16)
    k_inv = (k * jnp.exp(-bcum)).astype(BF16)
    vb = v.astype(BF16)
    att = jnp.where(causal, _dot_nt(q_dec, k_inv), 0.0)
    o_intra = _dot(att.astype(BF16), vb)
    b_last = [bcum[(ci + 1) * c - 1:(ci + 1) * c, :] for ci in range(n // c)]
    b_tot = jnp.concatenate([jnp.broadcast_to(bl, (c, bl.shape[1])) for bl in b_last], axis=0)
    k_state = (k * jnp.exp(b_tot - bcum)).astype(BF16)
    return q_dec, k_state, vb, o_intra, b_last


def _gla_walk(state, q_dec, k_state, vb, b_last):
    c = GLA_CHUNK
    o_inter = []
    for ci, bl in enumerate(b_last):
        sl = slice(ci * c, (ci + 1) * c)
        o_inter.append(_dot(q_dec[sl], state.astype(BF16)))
        decay = jnp.exp(jnp.transpose(jnp.broadcast_to(bl, (LANES, bl.shape[1]))))
        decay = jnp.concatenate([decay] * (GLA_DV_HEAD // LANES), axis=1)
        state = state * decay + _dot_tn(k_state[sl], vb[sl])
    return jnp.concatenate(o_inter, axis=0), state


def _gla_kernel(qm_ref, km_ref, vm_ref, rm_ref, lam_ref, q_ref, k_ref, v_ref, r_ref, la_ref,
                gain_ref, o_ref, state_ref, *, blocks_per_seq):
    s = pl.program_id(1)
    gain = gain_ref[...]
    n_batch = state_ref.shape[0]

    @pl.when(s == 0)
    def _():
        for bi in range(n_batch):
            sl = slice(bi * N_META, (bi + 1) * N_META)
            og, st = _gla_first_chunk(qm_ref[sl, :], km_ref[sl, :], vm_ref[sl, :], rm_ref[sl, :],
                                      lam_ref[sl, :], gain)
            state_ref[bi] = st
            o_ref[sl, :] = og

    @pl.when(s > 0)
    def _():
        bi = (s - 1) // blocks_per_seq
        subs = [slice(r0, r0 + GLA_SUB_ROWS) for r0 in range(0, q_ref.shape[0], GLA_SUB_ROWS)]
        intra = [_gla_intra(q_ref[sl, :], k_ref[sl, :], v_ref[sl, :], la_ref[sl, :]) for sl in subs]
        st = state_ref[bi]
        for sl, (q_dec, k_state, vb, o_intra, b_last) in zip(subs, intra):
            o_inter, st = _gla_walk(st, q_dec, k_state, vb, b_last)
            o_ref[sl, :] = _head_norm_gate(o_intra + o_inter, r_ref[sl, :], gain)
        state_ref[bi] = st


def gla_mix(qkvr, log_a, norm_gain, n_batch, seq):
    rows = qkvr.shape[0]
    n_real = n_batch * seq
    meta_rows = n_batch * N_META
    blocks_per_seq = seq // GLA_ROWS
    n_steps = 1 + n_real // GLA_ROWS
    meta_blk = n_real // meta_rows
    meta_out_blk = n_real // GLA_ROWS
    dk, dv, nh = GLA_DK_HEAD, GLA_DV_HEAD, GLA_HEADS
    kq, kk, kv_, kr = 0, GLA_DK // dk, 2 * GLA_DK // dv, (2 * GLA_DK + GLA_DV) // dv

    def real_blk(s):
        return jnp.maximum(s - 1, 0)

    def meta_spec(width, col0):
        return pl.BlockSpec((meta_rows, width), lambda h, s: (meta_blk, col0 + h))

    def real_spec(width, col0):
        return pl.BlockSpec((GLA_ROWS, width), lambda h, s: (real_blk(s), col0 + h))

    return pl.pallas_call(
        functools.partial(_gla_kernel, blocks_per_seq=blocks_per_seq),
        grid=(nh, n_steps),
        in_specs=[meta_spec(dk, kq), meta_spec(dk, kk), meta_spec(dv, kv_), meta_spec(dv, kr),
                  meta_spec(dk, 0),
                  real_spec(dk, kq), real_spec(dk, kk), real_spec(dv, kv_), real_spec(dv, kr),
                  real_spec(dk, 0),
                  pl.BlockSpec((1, dv), lambda h, s: (0, h))],
        out_specs=pl.BlockSpec((GLA_ROWS, dv),
                               lambda h, s: (jnp.where(s == 0, meta_out_blk, s - 1), h)),
        out_shape=jax.ShapeDtypeStruct((rows, GLA_DV), BF16),
        scratch_shapes=[pltpu.VMEM((n_batch, dk, dv), F32)],
        compiler_params=_params("parallel", "arbitrary"),
        name="gla_mix",
    )(qkvr, qkvr, qkvr, qkvr, log_a, qkvr, qkvr, qkvr, qkvr, log_a, norm_gain)


def _block_diag2(x2):
    lane = lax.broadcasted_iota(jnp.int32, x2.shape, 1)
    zero = jnp.zeros_like(x2)
    return jnp.concatenate([jnp.where(lane < SWA_HEAD_DIM, x2, zero),
                            jnp.where(lane >= SWA_HEAD_DIM, x2, zero)], axis=0)


def _swa_kernel(q_ref, km_ref, kp_ref, kc_ref, vm_ref, vp_ref, vc_ref,
                bm_ref, bp_ref, bc_ref, o_ref, *, blocks_per_seq):
    s = pl.program_id(1)
    bi = jnp.maximum(s - 1, 0) // blocks_per_seq
    m0 = pl.multiple_of(bi * N_META, N_META)
    pad = jnp.zeros((SWA_BLOCK - N_META, LANES), BF16)
    k_m = jnp.concatenate([km_ref[pl.ds(m0, N_META), :], pad], axis=0)
    v_m = jnp.concatenate([vm_ref[pl.ds(m0, N_META), :], pad], axis=0)
    kbd = [_block_diag2(k_m), _block_diag2(kp_ref[...]), _block_diag2(kc_ref[...])]
    vbd = [_block_diag2(v_m), _block_diag2(vp_ref[...]), _block_diag2(vc_ref[...])]
    bias_refs = (bm_ref, bp_ref, bc_ref)
    groups = [range(g0, g0 + SWA_PAIRS_PER_GROUP) for g0 in range(0, SWA_PAIRS, SWA_PAIRS_PER_GROUP)]
    scores = []
    for pairs in groups:
        rows = slice(pairs[0] * SWA_BLOCK, (pairs[-1] + 1) * SWA_BLOCK)
        qp = jnp.concatenate([q_ref[:, p * LANES:(p + 1) * LANES] for p in pairs], axis=0)
        scores.append([_dot_nt(qp, kbd[t]) + bias_refs[t][0, 0, rows, :] for t in range(3)])
    for pairs, sc in zip(groups, scores):
        top = jnp.maximum(jnp.maximum(sc[0], sc[1]), sc[2])
        m_even = jnp.max(top[:, :LANES], axis=-1, keepdims=True)
        m_odd = jnp.max(top[:, LANES:], axis=-1, keepdims=True)
        lane2 = lax.broadcasted_iota(jnp.int32, top.shape, 1)
        m_full = jnp.where(lane2 < LANES, m_even, m_odd)
        pr = [jnp.exp(sc[t] - m_full) for t in range(3)]
        p_sum = pr[0] + pr[1] + pr[2]
        l_even = jnp.sum(p_sum[:, :LANES], axis=-1, keepdims=True)
        l_odd = jnp.sum(p_sum[:, LANES:], axis=-1, keepdims=True)
        o = (_dot(pr[0].astype(BF16), vbd[0]) + _dot(pr[1].astype(BF16), vbd[1])
             + _dot(pr[2].astype(BF16), vbd[2]))
        lane1 = lax.broadcasted_iota(jnp.int32, o.shape, 1)
        o = o / jnp.where(lane1 < SWA_HEAD_DIM, l_even, l_odd)
        for k, p in enumerate(pairs):
            o_ref[:, p * LANES:(p + 1) * LANES] = (
                o[k * SWA_BLOCK:(k + 1) * SWA_BLOCK].astype(o_ref.dtype))


def swa_mix(q, kv2, bias_m, bias_p, bias_c, n_batch, seq):
    rows = q.shape[0]
    blocks_per_seq = seq // SWA_BLOCK
    n_real_blk = n_batch * blocks_per_seq
    n_steps = 1 + n_real_blk
    meta_blk = n_real_blk
    assert n_batch * N_META == SWA_BLOCK
    kvn, qw = SWA_KV_HEADS, SWA_GROUP * SWA_HEAD_DIM

    def cur_blk(s):
        return jnp.where(s == 0, meta_blk, s - 1)

    def prev_blk(s):
        return jnp.maximum(s - 2, 0)

    def variant(s):
        return jnp.where(s == 0, 2, jnp.where((s - 1) % blocks_per_seq == 0, 0, 1))

    def kv_spec(head0, blk_fn):
        return pl.BlockSpec((SWA_BLOCK, LANES), lambda kh, s: (blk_fn(s), head0 + kh))

    bias_spec = pl.BlockSpec((1, 1, SWA_PAIRS * SWA_BLOCK, 2 * SWA_BLOCK),
                             lambda kh, s: (variant(s), kh, 0, 0))
    return pl.pallas_call(
        functools.partial(_swa_kernel, blocks_per_seq=blocks_per_seq),
        grid=(kvn, n_steps),
        in_specs=[pl.BlockSpec((SWA_BLOCK, qw), lambda kh, s: (cur_blk(s), kh)),
                  kv_spec(0, lambda s: meta_blk), kv_spec(0, prev_blk), kv_spec(0, cur_blk),
                  kv_spec(kvn, lambda s: meta_blk), kv_spec(kvn, prev_blk), kv_spec(kvn, cur_blk),
                  bias_spec, bias_spec, bias_spec],
        out_specs=pl.BlockSpec((SWA_BLOCK, qw), lambda kh, s: (cur_blk(s), kh)),
        out_shape=jax.ShapeDtypeStruct((rows, kvn * qw), BF16),
        compiler_params=_params("parallel", "arbitrary"),
        name="swa_mix",
    )(q, kv2, kv2, kv2, kv2, kv2, kv2, bias_m, bias_p, bias_c)


def _t5_bucket(dist):
    exact = REL_BUCKETS // 2
    d = jnp.maximum(dist, 0)
    df = jnp.maximum(d, 1).astype(F32)
    large = exact + (jnp.log(df / exact) / math.log(REL_MAX_DIST / exact)
                     * (REL_BUCKETS - exact)).astype(jnp.int32)
    large = jnp.minimum(large, REL_BUCKETS - 1)
    return jnp.where(d < exact, d, large)


def _pair_layout(bias):
    nv = bias.shape[0]
    b = bias.reshape(nv, SWA_KV_HEADS, SWA_PAIRS, 2, SWA_BLOCK, SWA_BLOCK)
    b = jnp.transpose(b, (0, 1, 2, 4, 3, 5))
    return b.reshape(nv, SWA_KV_HEADS, SWA_PAIRS * SWA_BLOCK, 2 * SWA_BLOCK)


def _swa_bias_static(table):
    tab = table.astype(F32)
    blk = SWA_BLOCK
    i = jnp.arange(blk, dtype=jnp.int32)[:, None]
    j = jnp.arange(blk, dtype=jnp.int32)[None, :]
    m = jnp.arange(N_META, dtype=jnp.int32)[None, :]

    def lookup(dist):
        onehot = (_t5_bucket(dist)[..., None] == jnp.arange(REL_BUCKETS, dtype=jnp.int32)).astype(F32)
        return jnp.einsum('qsb,bh->hqs', onehot, tab, precision=lax.Precision.HIGHEST)

    neg = jnp.full((SWA_Q_HEADS, blk, blk), NEG_INF, F32)
    cur = jnp.where((i - j >= 0)[None], lookup(i - j), NEG_INF)
    d_prev = blk + i - j
    prev = jnp.where(((d_prev >= 0) & (d_prev < SWA_WINDOW))[None], lookup(d_prev), NEG_INF)
    meta0 = lookup(N_META + i - m)
    meta1 = lookup(N_META + blk + i - m)
    same_seq = (i // N_META) == (j // N_META)
    dm = (i % N_META) - (j % N_META)
    meta_tile = jnp.where((same_seq & (dm >= 0))[None], lookup(dm), NEG_INF)
    bias_p = _pair_layout(jnp.stack([neg, prev, neg]))
    bias_c = _pair_layout(jnp.stack([cur, cur, meta_tile]))
    return bias_p, bias_c, jnp.stack([meta0, meta1])


def _swa_bias_meta(meta01, sinks):
    nh, blk = SWA_Q_HEADS, SWA_BLOCK
    sink_col = jnp.broadcast_to(sinks.astype(F32)[None, :, None, None], (3, nh, blk, 1))
    meta = jnp.concatenate([meta01, jnp.full((1, nh, blk, N_META), NEG_INF, F32)], axis=0)
    rest = jnp.full((3, nh, blk, blk - N_META - 1), NEG_INF, F32)
    return _pair_layout(jnp.concatenate([meta, sink_col, rest], axis=-1))


def kernel(x, meta_tokens, rel_bias_table, ln_gain, ln_bias, gla_w_in, gla_w_gate2, gla_b_gate,
           gla_norm_gain, gla_w_out, kv_w_shared, swa_w_q, swa_sinks, swa_w_out,
           ffn_w_gate_up, ffn_w_down, moe_w_router, moe_w_gate_up, moe_w_down):
    bsz, seq, d = x.shape
    n_real = bsz * seq
    h = jnp.concatenate([x.reshape(n_real, d),
                         jnp.broadcast_to(meta_tokens.astype(x.dtype)[None], (bsz, N_META, d))
                         .reshape(bsz * N_META, d)], axis=0)
    rows = h.shape[0]
    hb = h.astype(BF16)
    bias_p, bias_c, meta01 = _swa_bias_static(rel_bias_table)

    w_in_b = gla_w_in.astype(BF16)
    w_gla_out_b = gla_w_out.astype(BF16)
    w_q_b = swa_w_q.astype(BF16)
    w_swa_out_b = swa_w_out.astype(BF16)
    w_ffn_gu_b = ffn_w_gate_up.astype(BF16)
    w_ffn_down_b = ffn_w_down.astype(BF16)
    hd = SWA_HEAD_DIM
    w_kv = kv_w_shared.reshape(d, 2 * SWA_KV_HEADS, 1, hd)
    w_kv2 = jnp.broadcast_to(w_kv, (d, 2 * SWA_KV_HEADS, 2, hd)).reshape(1, d, 4 * SWA_KV_HEADS * hd)
    w_kv2 = w_kv2.astype(BF16)

    kv2 = None
    for li in range(DEPTH):
        g0, b0 = ln_gain[li, 0][None, :], ln_bias[li, 0][None, :]
        g1, b1 = ln_gain[li, 1][None, :], ln_bias[li, 1][None, :]
        if li < N_A_LAYERS:
            qkvr = linear(hb, w_in_b, li, GLA_MAIN, F32)
            wg_pad = jnp.pad(w_in_b[li, :, GLA_MAIN:], ((0, 0), (0, LANES - GLA_GATE_RANK)))
            w2_pad = jnp.pad(gla_w_gate2[li].astype(BF16), ((0, LANES - GLA_GATE_RANK), (0, 0)))
            log_a = gla_gate(hb, wg_pad, w2_pad, gla_b_gate[li][None, :])
            mix_in = gla_mix(qkvr, log_a, gla_norm_gain[li][None, :], bsz, seq)
            h, hb = linear_res_ln(mix_in, w_gla_out_b, li, h, g0, b0)
        else:
            jb = li - N_A_LAYERS
            q = linear(hb, w_q_b, jb, d, BF16, scale=hd ** -0.5)
            bias_m = _swa_bias_meta(meta01, swa_sinks[jb])
            mix_in = swa_mix(q, kv2, bias_m, bias_p, bias_c, bsz, seq)
            h, hb = linear_res_ln(mix_in, w_swa_out_b, jb, h, g0, b0)
        if li % 2 == 0:
            h, hb = ffn_res_ln(hb, h, w_ffn_gu_b, w_ffn_down_b, li // 2, g1, b1)
        else:
            w_r = jnp.pad(moe_w_router[li // 2].astype(BF16), ((0, 0), (0, LANES - N_EXPERTS)))
            out_rows = n_real if li == DEPTH - 1 else rows
            h, hb = moe_res_ln(h, hb, w_r, moe_w_gate_up, moe_w_down, li // 2, g1, b1, out_rows)
        if li == N_A_LAYERS - 1:
            kv2 = linear(hb, w_kv2, 0, w_kv2.shape[-1], BF16)
    return h.reshape(bsz, seq, d)
```

```python
import functools
import math

import jax
import jax.numpy as jnp
from jax import lax
from jax.experimental import pallas as pl
from jax.experimental.pallas import tpu as pltpu

F32 = jnp.float32
BF16 = jnp.bfloat16

D_MODEL = 2048
DEPTH = 4
N_META = 16
N_A_LAYERS = DEPTH // 2
DN_ALPHA = (2 * DEPTH) ** 0.25
LN_EPS = 1e-5

GLA_HEADS = 4
GLA_DK = D_MODEL // 2
GLA_DV = D_MODEL
GLA_DK_HEAD = GLA_DK // GLA_HEADS
GLA_DV_HEAD = GLA_DV // GLA_HEADS
GLA_GATE_RANK = 16
GLA_GATE_TAU = 16.0
GLA_CHUNK = 64
GLA_MAIN = 2 * GLA_DK + 2 * GLA_DV

SWA_HEAD_DIM = 64
SWA_Q_HEADS = D_MODEL // SWA_HEAD_DIM
SWA_GROUP = 8
SWA_KV_HEADS = SWA_Q_HEADS // SWA_GROUP
SWA_PAIRS = SWA_GROUP // 2
SWA_PAIRS_PER_GROUP = 1
SWA_WINDOW = 128
SWA_BLOCK = 128

REL_BUCKETS = 32
REL_MAX_DIST = 128

FFN_DIM = 7 * D_MODEL // 2
N_EXPERTS = 8
TOP_K = 2
NEG_INF = -1e9

LANES = 128
VMEM_LIMIT = 56 * 1024 * 1024
ROW_TILE = 512
PROJ_ROW_TILE = 1376
PROJ_COL_TILE = 1024
FFN_TILE = 512
FFN_W_CHUNKS = 2
MOE_W_CHUNKS = 4
MOE_ROW_TILE = 1024
MOE_GATHER_ROWS = 80
GLA_ROWS = 512
GLA_SUB_ROWS = 256


def _params(*sem):
    return pltpu.CompilerParams(dimension_semantics=sem, vmem_limit_bytes=VMEM_LIMIT)


def _layer_norm_rows(y, g, b):
    mu = jnp.mean(y, axis=-1, keepdims=True)
    yc = y - mu
    var = jnp.mean(yc * yc, axis=-1, keepdims=True)
    return yc * lax.rsqrt(var + LN_EPS) * g + b


def _dot(a, b):
    return jnp.dot(a, b, preferred_element_type=F32)


def _dot_nt(a, b):
    return lax.dot_general(a, b, (((1,), (1,)), ((), ())), preferred_element_type=F32)


def _dot_tn(a, b, precision=None):
    return lax.dot_general(a, b, (((0,), (0,)), ((), ())), preferred_element_type=F32,
                           precision=precision)


def _linear_kernel(x_ref, w_ref, o_ref, *, scale):
    acc = _dot(x_ref[...], w_ref[...])
    if scale != 1.0:
        acc = acc * scale
    o_ref[...] = acc.astype(o_ref.dtype)


def linear(x, w, layer, n_out, out_dtype, scale=1.0):
    rows, k = x.shape
    tm = PROJ_ROW_TILE if rows % PROJ_ROW_TILE == 0 else ROW_TILE
    tn = PROJ_COL_TILE
    return pl.pallas_call(
        functools.partial(_linear_kernel, scale=scale),
        grid=(pl.cdiv(rows, tm), n_out // tn),
        in_specs=[pl.BlockSpec((tm, k), lambda i, j: (i, 0)),
                  pl.BlockSpec((None, k, tn), lambda i, j: (layer, 0, j))],
        out_specs=pl.BlockSpec((tm, tn), lambda i, j: (i, j)),
        out_shape=jax.ShapeDtypeStruct((rows, n_out), out_dtype),
        compiler_params=_params("parallel", "arbitrary"),
        name="linear",
    )(x, w)


def _gla_gate_kernel(x_ref, wg_ref, w2_ref, b_ref, o_ref):
    g_low = _dot(x_ref[...], wg_ref[...])
    z = _dot(g_low.astype(BF16), w2_ref[...]) + b_ref[...]
    log_sig = jnp.minimum(z, 0.0) - jnp.log1p(jnp.exp(-jnp.abs(z)))
    o_ref[...] = log_sig / GLA_GATE_TAU


def gla_gate(x, wg_pad, w2_pad, b_gate):
    rows, k = x.shape
    return pl.pallas_call(
        _gla_gate_kernel,
        grid=(pl.cdiv(rows, ROW_TILE),),
        in_specs=[pl.BlockSpec((ROW_TILE, k), lambda i: (i, 0)),
                  pl.BlockSpec((k, LANES), lambda i: (0, 0)),
                  pl.BlockSpec((LANES, GLA_DK), lambda i: (0, 0)),
                  pl.BlockSpec((1, GLA_DK), lambda i: (0, 0))],
        out_specs=pl.BlockSpec((ROW_TILE, GLA_DK), lambda i: (i, 0)),
        out_shape=jax.ShapeDtypeStruct((rows, GLA_DK), F32),
        compiler_params=_params("parallel"),
        name="gla_gate",
    )(x, wg_pad, w2_pad, b_gate)


def _linear_res_ln_kernel(x_ref, w_ref, res_ref, g_ref, b_ref, o_ref, ob_ref):
    n_part = 4
    part = x_ref.shape[0] // n_part
    parts = [slice(k * part, (k + 1) * part) for k in range(n_part)]
    ys = [DN_ALPHA * res_ref[sl, :] + _dot(x_ref[sl, :], w_ref[...]) for sl in parts]
    for sl, y in zip(parts, ys):
        o = _layer_norm_rows(y, g_ref[...], b_ref[...])
        o_ref[sl, :] = o
        ob_ref[sl, :] = o.astype(BF16)


def linear_res_ln(x, w, layer, res, g, b):
    rows, k = x.shape
    d = w.shape[-1]
    return pl.pallas_call(
        _linear_res_ln_kernel,
        grid=(pl.cdiv(rows, ROW_TILE),),
        in_specs=[pl.BlockSpec((ROW_TILE, k), lambda i: (i, 0)),
                  pl.BlockSpec((None, k, d), lambda i: (layer, 0, 0)),
                  pl.BlockSpec((ROW_TILE, d), lambda i: (i, 0)),
                  pl.BlockSpec((1, d), lambda i: (0, 0)),
                  pl.BlockSpec((1, d), lambda i: (0, 0))],
        out_specs=[pl.BlockSpec((ROW_TILE, d), lambda i: (i, 0)),
                   pl.BlockSpec((ROW_TILE, d), lambda i: (i, 0))],
        out_shape=[jax.ShapeDtypeStruct((rows, d), F32), jax.ShapeDtypeStruct((rows, d), BF16)],
        compiler_params=_params("parallel"),
        name="linear_res_ln",
    )(x, w, res, g, b)


def _swiglu_partial(x, wa, wu, wd):
    a = _dot(x, wa)
    u = _dot(x, wu)
    hidden = (a * jax.nn.sigmoid(a) * u).astype(BF16)
    return _dot(hidden, wd)


def _load_weight(chunk_refs):
    return jnp.concatenate([r[...].astype(BF16) for r in chunk_refs], axis=0)


def _ffn_kernel(x_ref, *refs):
    nc = FFN_W_CHUNKS
    wa_refs, wu_refs, wd_refs = refs[:nc], refs[nc:2 * nc], refs[2 * nc:3 * nc]
    res_ref, g_ref, b_ref, o_ref, ob_ref, acc_ref = refs[3 * nc:]
    j = pl.program_id(1)

    @pl.when(j == 0)
    def _():
        acc_ref[...] = jnp.zeros_like(acc_ref)

    acc_ref[...] += _swiglu_partial(x_ref[...], _load_weight(wa_refs), _load_weight(wu_refs),
                                    _load_weight(wd_refs))

    @pl.when(j == pl.num_programs(1) - 1)
    def _():
        o = _layer_norm_rows(DN_ALPHA * res_ref[...] + acc_ref[...], g_ref[...], b_ref[...])
        o_ref[...] = o
        ob_ref[...] = o.astype(BF16)


def ffn_res_ln(xb, res, w_gu, w_down, layer, g, b):
    rows, d = xb.shape
    f = w_down.shape[-2]
    nj = f // FFN_TILE
    nc = FFN_W_CHUNKS
    return pl.pallas_call(
        _ffn_kernel,
        grid=(pl.cdiv(rows, ROW_TILE), nj),
        in_specs=[pl.BlockSpec((ROW_TILE, d), lambda i, j: (i, 0))]
        + [pl.BlockSpec((None, d // nc, FFN_TILE), functools.partial(lambda c, i, j: (layer, c, j), c))
           for c in range(nc)]
        + [pl.BlockSpec((None, d // nc, FFN_TILE),
                        functools.partial(lambda c, i, j: (layer, c, nj + j), c)) for c in range(nc)]
        + [pl.BlockSpec((None, FFN_TILE // nc, d),
                        functools.partial(lambda c, i, j: (layer, j * nc + c, 0), c)) for c in range(nc)]
        + [pl.BlockSpec((ROW_TILE, d), lambda i, j: (i, 0)),
                  pl.BlockSpec((1, d), lambda i, j: (0, 0)),
                  pl.BlockSpec((1, d), lambda i, j: (0, 0))],
        out_specs=[pl.BlockSpec((ROW_TILE, d), lambda i, j: (i, 0)),
                   pl.BlockSpec((ROW_TILE, d), lambda i, j: (i, 0))],
        out_shape=[jax.ShapeDtypeStruct((rows, d), F32), jax.ShapeDtypeStruct((rows, d), BF16)],
        scratch_shapes=[pltpu.VMEM((ROW_TILE, d), F32)],
        compiler_params=_params("parallel", "arbitrary"),
        name="ffn_res_ln",
    )(xb, *([w_gu] * (2 * nc) + [w_down] * nc), res, g, b)


def _router_kernel(x_ref, w_ref, o_ref):
    logits = _dot(x_ref[...], w_ref[...])
    lane = lax.broadcasted_iota(jnp.int32, logits.shape, 1)
    lane_f = lane.astype(F32)
    logits = jnp.where(lane < N_EXPERTS, logits, -jnp.inf)
    m1 = jnp.max(logits, axis=-1, keepdims=True)
    i1 = jnp.min(jnp.where(logits == m1, lane_f, float(LANES)), axis=-1, keepdims=True)
    rest = jnp.where(lane_f == i1, -jnp.inf, logits)
    m2 = jnp.max(rest, axis=-1, keepdims=True)
    i2 = jnp.min(jnp.where(rest == m2, lane_f, float(LANES)), axis=-1, keepdims=True)
    e2 = jnp.exp(m2 - m1)
    g1 = 1.0 / (1.0 + e2)
    g2 = e2 / (1.0 + e2)
    o_ref[...] = jnp.where(lane == 0, i1, jnp.where(lane == 1, i2,
                           jnp.where(lane == 2, g1, jnp.where(lane == 3, g2, 0.0))))


def router(xb, w_router_pad):
    rows, d = xb.shape
    return pl.pallas_call(
        _router_kernel,
        grid=(pl.cdiv(rows, ROW_TILE),),
        in_specs=[pl.BlockSpec((ROW_TILE, d), lambda i: (i, 0)),
                  pl.BlockSpec((d, LANES), lambda i: (0, 0))],
        out_specs=pl.BlockSpec((ROW_TILE, LANES), lambda i: (i, 0)),
        out_shape=jax.ShapeDtypeStruct((rows, LANES), F32),
        compiler_params=_params("parallel"),
        name="router",
    )(xb, w_router_pad)


def _moe_kernel(blk_e_ref, n_half_ref, tok_ref, h_hbm, *refs):
    del blk_e_ref
    nc = MOE_W_CHUNKS
    wa_refs, wu_refs, wd_refs = refs[:nc], refs[nc:2 * nc], refs[2 * nc:3 * nc]
    o_ref, land_ref, x_ref, sem = refs[3 * nc:]
    i = pl.program_id(0)
    j = pl.program_id(1)
    n_blk = pl.num_programs(0)
    nj = pl.num_programs(1)
    n_half = n_half_ref[i]
    half = MOE_ROW_TILE // 2
    land_rows = land_ref.shape[0]

    def row_copy(slot, dst_row):
        return pltpu.make_async_copy(h_hbm.at[pl.ds(tok_ref[slot], 1), :],
                                     land_ref.at[pl.ds(dst_row, 1), :], sem)

    def wait_landing():
        pltpu.make_async_copy(h_hbm.at[pl.ds(0, land_rows), :], land_ref, sem).wait()

    @pl.when(j == 0)
    def _():
        o_ref[...] = jnp.zeros_like(o_ref)

        @pl.when(i == 0)
        def _():
            def issue(r, carry):
                row_copy(r, r).start()
                return carry
            lax.fori_loop(0, land_rows, issue, 0, unroll=8)

        @pl.when(jnp.logical_or(i == 0, n_half_ref[jnp.maximum(i - 1, 0)] > 0))
        def _():
            wait_landing()

        for hf in range(2):
            @pl.when(n_half > hf)
            def _():
                sl = slice(hf * half, (hf + 1) * half)
                x_ref[sl, :] = land_ref[sl, :].astype(BF16)

    @pl.when(n_half > 0)
    def _():
        next_base = jnp.minimum(i + 1, n_blk - 1) * MOE_ROW_TILE
        r0 = j * MOE_GATHER_ROWS
        for k in range(MOE_GATHER_ROWS):
            row_copy(next_base + r0 + k, r0 + k).start()
        wa, wu, wd = _load_weight(wa_refs), _load_weight(wu_refs), _load_weight(wd_refs)
        o_ref[:half, :] += _swiglu_partial(x_ref[:half, :], wa, wu, wd)

        @pl.when(n_half > 1)
        def _():
            o_ref[half:, :] += _swiglu_partial(x_ref[half:, :], wa, wu, wd)

    @pl.when(jnp.logical_and(jnp.logical_and(i == n_blk - 1, j == nj - 1), n_half > 0))
    def _():
        wait_landing()


def moe_experts(h, buf_tok, blk_e, n_half, w_gu, w_down, layer):
    d = h.shape[1]
    p_rows = buf_tok.shape[0]
    f = w_down.shape[-2]
    nj = f // FFN_TILE
    n_blk = p_rows // MOE_ROW_TILE
    nc = MOE_W_CHUNKS
    assert nj * MOE_GATHER_ROWS >= MOE_ROW_TILE
    buf_tok = jnp.pad(buf_tok, (0, nj * MOE_GATHER_ROWS - MOE_ROW_TILE))

    def jj(i, j, nh):
        return jnp.where(nh[i] > 0, j, nj - 1)

    grid_spec = pltpu.PrefetchScalarGridSpec(
        num_scalar_prefetch=3,
        grid=(n_blk, nj),
        in_specs=[pl.BlockSpec(memory_space=pl.ANY)]
        + [pl.BlockSpec((None, None, d // nc, FFN_TILE), functools.partial(
            lambda c, i, j, be, nh, tk: (layer, be[i], c, jj(i, j, nh)), c)) for c in range(nc)]
        + [pl.BlockSpec((None, None, d // nc, FFN_TILE), functools.partial(
            lambda c, i, j, be, nh, tk: (layer, be[i], c, nj + jj(i, j, nh)), c)) for c in range(nc)]
        + [pl.BlockSpec((None, None, FFN_TILE // nc, d), functools.partial(
            lambda c, i, j, be, nh, tk: (layer, be[i], jj(i, j, nh) * nc + c, 0), c))
           for c in range(nc)],
        out_specs=pl.BlockSpec((MOE_ROW_TILE, d), lambda i, j, be, nh, tk: (i, 0),
                               pipeline_mode=pl.Buffered(1)),
        scratch_shapes=[pltpu.VMEM((nj * MOE_GATHER_ROWS, d), F32),
                        pltpu.VMEM((MOE_ROW_TILE, d), BF16),
                        pltpu.SemaphoreType.DMA(())],
    )
    return pl.pallas_call(
        _moe_kernel,
        grid_spec=grid_spec,
        out_shape=jax.ShapeDtypeStruct((p_rows, d), F32),
        compiler_params=_params("arbitrary", "arbitrary"),
        name="moe_experts",
    )(blk_e, n_half, buf_tok, h, *([w_gu] * (2 * nc) + [w_down] * nc))


def _combine_ln_kernel(res_ref, y0_ref, y1_ref, gate_ref, g_ref, b_ref, o_ref, ob_ref):
    gate = gate_ref[...]
    moe = y0_ref[...] * gate[:, 2:3] + y1_ref[...] * gate[:, 3:4]
    o = _layer_norm_rows(DN_ALPHA * res_ref[...] + moe, g_ref[...], b_ref[...])
    o_ref[...] = o
    ob_ref[...] = o.astype(BF16)


def combine_ln(res, y0, y1, route, g, b, out_rows):
    d = res.shape[1]
    row_spec = pl.BlockSpec((ROW_TILE, d), lambda i: (i, 0))
    vec_spec = pl.BlockSpec((1, d), lambda i: (0, 0))
    return pl.pallas_call(
        _combine_ln_kernel,
        grid=(pl.cdiv(out_rows, ROW_TILE),),
        in_specs=[row_spec, row_spec, row_spec, pl.BlockSpec((ROW_TILE, LANES), lambda i: (i, 0)),
                  vec_spec, vec_spec],
        out_specs=[row_spec, row_spec],
        out_shape=[jax.ShapeDtypeStruct((out_rows, d), F32),
                   jax.ShapeDtypeStruct((out_rows, d), BF16)],
        compiler_params=_params("parallel"),
        name="combine_ln",
    )(res, y0, y1, route, g, b)


def moe_res_ln(h, hb, w_router_pad, w_gu, w_down, layer, g, b, out_rows):
    rows, d = h.shape
    route = router(hb, w_router_pad)
    flat_e = route[:, :TOP_K].astype(jnp.int32).reshape(-1)
    n_flat = rows * TOP_K
    n_blk = -(-n_flat // MOE_ROW_TILE) + N_EXPERTS
    p_rows = n_blk * MOE_ROW_TILE
    onehot = (flat_e[:, None] == jnp.arange(N_EXPERTS, dtype=jnp.int32)[None, :]).astype(jnp.int32)
    csum = jnp.cumsum(onehot, axis=0)
    rank = jnp.sum(csum * onehot, axis=1) - 1
    counts = csum[-1]
    padded = ((counts + MOE_ROW_TILE - 1) // MOE_ROW_TILE) * MOE_ROW_TILE
    padded_end = jnp.cumsum(padded)
    start_padded = padded_end - padded
    pos = start_padded[flat_e] + rank
    flat_tok = jnp.arange(n_flat, dtype=jnp.int32) // TOP_K
    buf_tok = jnp.zeros((p_rows,), jnp.int32).at[pos].set(flat_tok)
    blk_start = jnp.arange(n_blk, dtype=jnp.int32) * MOE_ROW_TILE
    blk_e = jnp.minimum(jnp.searchsorted(padded_end, blk_start, side='right'),
                        N_EXPERTS - 1).astype(jnp.int32)
    valid = jnp.clip(counts[blk_e] - (blk_start - start_padded[blk_e]), 0, MOE_ROW_TILE)
    valid = jnp.where(blk_start < padded_end[-1], valid, 0)
    half = MOE_ROW_TILE // 2
    n_half = ((valid + half - 1) // half).astype(jnp.int32)
    y = moe_experts(h, buf_tok, blk_e, n_half, w_gu, w_down, layer)
    pos2 = pos.reshape(rows, TOP_K)
    y0 = y.at[pos2[:, 0]].get(mode="promise_in_bounds")
    y1 = y.at[pos2[:, 1]].get(mode="promise_in_bounds")
    return combine_ln(h, y0, y1, route, g, b, out_rows)


def _head_norm_gate(o, r, gain):
    mu = jnp.mean(o, axis=-1, keepdims=True)
    oc = o - mu
    var = jnp.mean(oc * oc, axis=-1, keepdims=True)
    on = oc * lax.rsqrt(var + LN_EPS) * gain
    return (on * (r * jax.nn.sigmoid(r))).astype(BF16)


def _masked_cumsum(mask, x):
    m = mask.astype(BF16)
    hi = x.astype(BF16)
    r1 = x - hi.astype(F32)
    mid = r1.astype(BF16)
    lo = (r1 - mid.astype(F32)).astype(BF16)
    return _dot(m, hi) + _dot(m, mid) + _dot(m, lo)


def _gla_first_chunk(q, k, v, r, la, gain):
    c = q.shape[0]
    row = lax.broadcasted_iota(jnp.int32, (c, c), 0)
    col = lax.broadcasted_iota(jnp.int32, (c, c), 1)
    causal = row >= col
    bcum = _masked_cumsum(causal, la)
    q_dec = (q * (GLA_DK_HEAD ** -0.5) * jnp.exp(bcum)).astype(BF16)
    k_inv = (k * jnp.exp(-bcum)).astype(BF16)
    vb = v.astype(BF16)
    att = jnp.where(causal, _dot_nt(q_dec, k_inv), 0.0)
    o = _dot(att.astype(BF16), vb)
    k_state = (k * jnp.exp(bcum[c - 1:c, :] - bcum)).astype(BF16)
    return _head_norm_gate(o, r, gain), _dot_tn(k_state, vb)


def _gla_intra(q, k, v, la):
    n = q.shape[0]
    c = GLA_CHUNK
    row = lax.broadcasted_iota(jnp.int32, (n, n), 0)
    col = lax.broadcasted_iota(jnp.int32, (n, n), 1)
    causal = (row >= col) & ((row // c) == (col // c))
    bcum = _masked_cumsum(causal, la)
    q_dec = (q * (GLA_DK_HEAD ** -0.5) * jnp.exp(bcum)).astype(BF16)
    k_inv = (k * jnp.exp(-bcum)).astype(BF16)
    vb = v.astype(BF16)
    att = jnp.where(causal, _dot_nt(q_dec, k_inv), 0.0)
    o_intra = _dot(att.astype(BF16), vb)
    b_last = [bcum[(ci + 1) * c - 1:(ci + 1) * c, :] for ci in range(n // c)]
    b_tot = jnp.concatenate([jnp.broadcast_to(bl, (c, bl.shape[1])) for bl in b_last], axis=0)
    k_state = (k * jnp.exp(b_tot - bcum)).astype(BF16)
    return q_dec, k_state, vb, o_intra, b_last


def _gla_walk(state, q_dec, k_state, vb, b_last):
    c = GLA_CHUNK
    o_inter = []
    for ci, bl in enumerate(b_last):
        sl = slice(ci * c, (ci + 1) * c)
        o_inter.append(_dot(q_dec[sl], state.astype(BF16)))
        decay = jnp.exp(jnp.transpose(jnp.broadcast_to(bl, (LANES, bl.shape[1]))))
        decay = jnp.concatenate([decay] * (GLA_DV_HEAD // LANES), axis=1)
        state = state * decay + _dot_tn(k_state[sl], vb[sl])
    return jnp.concatenate(o_inter, axis=0), state


def _gla_kernel(qm_ref, km_ref, vm_ref, rm_ref, lam_ref, q_ref, k_ref, v_ref, r_ref, la_ref,
                gain_ref, o_ref, state_ref, *, blocks_per_seq):
    s = pl.program_id(1)
    gain = gain_ref[...]
    n_batch = state_ref.shape[0]

    @pl.when(s == 0)
    def _():
        for bi in range(n_batch):
            sl = slice(bi * N_META, (bi + 1) * N_META)
            og, st = _gla_first_chunk(qm_ref[sl, :], km_ref[sl, :], vm_ref[sl, :], rm_ref[sl, :],
                                      lam_ref[sl, :], gain)
            state_ref[bi] = st
            o_ref[sl, :] = og

    @pl.when(s > 0)
    def _():
        bi = (s - 1) // blocks_per_seq
        subs = [slice(r0, r0 + GLA_SUB_ROWS) for r0 in range(0, q_ref.shape[0], GLA_SUB_ROWS)]
        intra = [_gla_intra(q_ref[sl, :], k_ref[sl, :], v_ref[sl, :], la_ref[sl, :]) for sl in subs]
        st = state_ref[bi]
        for sl, (q_dec, k_state, vb, o_intra, b_last) in zip(subs, intra):
            o_inter, st = _gla_walk(st, q_dec, k_state, vb, b_last)
            o_ref[sl, :] = _head_norm_gate(o_intra + o_inter, r_ref[sl, :], gain)
        state_ref[bi] = st


def gla_mix(qkvr, log_a, norm_gain, n_batch, seq):
    rows = qkvr.shape[0]
    n_real = n_batch * seq
    meta_rows = n_batch * N_META
    blocks_per_seq = seq // GLA_ROWS
    n_steps = 1 + n_real // GLA_ROWS
    meta_blk = n_real // meta_rows
    meta_out_blk = n_real // GLA_ROWS
    dk, dv, nh = GLA_DK_HEAD, GLA_DV_HEAD, GLA_HEADS
    kq, kk, kv_, kr = 0, GLA_DK // dk, 2 * GLA_DK // dv, (2 * GLA_DK + GLA_DV) // dv

    def real_blk(s):
        return jnp.maximum(s - 1, 0)

    def meta_spec(width, col0):
        return pl.BlockSpec((meta_rows, width), lambda h, s: (meta_blk, col0 + h))

    def real_spec(width, col0):
        return pl.BlockSpec((GLA_ROWS, width), lambda h, s: (real_blk(s), col0 + h))

    return pl.pallas_call(
        functools.partial(_gla_kernel, blocks_per_seq=blocks_per_seq),
        grid=(nh, n_steps),
        in_specs=[meta_spec(dk, kq), meta_spec(dk, kk), meta_spec(dv, kv_), meta_spec(dv, kr),
                  meta_spec(dk, 0),
                  real_spec(dk, kq), real_spec(dk, kk), real_spec(dv, kv_), real_spec(dv, kr),
                  real_spec(dk, 0),
                  pl.BlockSpec((1, dv), lambda h, s: (0, h))],
        out_specs=pl.BlockSpec((GLA_ROWS, dv),
                               lambda h, s: (jnp.where(s == 0, meta_out_blk, s - 1), h)),
        out_shape=jax.ShapeDtypeStruct((rows, GLA_DV), BF16),
        scratch_shapes=[pltpu.VMEM((n_batch, dk, dv), F32)],
        compiler_params=_params("parallel", "arbitrary"),
        name="gla_mix",
    )(qkvr, qkvr, qkvr, qkvr, log_a, qkvr, qkvr, qkvr, qkvr, log_a, norm_gain)


def _block_diag2(x2):
    lane = lax.broadcasted_iota(jnp.int32, x2.shape, 1)
    zero = jnp.zeros_like(x2)
    return jnp.concatenate([jnp.where(lane < SWA_HEAD_DIM, x2, zero),
                            jnp.where(lane >= SWA_HEAD_DIM, x2, zero)], axis=0)


def _swa_kernel(q_ref, km_ref, kp_ref, kc_ref, vm_ref, vp_ref, vc_ref,
                bm_ref, bp_ref, bc_ref, o_ref, *, blocks_per_seq):
    s = pl.program_id(1)
    bi = jnp.maximum(s - 1, 0) // blocks_per_seq
    m0 = pl.multiple_of(bi * N_META, N_META)
    pad = jnp.zeros((SWA_BLOCK - N_META, LANES), BF16)
    k_m = jnp.concatenate([km_ref[pl.ds(m0, N_META), :], pad], axis=0)
    v_m = jnp.concatenate([vm_ref[pl.ds(m0, N_META), :], pad], axis=0)
    kbd = [_block_diag2(k_m), _block_diag2(kp_ref[...]), _block_diag2(kc_ref[...])]
    vbd = [_block_diag2(v_m), _block_diag2(vp_ref[...]), _block_diag2(vc_ref[...])]
    bias_refs = (bm_ref, bp_ref, bc_ref)
    groups = [range(g0, g0 + SWA_PAIRS_PER_GROUP) for g0 in range(0, SWA_PAIRS, SWA_PAIRS_PER_GROUP)]
    scores = []
    for pairs in groups:
        rows = slice(pairs[0] * SWA_BLOCK, (pairs[-1] + 1) * SWA_BLOCK)
        qp = jnp.concatenate([q_ref[:, p * LANES:(p + 1) * LANES] for p in pairs], axis=0)
        scores.append([_dot_nt(qp, kbd[t]) + bias_refs[t][0, 0, rows, :] for t in range(3)])
    for pairs, sc in zip(groups, scores):
        top = jnp.maximum(jnp.maximum(sc[0], sc[1]), sc[2])
        m_even = jnp.max(top[:, :LANES], axis=-1, keepdims=True)
        m_odd = jnp.max(top[:, LANES:], axis=-1, keepdims=True)
        lane2 = lax.broadcasted_iota(jnp.int32, top.shape, 1)
        m_full = jnp.where(lane2 < LANES, m_even, m_odd)
        pr = [jnp.exp(sc[t] - m_full) for t in range(3)]
        p_sum = pr[0] + pr[1] + pr[2]
        l_even = jnp.sum(p_sum[:, :LANES], axis=-1, keepdims=True)
        l_odd = jnp.sum(p_sum[:, LANES:], axis=-1, keepdims=True)
        o = (_dot(pr[0].astype(BF16), vbd[0]) + _dot(pr[1].astype(BF16), vbd[1])
             + _dot(pr[2].astype(BF16), vbd[2]))
        lane1 = lax.broadcasted_iota(jnp.int32, o.shape, 1)
        o = o / jnp.where(lane1 < SWA_HEAD_DIM, l_even, l_odd)
        for k, p in enumerate(pairs):
            o_ref[:, p * LANES:(p + 1) * LANES] = (
                o[k * SWA_BLOCK:(k + 1) * SWA_BLOCK].astype(o_ref.dtype))


def swa_mix(q, kv2, bias_m, bias_p, bias_c, n_batch, seq):
    rows = q.shape[0]
    blocks_per_seq = seq // SWA_BLOCK
    n_real_blk = n_batch * blocks_per_seq
    n_steps = 1 + n_real_blk
    meta_blk = n_real_blk
    assert n_batch * N_META == SWA_BLOCK
    kvn, qw = SWA_KV_HEADS, SWA_GROUP * SWA_HEAD_DIM

    def cur_blk(s):
        return jnp.where(s == 0, meta_blk, s - 1)

    def prev_blk(s):
        return jnp.maximum(s - 2, 0)

    def variant(s):
        return jnp.where(s == 0, 2, jnp.where((s - 1) % blocks_per_seq == 0, 0, 1))

    def kv_spec(head0, blk_fn):
        return pl.BlockSpec((SWA_BLOCK, LANES), lambda kh, s: (blk_fn(s), head0 + kh))

    bias_spec = pl.BlockSpec((1, 1, SWA_PAIRS * SWA_BLOCK, 2 * SWA_BLOCK),
                             lambda kh, s: (variant(s), kh, 0, 0))
    return pl.pallas_call(
        functools.partial(_swa_kernel, blocks_per_seq=blocks_per_seq),
        grid=(kvn, n_steps),
        in_specs=[pl.BlockSpec((SWA_BLOCK, qw), lambda kh, s: (cur_blk(s), kh)),
                  kv_spec(0, lambda s: meta_blk), kv_spec(0, prev_blk), kv_spec(0, cur_blk),
                  kv_spec(kvn, lambda s: meta_blk), kv_spec(kvn, prev_blk), kv_spec(kvn, cur_blk),
                  bias_spec, bias_spec, bias_spec],
        out_specs=pl.BlockSpec((SWA_BLOCK, qw), lambda kh, s: (cur_blk(s), kh)),
        out_shape=jax.ShapeDtypeStruct((rows, kvn * qw), BF16),
        compiler_params=_params("parallel", "arbitrary"),
        name="swa_mix",
    )(q, kv2, kv2, kv2, kv2, kv2, kv2, bias_m, bias_p, bias_c)


def _t5_bucket(dist):
    exact = REL_BUCKETS // 2
    d = jnp.maximum(dist, 0)
    df = jnp.maximum(d, 1).astype(F32)
    large = exact + (jnp.log(df / exact) / math.log(REL_MAX_DIST / exact)
                     * (REL_BUCKETS - exact)).astype(jnp.int32)
    large = jnp.minimum(large, REL_BUCKETS - 1)
    return jnp.where(d < exact, d, large)


def _pair_layout(bias):
    nv = bias.shape[0]
    b = bias.reshape(nv, SWA_KV_HEADS, SWA_PAIRS, 2, SWA_BLOCK, SWA_BLOCK)
    b = jnp.transpose(b, (0, 1, 2, 4, 3, 5))
    return b.reshape(nv, SWA_KV_HEADS, SWA_PAIRS * SWA_BLOCK, 2 * SWA_BLOCK)


def _swa_bias_static(table):
    tab = table.astype(F32)
    blk = SWA_BLOCK
    i = jnp.arange(blk, dtype=jnp.int32)[:, None]
    j = jnp.arange(blk, dtype=jnp.int32)[None, :]
    m = jnp.arange(N_META, dtype=jnp.int32)[None, :]

    def lookup(dist):
        onehot = (_t5_bucket(dist)[..., None] == jnp.arange(REL_BUCKETS, dtype=jnp.int32)).astype(F32)
        return jnp.einsum('qsb,bh->hqs', onehot, tab, precision=lax.Precision.HIGHEST)

    neg = jnp.full((SWA_Q_HEADS, blk, blk), NEG_INF, F32)
    cur = jnp.where((i - j >= 0)[None], lookup(i - j), NEG_INF)
    d_prev = blk + i - j
    prev = jnp.where(((d_prev >= 0) & (d_prev < SWA_WINDOW))[None], lookup(d_prev), NEG_INF)
    meta0 = lookup(N_META + i - m)
    meta1 = lookup(N_META + blk + i - m)
    same_seq = (i // N_META) == (j // N_META)
    dm = (i % N_META) - (j % N_META)
    meta_tile = jnp.where((same_seq & (dm >= 0))[None], lookup(dm), NEG_INF)
    bias_p = _pair_layout(jnp.stack([neg, prev, neg]))
    bias_c = _pair_layout(jnp.stack([cur, cur, meta_tile]))
    return bias_p, bias_c, jnp.stack([meta0, meta1])


def _swa_bias_meta(meta01, sinks):
    nh, blk = SWA_Q_HEADS, SWA_BLOCK
    sink_col = jnp.broadcast_to(sinks.astype(F32)[None, :, None, None], (3, nh, blk, 1))
    meta = jnp.concatenate([meta01, jnp.full((1, nh, blk, N_META), NEG_INF, F32)], axis=0)
    rest = jnp.full((3, nh, blk, blk - N_META - 1), NEG_INF, F32)
    return _pair_layout(jnp.concatenate([meta, sink_col, rest], axis=-1))


def kernel(x, meta_tokens, rel_bias_table, ln_gain, ln_bias, gla_w_in, gla_w_gate2, gla_b_gate,
           gla_norm_gain, gla_w_out, kv_w_shared, swa_w_q, swa_sinks, swa_w_out,
           ffn_w_gate_up, ffn_w_down, moe_w_router, moe_w_gate_up, moe_w_down):
    bsz, seq, d = x.shape
    n_real = bsz * seq
    h = jnp.concatenate([x.reshape(n_real, d),
                         jnp.broadcast_to(meta_tokens.astype(x.dtype)[None], (bsz, N_META, d))
                         .reshape(bsz * N_META, d)], axis=0)
    rows = h.shape[0]
    hb = h.astype(BF16)
    bias_p, bias_c, meta01 = _swa_bias_static(rel_bias_table)

    w_in_b = gla_w_in.astype(BF16)
    w_gla_out_b = gla_w_out.astype(BF16)
    w_q_b = swa_w_q.astype(BF16)
    w_swa_out_b = swa_w_out.astype(BF16)
    w_ffn_gu_b = ffn_w_gate_up.astype(BF16)
    w_ffn_down_b = ffn_w_down.astype(BF16)
    hd = SWA_HEAD_DIM
    w_kv = kv_w_shared.reshape(d, 2 * SWA_KV_HEADS, 1, hd)
    w_kv2 = jnp.broadcast_to(w_kv, (d, 2 * SWA_KV_HEADS, 2, hd)).reshape(1, d, 4 * SWA_KV_HEADS * hd)
    w_kv2 = w_kv2.astype(BF16)

    kv2 = None
    for li in range(DEPTH):
        g0, b0 = ln_gain[li, 0][None, :], ln_bias[li, 0][None, :]
        g1, b1 = ln_gain[li, 1][None, :], ln_bias[li, 1][None, :]
        if li < N_A_LAYERS:
            qkvr = linear(hb, w_in_b, li, GLA_MAIN, F32)
            wg_pad = jnp.pad(w_in_b[li, :, GLA_MAIN:], ((0, 0), (0, LANES - GLA_GATE_RANK)))
            w2_pad = jnp.pad(gla_w_gate2[li].astype(BF16), ((0, LANES - GLA_GATE_RANK), (0, 0)))
            log_a = gla_gate(hb, wg_pad, w2_pad, gla_b_gate[li][None, :])
            mix_in = gla_mix(qkvr, log_a, gla_norm_gain[li][None, :], bsz, seq)
            h, hb = linear_res_ln(mix_in, w_gla_out_b, li, h, g0, b0)
        else:
            jb = li - N_A_LAYERS
            q = linear(hb, w_q_b, jb, d, BF16, scale=hd ** -0.5)
            bias_m = _swa_bias_meta(meta01, swa_sinks[jb])
            mix_in = swa_mix(q, kv2, bias_m, bias_p, bias_c, bsz, seq)
            h, hb = linear_res_ln(mix_in, w_swa_out_b, jb, h, g0, b0)
        if li % 2 == 0:
            h, hb = ffn_res_ln(hb, h, w_ffn_gu_b, w_ffn_down_b, li // 2, g1, b1)
        else:
            w_r = jnp.pad(moe_w_router[li // 2].astype(BF16), ((0, 0), (0, LANES - N_EXPERTS)))
            out_rows = n_real if li == DEPTH - 1 else rows
            h, hb = moe_res_ln(h, hb, w_r, moe_w_gate_up, moe_w_down, li // 2, g1, b1, out_rows)
        if li == N_A_LAYERS - 1:
            kv2 = linear(hb, w_kv2, 0, w_kv2.shape[-1], BF16)
    return h.reshape(bsz, seq, d)
```

```python
import functools
import math

import jax
import jax.numpy as jnp
from jax import lax
from jax.experimental import pallas as pl
from jax.experimental.pallas import tpu as pltpu

F32 = jnp.float32
BF16 = jnp.bfloat16

D_MODEL = 2048
DEPTH = 4
N_META = 16
N_A_LAYERS = DEPTH // 2
DN_ALPHA = (2 * DEPTH) ** 0.25
LN_EPS = 1e-5

GLA_HEADS = 4
GLA_DK = D_MODEL // 2
GLA_DV = D_MODEL
GLA_DK_HEAD = GLA_DK // GLA_HEADS
GLA_DV_HEAD = GLA_DV // GLA_HEADS
GLA_GATE_RANK = 16
GLA_GATE_TAU = 16.0
GLA_CHUNK = 64
GLA_MAIN = 2 * GLA_DK + 2 * GLA_DV

SWA_HEAD_DIM = 64
SWA_Q_HEADS = D_MODEL // SWA_HEAD_DIM
SWA_GROUP = 8
SWA_KV_HEADS = SWA_Q_HEADS // SWA_GROUP
SWA_PAIRS = SWA_GROUP // 2
SWA_PAIRS_PER_GROUP = 1
SWA_WINDOW = 128
SWA_BLOCK = 128

REL_BUCKETS = 32
REL_MAX_DIST = 128

FFN_DIM = 7 * D_MODEL // 2
N_EXPERTS = 8
TOP_K = 2
NEG_INF = -1e9

LANES = 128
VMEM_LIMIT = 56 * 1024 * 1024
ROW_TILE = 512
PROJ_ROW_TILE = 1376
PROJ_COL_TILE = 1024
FFN_TILE = 512
FFN_ROW_TILE = 688
MOE_ROW_TILE = 1024
MOE_GATHER_ROWS = 80
GLA_ROWS = 512
GLA_SUB_ROWS = 256


def _params(*sem):
    return pltpu.CompilerParams(dimension_semantics=sem, vmem_limit_bytes=VMEM_LIMIT)


def _layer_norm_rows(y, g, b):
    mu = jnp.mean(y, axis=-1, keepdims=True)
    yc = y - mu
    var = jnp.mean(yc * yc, axis=-1, keepdims=True)
    return yc * lax.rsqrt(var + LN_EPS) * g + b


def _dot(a, b):
    return jnp.dot(a, b, preferred_element_type=F32)


def _dot_nt(a, b):
    return lax.dot_general(a, b, (((1,), (1,)), ((), ())), preferred_element_type=F32)


def _dot_tn(a, b, precision=None):
    return lax.dot_general(a, b, (((0,), (0,)), ((), ())), preferred_element_type=F32,
                           precision=precision)


def _linear_kernel(x_ref, w_ref, o_ref, *, scale):
    acc = _dot(x_ref[...], w_ref[...])
    if scale != 1.0:
        acc = acc * scale
    o_ref[...] = acc.astype(o_ref.dtype)


def linear(x, w, layer, n_out, out_dtype, scale=1.0):
    rows, k = x.shape
    tm = PROJ_ROW_TILE if rows % PROJ_ROW_TILE == 0 else ROW_TILE
    tn = PROJ_COL_TILE
    return pl.pallas_call(
        functools.partial(_linear_kernel, scale=scale),
        grid=(pl.cdiv(rows, tm), n_out // tn),
        in_specs=[pl.BlockSpec((tm, k), lambda i, j: (i, 0)),
                  pl.BlockSpec((None, k, tn), lambda i, j: (layer, 0, j))],
        out_specs=pl.BlockSpec((tm, tn), lambda i, j: (i, j)),
        out_shape=jax.ShapeDtypeStruct((rows, n_out), out_dtype),
        compiler_params=_params("parallel", "arbitrary"),
        name="linear",
    )(x, w)


def _gla_gate_kernel(x_ref, wg_ref, w2_ref, b_ref, o_ref):
    g_low = _dot(x_ref[...], wg_ref[...])
    z = _dot(g_low.astype(BF16), w2_ref[...]) + b_ref[...]
    log_sig = jnp.minimum(z, 0.0) - jnp.log1p(jnp.exp(-jnp.abs(z)))
    o_ref[...] = log_sig / GLA_GATE_TAU


def gla_gate(x, wg_pad, w2_pad, b_gate):
    rows, k = x.shape
    return pl.pallas_call(
        _gla_gate_kernel,
        grid=(pl.cdiv(rows, ROW_TILE),),
        in_specs=[pl.BlockSpec((ROW_TILE, k), lambda i: (i, 0)),
                  pl.BlockSpec((k, LANES), lambda i: (0, 0)),
                  pl.BlockSpec((LANES, GLA_DK), lambda i: (0, 0)),
                  pl.BlockSpec((1, GLA_DK), lambda i: (0, 0))],
        out_specs=pl.BlockSpec((ROW_TILE, GLA_DK), lambda i: (i, 0)),
        out_shape=jax.ShapeDtypeStruct((rows, GLA_DK), F32),
        compiler_params=_params("parallel"),
        name="gla_gate",
    )(x, wg_pad, w2_pad, b_gate)


def _linear_res_ln_kernel(x_ref, w_ref, res_ref, g_ref, b_ref, o_ref, ob_ref):
    n_part = 4
    part = x_ref.shape[0] // n_part
    parts = [slice(k * part, (k + 1) * part) for k in range(n_part)]
    ys = [DN_ALPHA * res_ref[sl, :] + _dot(x_ref[sl, :], w_ref[...]) for sl in parts]
    for sl, y in zip(parts, ys):
        o = _layer_norm_rows(y, g_ref[...], b_ref[...])
        o_ref[sl, :] = o
        ob_ref[sl, :] = o.astype(BF16)


def linear_res_ln(x, w, layer, res, g, b):
    rows, k = x.shape
    d = w.shape[-1]
    return pl.pallas_call(
        _linear_res_ln_kernel,
        grid=(pl.cdiv(rows, ROW_TILE),),
        in_specs=[pl.BlockSpec((ROW_TILE, k), lambda i: (i, 0)),
                  pl.BlockSpec((None, k, d), lambda i: (layer, 0, 0)),
                  pl.BlockSpec((ROW_TILE, d), lambda i: (i, 0)),
                  pl.BlockSpec((1, d), lambda i: (0, 0)),
                  pl.BlockSpec((1, d), lambda i: (0, 0))],
        out_specs=[pl.BlockSpec((ROW_TILE, d), lambda i: (i, 0)),
                   pl.BlockSpec((ROW_TILE, d), lambda i: (i, 0))],
        out_shape=[jax.ShapeDtypeStruct((rows, d), F32), jax.ShapeDtypeStruct((rows, d), BF16)],
        compiler_params=_params("parallel"),
        name="linear_res_ln",
    )(x, w, res, g, b)


def _swiglu_partial(x, wa, wu, wd):
    a = _dot(x, wa)
    u = _dot(x, wu)
    hidden = (a * jax.nn.sigmoid(a) * u).astype(BF16)
    return _dot(hidden, wd)


def _ffn_kernel(x_ref, wa_ref, wu_ref, wd_ref, res_ref, g_ref, b_ref, o_ref, ob_ref):
    j = pl.program_id(1)

    @pl.when(j == 0)
    def _():
        o_ref[...] = jnp.zeros_like(o_ref)

    o_ref[...] += _swiglu_partial(x_ref[...], wa_ref[...], wu_ref[...], wd_ref[...])

    @pl.when(j == pl.num_programs(1) - 1)
    def _():
        o = _layer_norm_rows(DN_ALPHA * res_ref[...] + o_ref[...], g_ref[...], b_ref[...])
        o_ref[...] = o
        ob_ref[...] = o.astype(BF16)


def ffn_res_ln(xb, res, w_gu, w_down, layer, g, b):
    rows, d = xb.shape
    f = w_down.shape[-2]
    nj = f // FFN_TILE
    tm = FFN_ROW_TILE if rows % FFN_ROW_TILE == 0 else ROW_TILE
    return pl.pallas_call(
        _ffn_kernel,
        grid=(pl.cdiv(rows, tm), nj),
        in_specs=[pl.BlockSpec((tm, d), lambda i, j: (i, 0)),
                  pl.BlockSpec((None, d, FFN_TILE), lambda i, j: (layer, 0, j)),
                  pl.BlockSpec((None, d, FFN_TILE), lambda i, j: (layer, 0, nj + j)),
                  pl.BlockSpec((None, FFN_TILE, d), lambda i, j: (layer, j, 0)),
                  pl.BlockSpec((tm, d), lambda i, j: (i, 0)),
                  pl.BlockSpec((1, d), lambda i, j: (0, 0)),
                  pl.BlockSpec((1, d), lambda i, j: (0, 0))],
        out_specs=[pl.BlockSpec((tm, d), lambda i, j: (i, 0)),
                   pl.BlockSpec((tm, d), lambda i, j: (i, 0))],
        out_shape=[jax.ShapeDtypeStruct((rows, d), F32), jax.ShapeDtypeStruct((rows, d), BF16)],
        compiler_params=_params("parallel", "arbitrary"),
        name="ffn_res_ln",
    )(xb, w_gu, w_gu, w_down, res, g, b)


def _router_kernel(x_ref, w_ref, o_ref):
    logits = _dot(x_ref[...], w_ref[...])
    lane = lax.broadcasted_iota(jnp.int32, logits.shape, 1)
    lane_f = lane.astype(F32)
    logits = jnp.where(lane < N_EXPERTS, logits, -jnp.inf)
    m1 = jnp.max(logits, axis=-1, keepdims=True)
    i1 = jnp.min(jnp.where(logits == m1, lane_f, float(LANES)), axis=-1, keepdims=True)
    rest = jnp.where(lane_f == i1, -jnp.inf, logits)
    m2 = jnp.max(rest, axis=-1, keepdims=True)
    i2 = jnp.min(jnp.where(rest == m2, lane_f, float(LANES)), axis=-1, keepdims=True)
    e2 = jnp.exp(m2 - m1)
    g1 = 1.0 / (1.0 + e2)
    g2 = e2 / (1.0 + e2)
    o_ref[...] = jnp.where(lane == 0, i1, jnp.where(lane == 1, i2,
                           jnp.where(lane == 2, g1, jnp.where(lane == 3, g2, 0.0))))


def router(xb, w_router_pad):
    rows, d = xb.shape
    return pl.pallas_call(
        _router_kernel,
        grid=(pl.cdiv(rows, ROW_TILE),),
        in_specs=[pl.BlockSpec((ROW_TILE, d), lambda i: (i, 0)),
                  pl.BlockSpec((d, LANES), lambda i: (0, 0))],
        out_specs=pl.BlockSpec((ROW_TILE, LANES), lambda i: (i, 0)),
        out_shape=jax.ShapeDtypeStruct((rows, LANES), F32),
        compiler_params=_params("parallel"),
        name="router",
    )(xb, w_router_pad)


def _moe_kernel(blk_e_ref, n_half_ref, tok_ref, h_hbm, wa_ref, wu_ref, wd_ref, o_ref,
                land_ref, x_ref, sem):
    del blk_e_ref
    i = pl.program_id(0)
    j = pl.program_id(1)
    n_blk = pl.num_programs(0)
    nj = pl.num_programs(1)
    n_half = n_half_ref[i]
    half = MOE_ROW_TILE // 2
    land_rows = land_ref.shape[0]

    def row_copy(slot, dst_row):
        return pltpu.make_async_copy(h_hbm.at[pl.ds(tok_ref[slot], 1), :],
                                     land_ref.at[pl.ds(dst_row, 1), :], sem)

    def wait_landing():
        pltpu.make_async_copy(h_hbm.at[pl.ds(0, land_rows), :], land_ref, sem).wait()

    @pl.when(j == 0)
    def _():
        o_ref[...] = jnp.zeros_like(o_ref)

        @pl.when(i == 0)
        def _():
            def issue(r, carry):
                row_copy(r, r).start()
                return carry
            lax.fori_loop(0, land_rows, issue, 0, unroll=8)

        @pl.when(jnp.logical_or(i == 0, n_half_ref[jnp.maximum(i - 1, 0)] > 0))
        def _():
            wait_landing()

        for hf in range(2):
            @pl.when(n_half > hf)
            def _():
                sl = slice(hf * half, (hf + 1) * half)
                x_ref[sl, :] = land_ref[sl, :].astype(BF16)

    @pl.when(n_half > 0)
    def _():
        next_base = jnp.minimum(i + 1, n_blk - 1) * MOE_ROW_TILE
        r0 = j * MOE_GATHER_ROWS
        for k in range(MOE_GATHER_ROWS):
            row_copy(next_base + r0 + k, r0 + k).start()
        wa = wa_ref[...].astype(BF16)
        wu = wu_ref[...].astype(BF16)
        wd = wd_ref[...].astype(BF16)
        o_ref[:half, :] += _swiglu_partial(x_ref[:half, :], wa, wu, wd)

        @pl.when(n_half > 1)
        def _():
            o_ref[half:, :] += _swiglu_partial(x_ref[half:, :], wa, wu, wd)

    @pl.when(jnp.logical_and(jnp.logical_and(i == n_blk - 1, j == nj - 1), n_half > 0))
    def _():
        wait_landing()


def moe_experts(h, buf_tok, blk_e, n_half, w_gu, w_down, layer):
    d = h.shape[1]
    p_rows = buf_tok.shape[0]
    f = w_down.shape[-2]
    nj = f // FFN_TILE
    n_blk = p_rows // MOE_ROW_TILE
    assert nj * MOE_GATHER_ROWS >= MOE_ROW_TILE
    buf_tok = jnp.pad(buf_tok, (0, nj * MOE_GATHER_ROWS - MOE_ROW_TILE))

    def jj(i, j, nh):
        return jnp.where(nh[i] > 0, j, nj - 1)

    grid_spec = pltpu.PrefetchScalarGridSpec(
        num_scalar_prefetch=3,
        grid=(n_blk, nj),
        in_specs=[pl.BlockSpec(memory_space=pl.ANY),
                  pl.BlockSpec((None, None, d, FFN_TILE),
                               lambda i, j, be, nh, tk: (layer, be[i], 0, jj(i, j, nh))),
                  pl.BlockSpec((None, None, d, FFN_TILE),
                               lambda i, j, be, nh, tk: (layer, be[i], 0, nj + jj(i, j, nh))),
                  pl.BlockSpec((None, None, FFN_TILE, d),
                               lambda i, j, be, nh, tk: (layer, be[i], jj(i, j, nh), 0))],
        out_specs=pl.BlockSpec((MOE_ROW_TILE, d), lambda i, j, be, nh, tk: (i, 0),
                               pipeline_mode=pl.Buffered(1)),
        scratch_shapes=[pltpu.VMEM((nj * MOE_GATHER_ROWS, d), F32),
                        pltpu.VMEM((MOE_ROW_TILE, d), BF16),
                        pltpu.SemaphoreType.DMA(())],
    )
    return pl.pallas_call(
        _moe_kernel,
        grid_spec=grid_spec,
        out_shape=jax.ShapeDtypeStruct((p_rows, d), F32),
        compiler_params=_params("arbitrary", "arbitrary"),
        name="moe_experts",
    )(blk_e, n_half, buf_tok, h, w_gu, w_gu, w_down)


def _combine_ln_kernel(res_ref, y0_ref, y1_ref, gate_ref, g_ref, b_ref, o_ref, ob_ref):
    gate = gate_ref[...]
    moe = y0_ref[...] * gate[:, 2:3] + y1_ref[...] * gate[:, 3:4]
    o = _layer_norm_rows(DN_ALPHA * res_ref[...] + moe, g_ref[...], b_ref[...])
    o_ref[...] = o
    ob_ref[...] = o.astype(BF16)


def combine_ln(res, y0, y1, route, g, b, out_rows):
    d = res.shape[1]
    row_spec = pl.BlockSpec((ROW_TILE, d), lambda i: (i, 0))
    vec_spec = pl.BlockSpec((1, d), lambda i: (0, 0))
    return pl.pallas_call(
        _combine_ln_kernel,
        grid=(pl.cdiv(out_rows, ROW_TILE),),
        in_specs=[row_spec, row_spec, row_spec, pl.BlockSpec((ROW_TILE, LANES), lambda i: (i, 0)),
                  vec_spec, vec_spec],
        out_specs=[row_spec, row_spec],
        out_shape=[jax.ShapeDtypeStruct((out_rows, d), F32),
                   jax.ShapeDtypeStruct((out_rows, d), BF16)],
        compiler_params=_params("parallel"),
        name="combine_ln",
    )(res, y0, y1, route, g, b)


def moe_res_ln(h, hb, w_router_pad, w_gu, w_down, layer, g, b, out_rows):
    rows, d = h.shape
    route = router(hb, w_router_pad)
    flat_e = route[:, :TOP_K].astype(jnp.int32).reshape(-1)
    n_flat = rows * TOP_K
    n_blk = -(-n_flat // MOE_ROW_TILE) + N_EXPERTS
    p_rows = n_blk * MOE_ROW_TILE
    onehot = (flat_e[:, None] == jnp.arange(N_EXPERTS, dtype=jnp.int32)[None, :]).astype(jnp.int32)
    csum = jnp.cumsum(onehot, axis=0)
    rank = jnp.sum(csum * onehot, axis=1) - 1
    counts = csum[-1]
    padded = ((counts + MOE_ROW_TILE - 1) // MOE_ROW_TILE) * MOE_ROW_TILE
    padded_end = jnp.cumsum(padded)
    start_padded = padded_end - padded
    pos = start_padded[flat_e] + rank
    flat_tok = jnp.arange(n_flat, dtype=jnp.int32) // TOP_K
    buf_tok = jnp.zeros((p_rows,), jnp.int32).at[pos].set(flat_tok)
    blk_start = jnp.arange(n_blk, dtype=jnp.int32) * MOE_ROW_TILE
    blk_e = jnp.minimum(jnp.searchsorted(padded_end, blk_start, side='right'),
                        N_EXPERTS - 1).astype(jnp.int32)
    valid = jnp.clip(counts[blk_e] - (blk_start - start_padded[blk_e]), 0, MOE_ROW_TILE)
    valid = jnp.where(blk_start < padded_end[-1], valid, 0)
    half = MOE_ROW_TILE // 2
    n_half = ((valid + half - 1) // half).astype(jnp.int32)
    y = moe_experts(h, buf_tok, blk_e, n_half, w_gu, w_down, layer)
    pos2 = pos.reshape(rows, TOP_K)
    y0 = y.at[pos2[:, 0]].get(mode="promise_in_bounds")
    y1 = y.at[pos2[:, 1]].get(mode="promise_in_bounds")
    return combine_ln(h, y0, y1, route, g, b, out_rows)


def _head_norm_gate(o, r, gain):
    mu = jnp.mean(o, axis=-1, keepdims=True)
    oc = o - mu
    var = jnp.mean(oc * oc, axis=-1, keepdims=True)
    on = oc * lax.rsqrt(var + LN_EPS) * gain
    return (on * (r * jax.nn.sigmoid(r))).astype(BF16)


def _masked_cumsum(mask, x):
    m = mask.astype(BF16)
    hi = x.astype(BF16)
    r1 = x - hi.astype(F32)
    mid = r1.astype(BF16)
    lo = (r1 - mid.astype(F32)).astype(BF16)
    return _dot(m, hi) + _dot(m, mid) + _dot(m, lo)


def _gla_first_chunk(q, k, v, r, la, gain):
    c = q.shape[0]
    row = lax.broadcasted_iota(jnp.int32, (c, c), 0)
    col = lax.broadcasted_iota(jnp.int32, (c, c), 1)
    causal = row >= col
    bcum = _masked_cumsum(causal, la)
    q_dec = (q * (GLA_DK_HEAD ** -0.5) * jnp.exp(bcum)).astype(BF16)
    k_inv = (k * jnp.exp(-bcum)).astype(BF16)
    vb = v.astype(BF16)
    att = jnp.where(causal, _dot_nt(q_dec, k_inv), 0.0)
    o = _dot(att.astype(BF16), vb)
    k_state = (k * jnp.exp(bcum[c - 1:c, :] - bcum)).astype(BF16)
    return _head_norm_gate(o, r, gain), _dot_tn(k_state, vb)


def _gla_intra(q, k, v, la):
    n = q.shape[0]
    c = GLA_CHUNK
    row = lax.broadcasted_iota(jnp.int32, (n, n), 0)
    col = lax.broadcasted_iota(jnp.int32, (n, n), 1)
    causal = (row >= col) & ((row // c) == (col // c))
    bcum = _masked_cumsum(causal, la)
    q_dec = (q * (GLA_DK_HEAD ** -0.5) * jnp.exp(bcum)).astype(BF16)
    k_inv = (k * jnp.exp(-bcum)).astype(BF16)
    vb = v.astype(BF16)
    att = jnp.where(causal, _dot_nt(q_dec, k_inv), 0.0)
    o_intra = _dot(att.astype(BF16), vb)
    b_last = [bcum[(ci + 1) * c - 1:(ci + 1) * c, :] for ci in range(n // c)]
    b_tot = jnp.concatenate([jnp.broadcast_to(bl, (c, bl.shape[1])) for bl in b_last], axis=0)
    k_state = (k * jnp.exp(b_tot - bcum)).astype(BF16)
    return q_dec, k_state, vb, o_intra, b_last


def _gla_walk(state, q_dec, k_state, vb, b_last):
    c = GLA_CHUNK
    o_inter = []
    for ci, bl in enumerate(b_last):
        sl = slice(ci * c, (ci + 1) * c)
        o_inter.append(_dot(q_dec[sl], state.astype(BF16)))
        decay = jnp.exp(jnp.transpose(jnp.broadcast_to(bl, (LANES, bl.shape[1]))))
        decay = jnp.concatenate([decay] * (GLA_DV_HEAD // LANES), axis=1)
        state = state * decay + _dot_tn(k_state[sl], vb[sl])
    return jnp.concatenate(o_inter, axis=0), state


def _gla_kernel(qm_ref, km_ref, vm_ref, rm_ref, lam_ref, q_ref, k_ref, v_ref, r_ref, la_ref,
                gain_ref, o_ref, state_ref, *, blocks_per_seq):
    s = pl.program_id(1)
    gain = gain_ref[...]
    n_batch = state_ref.shape[0]

    @pl.when(s == 0)
    def _():
        for bi in range(n_batch):
            sl = slice(bi * N_META, (bi + 1) * N_META)
            og, st = _gla_first_chunk(qm_ref[sl, :], km_ref[sl, :], vm_ref[sl, :], rm_ref[sl, :],
                                      lam_ref[sl, :], gain)
            state_ref[bi] = st
            o_ref[sl, :] = og

    @pl.when(s > 0)
    def _():
        bi = (s - 1) // blocks_per_seq
        subs = [slice(r0, r0 + GLA_SUB_ROWS) for r0 in range(0, q_ref.shape[0], GLA_SUB_ROWS)]
        intra = [_gla_intra(q_ref[sl, :], k_ref[sl, :], v_ref[sl, :], la_ref[sl, :]) for sl in subs]
        st = state_ref[bi]
        for sl, (q_dec, k_state, vb, o_intra, b_last) in zip(subs, intra):
            o_inter, st = _gla_walk(st, q_dec, k_state, vb, b_last)
            o_ref[sl, :] = _head_norm_gate(o_intra + o_inter, r_ref[sl, :], gain)
        state_ref[bi] = st


def gla_mix(qkvr, log_a, norm_gain, n_batch, seq):
    rows = qkvr.shape[0]
    n_real = n_batch * seq
    meta_rows = n_batch * N_META
    blocks_per_seq = seq // GLA_ROWS
    n_steps = 1 + n_real // GLA_ROWS
    meta_blk = n_real // meta_rows
    meta_out_blk = n_real // GLA_ROWS
    dk, dv, nh = GLA_DK_HEAD, GLA_DV_HEAD, GLA_HEADS
    kq, kk, kv_, kr = 0, GLA_DK // dk, 2 * GLA_DK // dv, (2 * GLA_DK + GLA_DV) // dv

    def real_blk(s):
        return jnp.maximum(s - 1, 0)

    def meta_spec(width, col0):
        return pl.BlockSpec((meta_rows, width), lambda h, s: (meta_blk, col0 + h))

    def real_spec(width, col0):
        return pl.BlockSpec((GLA_ROWS, width), lambda h, s: (real_blk(s), col0 + h))

    return pl.pallas_call(
        functools.partial(_gla_kernel, blocks_per_seq=blocks_per_seq),
        grid=(nh, n_steps),
        in_specs=[meta_spec(dk, kq), meta_spec(dk, kk), meta_spec(dv, kv_), meta_spec(dv, kr),
                  meta_spec(dk, 0),
                  real_spec(dk, kq), real_spec(dk, kk), real_spec(dv, kv_), real_spec(dv, kr),
                  real_spec(dk, 0),
                  pl.BlockSpec((1, dv), lambda h, s: (0, h))],
        out_specs=pl.BlockSpec((GLA_ROWS, dv),
                               lambda h, s: (jnp.where(s == 0, meta_out_blk, s - 1), h)),
        out_shape=jax.ShapeDtypeStruct((rows, GLA_DV), BF16),
        scratch_shapes=[pltpu.VMEM((n_batch, dk, dv), F32)],
        compiler_params=_params("parallel", "arbitrary"),
        name="gla_mix",
    )(qkvr, qkvr, qkvr, qkvr, log_a, qkvr, qkvr, qkvr, qkvr, log_a, norm_gain)


def _block_diag2(x2):
    lane = lax.broadcasted_iota(jnp.int32, x2.shape, 1)
    zero = jnp.zeros_like(x2)
    return jnp.concatenate([jnp.where(lane < SWA_HEAD_DIM, x2, zero),
                            jnp.where(lane >= SWA_HEAD_DIM, x2, zero)], axis=0)


def _swa_kernel(q_ref, km_ref, kp_ref, kc_ref, vm_ref, vp_ref, vc_ref,
                bm_ref, bp_ref, bc_ref, o_ref, *, blocks_per_seq):
    s = pl.program_id(1)
    bi = jnp.maximum(s - 1, 0) // blocks_per_seq
    m0 = pl.multiple_of(bi * N_META, N_META)
    pad = jnp.zeros((SWA_BLOCK - N_META, LANES), BF16)
    k_m = jnp.concatenate([km_ref[pl.ds(m0, N_META), :], pad], axis=0)
    v_m = jnp.concatenate([vm_ref[pl.ds(m0, N_META), :], pad], axis=0)
    kbd = [_block_diag2(k_m), _block_diag2(kp_ref[...]), _block_diag2(kc_ref[...])]
    vbd = [_block_diag2(v_m), _block_diag2(vp_ref[...]), _block_diag2(vc_ref[...])]
    bias_refs = (bm_ref, bp_ref, bc_ref)
    groups = [range(g0, g0 + SWA_PAIRS_PER_GROUP) for g0 in range(0, SWA_PAIRS, SWA_PAIRS_PER_GROUP)]
    scores = []
    for pairs in groups:
        rows = slice(pairs[0] * SWA_BLOCK, (pairs[-1] + 1) * SWA_BLOCK)
        qp = jnp.concatenate([q_ref[:, p * LANES:(p + 1) * LANES] for p in pairs], axis=0)
        scores.append([_dot_nt(qp, kbd[t]) + bias_refs[t][0, 0, rows, :] for t in range(3)])
    for pairs, sc in zip(groups, scores):
        top = jnp.maximum(jnp.maximum(sc[0], sc[1]), sc[2])
        m_even = jnp.max(top[:, :LANES], axis=-1, keepdims=True)
        m_odd = jnp.max(top[:, LANES:], axis=-1, keepdims=True)
        lane2 = lax.broadcasted_iota(jnp.int32, top.shape, 1)
        m_full = jnp.where(lane2 < LANES, m_even, m_odd)
        pr = [jnp.exp(sc[t] - m_full) for t in range(3)]
        p_sum = pr[0] + pr[1] + pr[2]
        l_even = jnp.sum(p_sum[:, :LANES], axis=-1, keepdims=True)
        l_odd = jnp.sum(p_sum[:, LANES:], axis=-1, keepdims=True)
        o = (_dot(pr[0].astype(BF16), vbd[0]) + _dot(pr[1].astype(BF16), vbd[1])
             + _dot(pr[2].astype(BF16), vbd[2]))
        lane1 = lax.broadcasted_iota(jnp.int32, o.shape, 1)
        o = o / jnp.where(lane1 < SWA_HEAD_DIM, l_even, l_odd)
        for k, p in enumerate(pairs):
            o_ref[:, p * LANES:(p + 1) * LANES] = (
                o[k * SWA_BLOCK:(k + 1) * SWA_BLOCK].astype(o_ref.dtype))


def swa_mix(q, kv2, bias_m, bias_p, bias_c, n_batch, seq):
    rows = q.shape[0]
    blocks_per_seq = seq // SWA_BLOCK
    n_real_blk = n_batch * blocks_per_seq
    n_steps = 1 + n_real_blk
    meta_blk = n_real_blk
    assert n_batch * N_META == SWA_BLOCK
    kvn, qw = SWA_KV_HEADS, SWA_GROUP * SWA_HEAD_DIM

    def cur_blk(s):
        return jnp.where(s == 0, meta_blk, s - 1)

    def prev_blk(s):
        return jnp.maximum(s - 2, 0)

    def variant(s):
        return jnp.where(s == 0, 2, jnp.where((s - 1) % blocks_per_seq == 0, 0, 1))

    def kv_spec(head0, blk_fn):
        return pl.BlockSpec((SWA_BLOCK, LANES), lambda kh, s: (blk_fn(s), head0 + kh))

    bias_spec = pl.BlockSpec((1, 1, SWA_PAIRS * SWA_BLOCK, 2 * SWA_BLOCK),
                             lambda kh, s: (variant(s), kh, 0, 0))
    return pl.pallas_call(
        functools.partial(_swa_kernel, blocks_per_seq=blocks_per_seq),
        grid=(kvn, n_steps),
        in_specs=[pl.BlockSpec((SWA_BLOCK, qw), lambda kh, s: (cur_blk(s), kh)),
                  kv_spec(0, lambda s: meta_blk), kv_spec(0, prev_blk), kv_spec(0, cur_blk),
                  kv_spec(kvn, lambda s: meta_blk), kv_spec(kvn, prev_blk), kv_spec(kvn, cur_blk),
                  bias_spec, bias_spec, bias_spec],
        out_specs=pl.BlockSpec((SWA_BLOCK, qw), lambda kh, s: (cur_blk(s), kh)),
        out_shape=jax.ShapeDtypeStruct((rows, kvn * qw), BF16),
        compiler_params=_params("parallel", "arbitrary"),
        name="swa_mix",
    )(q, kv2, kv2, kv2, kv2, kv2, kv2, bias_m, bias_p, bias_c)


def _t5_bucket(dist):
    exact = REL_BUCKETS // 2
    d = jnp.maximum(dist, 0)
    df = jnp.maximum(d, 1).astype(F32)
    large = exact + (jnp.log(df / exact) / math.log(REL_MAX_DIST / exact)
                     * (REL_BUCKETS - exact)).astype(jnp.int32)
    large = jnp.minimum(large, REL_BUCKETS - 1)
    return jnp.where(d < exact, d, large)


def _pair_layout(bias):
    nv = bias.shape[0]
    b = bias.reshape(nv, SWA_KV_HEADS, SWA_PAIRS, 2, SWA_BLOCK, SWA_BLOCK)
    b = jnp.transpose(b, (0, 1, 2, 4, 3, 5))
    return b.reshape(nv, SWA_KV_HEADS, SWA_PAIRS * SWA_BLOCK, 2 * SWA_BLOCK)


def _swa_bias_static(table):
    tab = table.astype(F32)
    blk = SWA_BLOCK
    i = jnp.arange(blk, dtype=jnp.int32)[:, None]
    j = jnp.arange(blk, dtype=jnp.int32)[None, :]
    m = jnp.arange(N_META, dtype=jnp.int32)[None, :]

    def lookup(dist):
        onehot = (_t5_bucket(dist)[..., None] == jnp.arange(REL_BUCKETS, dtype=jnp.int32)).astype(F32)
        return jnp.einsum('qsb,bh->hqs', onehot, tab, precision=lax.Precision.HIGHEST)

    neg = jnp.full((SWA_Q_HEADS, blk, blk), NEG_INF, F32)
    cur = jnp.where((i - j >= 0)[None], lookup(i - j), NEG_INF)
    d_prev = blk + i - j
    prev = jnp.where(((d_prev >= 0) & (d_prev < SWA_WINDOW))[None], lookup(d_prev), NEG_INF)
    meta0 = lookup(N_META + i - m)
    meta1 = lookup(N_META + blk + i - m)
    same_seq = (i // N_META) == (j // N_META)
    dm = (i % N_META) - (j % N_META)
    meta_tile = jnp.where((same_seq & (dm >= 0))[None], lookup(dm), NEG_INF)
    bias_p = _pair_layout(jnp.stack([neg, prev, neg]))
    bias_c = _pair_layout(jnp.stack([cur, cur, meta_tile]))
    return bias_p, bias_c, jnp.stack([meta0, meta1])


def _swa_bias_meta(meta01, sinks):
    nh, blk = SWA_Q_HEADS, SWA_BLOCK
    sink_col = jnp.broadcast_to(sinks.astype(F32)[None, :, None, None], (3, nh, blk, 1))
    meta = jnp.concatenate([meta01, jnp.full((1, nh, blk, N_META), NEG_INF, F32)], axis=0)
    rest = jnp.full((3, nh, blk, blk - N_META - 1), NEG_INF, F32)
    return _pair_layout(jnp.concatenate([meta, sink_col, rest], axis=-1))


def kernel(x, meta_tokens, rel_bias_table, ln_gain, ln_bias, gla_w_in, gla_w_gate2, gla_b_gate,
           gla_norm_gain, gla_w_out, kv_w_shared, swa_w_q, swa_sinks, swa_w_out,
           ffn_w_gate_up, ffn_w_down, moe_w_router, moe_w_gate_up, moe_w_down):
    bsz, seq, d = x.shape
    n_real = bsz * seq
    h = jnp.concatenate([x.reshape(n_real, d),
                         jnp.broadcast_to(meta_tokens.astype(x.dtype)[None], (bsz, N_META, d))
                         .reshape(bsz * N_META, d)], axis=0)
    rows = h.shape[0]
    hb = h.astype(BF16)
    bias_p, bias_c, meta01 = _swa_bias_static(rel_bias_table)

    w_in_b = gla_w_in[:, :, :GLA_MAIN].astype(BF16)
    w_gla_out_b = gla_w_out.astype(BF16)
    w_q_b = swa_w_q.astype(BF16)
    w_swa_out_b = swa_w_out.astype(BF16)
    w_ffn_gu_b = ffn_w_gate_up.astype(BF16)
    w_ffn_down_b = ffn_w_down.astype(BF16)
    hd = SWA_HEAD_DIM
    w_kv = kv_w_shared.reshape(d, 2 * SWA_KV_HEADS, 1, hd)
    w_kv2 = jnp.broadcast_to(w_kv, (d, 2 * SWA_KV_HEADS, 2, hd)).reshape(1, d, 4 * SWA_KV_HEADS * hd)
    w_kv2 = w_kv2.astype(BF16)

    kv2 = None
    for li in range(DEPTH):
        g0, b0 = ln_gain[li, 0][None, :], ln_bias[li, 0][None, :]
        g1, b1 = ln_gain[li, 1][None, :], ln_bias[li, 1][None, :]
        if li < N_A_LAYERS:
            qkvr = linear(hb, w_in_b, li, GLA_MAIN, F32)
            wg_pad = jnp.pad(gla_w_in[li, :, GLA_MAIN:].astype(BF16),
                             ((0, 0), (0, LANES - GLA_GATE_RANK)))
            w2_pad = jnp.pad(gla_w_gate2[li].astype(BF16), ((0, LANES - GLA_GATE_RANK), (0, 0)))
            log_a = gla_gate(hb, wg_pad, w2_pad, gla_b_gate[li][None, :])
            mix_in = gla_mix(qkvr, log_a, gla_norm_gain[li][None, :], bsz, seq)
            h, hb = linear_res_ln(mix_in, w_gla_out_b, li, h, g0, b0)
        else:
            jb = li - N_A_LAYERS
            q = linear(hb, w_q_b, jb, d, BF16, scale=hd ** -0.5)
            bias_m = _swa_bias_meta(meta01, swa_sinks[jb])
            mix_in = swa_mix(q, kv2, bias_m, bias_p, bias_c, bsz, seq)
            h, hb = linear_res_ln(mix_in, w_swa_out_b, jb, h, g0, b0)
        if li % 2 == 0:
            h, hb = ffn_res_ln(hb, h, w_ffn_gu_b, w_ffn_down_b, li // 2, g1, b1)
        else:
            w_r = jnp.pad(moe_w_router[li // 2].astype(BF16), ((0, 0), (0, LANES - N_EXPERTS)))
            out_rows = n_real if li == DEPTH - 1 else rows
            h, hb = moe_res_ln(h, hb, w_r, moe_w_gate_up, moe_w_down, li // 2, g1, b1, out_rows)
        if li == N_A_LAYERS - 1:
            kv2 = linear(hb, w_kv2, 0, w_kv2.shape[-1], BF16)
    return h.reshape(bsz, seq, d)
```

```python
import functools
import math

import jax
import jax.numpy as jnp
from jax import lax
from jax.experimental import pallas as pl
from jax.experimental.pallas import tpu as pltpu

F32 = jnp.float32
BF16 = jnp.bfloat16

D_MODEL = 2048
DEPTH = 4
N_META = 16
N_A_LAYERS = DEPTH // 2
DN_ALPHA = (2 * DEPTH) ** 0.25
LN_EPS = 1e-5

GLA_HEADS = 4
GLA_DK = D_MODEL // 2
GLA_DV = D_MODEL
GLA_DK_HEAD = GLA_DK // GLA_HEADS
GLA_DV_HEAD = GLA_DV // GLA_HEADS
GLA_GATE_RANK = 16
GLA_GATE_TAU = 16.0
GLA_CHUNK = 64
GLA_MAIN = 2 * GLA_DK + 2 * GLA_DV

SWA_HEAD_DIM = 64
SWA_Q_HEADS = D_MODEL // SWA_HEAD_DIM
SWA_GROUP = 8
SWA_KV_HEADS = SWA_Q_HEADS // SWA_GROUP
SWA_PAIRS = SWA_GROUP // 2
SWA_PAIRS_PER_GROUP = 1
SWA_WINDOW = 128
SWA_BLOCK = 128

REL_BUCKETS = 32
REL_MAX_DIST = 128

FFN_DIM = 7 * D_MODEL // 2
N_EXPERTS = 8
TOP_K = 2
NEG_INF = -1e9

LANES = 128
VMEM_LIMIT = 56 * 1024 * 1024
ROW_TILE = 512
PROJ_ROW_TILE = 1376
PROJ_COL_TILE = 1024
FFN_TILE = 512
FFN_ROW_TILE = 688
MOE_ROW_TILE = 1024
MOE_GATHER_ROWS = 76
MOE_VMEM_LIMIT = 62 * 1024 * 1024
GLA_ROWS = 1024
GLA_SUB_ROWS = 256


def _params(*sem):
    return pltpu.CompilerParams(dimension_semantics=sem, vmem_limit_bytes=VMEM_LIMIT)


def _layer_norm_rows(y, g, b):
    mu = jnp.mean(y, axis=-1, keepdims=True)
    yc = y - mu
    var = jnp.mean(yc * yc, axis=-1, keepdims=True)
    return yc * lax.rsqrt(var + LN_EPS) * g + b


def _dot(a, b):
    return jnp.dot(a, b, preferred_element_type=F32)


def _dot_nt(a, b):
    return lax.dot_general(a, b, (((1,), (1,)), ((), ())), preferred_element_type=F32)


def _dot_tn(a, b, precision=None):
    return lax.dot_general(a, b, (((0,), (0,)), ((), ())), preferred_element_type=F32,
                           precision=precision)


def _linear_kernel(x_ref, w_ref, o_ref, *, scale):
    acc = _dot(x_ref[...], w_ref[...])
    if scale != 1.0:
        acc = acc * scale
    o_ref[...] = acc.astype(o_ref.dtype)


def linear(x, w, layer, n_out, out_dtype, scale=1.0):
    rows, k = x.shape
    tm = PROJ_ROW_TILE if rows % PROJ_ROW_TILE == 0 else ROW_TILE
    tn = PROJ_COL_TILE
    return pl.pallas_call(
        functools.partial(_linear_kernel, scale=scale),
        grid=(pl.cdiv(rows, tm), n_out // tn),
        in_specs=[pl.BlockSpec((tm, k), lambda i, j: (i, 0)),
                  pl.BlockSpec((None, k, tn), lambda i, j: (layer, 0, j))],
        out_specs=pl.BlockSpec((tm, tn), lambda i, j: (i, j)),
        out_shape=jax.ShapeDtypeStruct((rows, n_out), out_dtype),
        compiler_params=_params("parallel", "arbitrary"),
        name="linear",
    )(x, w)


def _gla_gate_kernel(x_ref, wg_ref, w2_ref, b_ref, o_ref):
    g_low = _dot(x_ref[...], wg_ref[...])
    z = _dot(g_low.astype(BF16), w2_ref[...]) + b_ref[...]
    log_sig = jnp.minimum(z, 0.0) - jnp.log1p(jnp.exp(-jnp.abs(z)))
    o_ref[...] = log_sig / GLA_GATE_TAU


def gla_gate(x, wg_pad, w2_pad, b_gate):
    rows, k = x.shape
    return pl.pallas_call(
        _gla_gate_kernel,
        grid=(pl.cdiv(rows, ROW_TILE),),
        in_specs=[pl.BlockSpec((ROW_TILE, k), lambda i: (i, 0)),
                  pl.BlockSpec((k, LANES), lambda i: (0, 0)),
                  pl.BlockSpec((LANES, GLA_DK), lambda i: (0, 0)),
                  pl.BlockSpec((1, GLA_DK), lambda i: (0, 0))],
        out_specs=pl.BlockSpec((ROW_TILE, GLA_DK), lambda i: (i, 0)),
        out_shape=jax.ShapeDtypeStruct((rows, GLA_DK), F32),
        compiler_params=_params("parallel"),
        name="gla_gate",
    )(x, wg_pad, w2_pad, b_gate)


def _linear_res_ln_kernel(x_ref, w_ref, res_ref, g_ref, b_ref, o_ref, ob_ref):
    n_part = 4
    part = x_ref.shape[0] // n_part
    parts = [slice(k * part, (k + 1) * part) for k in range(n_part)]
    ys = [DN_ALPHA * res_ref[sl, :] + _dot(x_ref[sl, :], w_ref[...]) for sl in parts]
    for sl, y in zip(parts, ys):
        o = _layer_norm_rows(y, g_ref[...], b_ref[...])
        o_ref[sl, :] = o
        ob_ref[sl, :] = o.astype(BF16)


def linear_res_ln(x, w, layer, res, g, b):
    rows, k = x.shape
    d = w.shape[-1]
    return pl.pallas_call(
        _linear_res_ln_kernel,
        grid=(pl.cdiv(rows, ROW_TILE),),
        in_specs=[pl.BlockSpec((ROW_TILE, k), lambda i: (i, 0)),
                  pl.BlockSpec((None, k, d), lambda i: (layer, 0, 0)),
                  pl.BlockSpec((ROW_TILE, d), lambda i: (i, 0)),
                  pl.BlockSpec((1, d), lambda i: (0, 0)),
                  pl.BlockSpec((1, d), lambda i: (0, 0))],
        out_specs=[pl.BlockSpec((ROW_TILE, d), lambda i: (i, 0)),
                   pl.BlockSpec((ROW_TILE, d), lambda i: (i, 0))],
        out_shape=[jax.ShapeDtypeStruct((rows, d), F32), jax.ShapeDtypeStruct((rows, d), BF16)],
        compiler_params=_params("parallel"),
        name="linear_res_ln",
    )(x, w, res, g, b)


def _swiglu_partial(x, wa, wu, wd):
    a = _dot(x, wa)
    u = _dot(x, wu)
    hidden = (a * jax.nn.sigmoid(a) * u).astype(BF16)
    return _dot(hidden, wd)


def _ffn_kernel(x_ref, wa_ref, wu_ref, wd_ref, res_ref, g_ref, b_ref, o_ref, ob_ref):
    j = pl.program_id(1)

    @pl.when(j == 0)
    def _():
        o_ref[...] = jnp.zeros_like(o_ref)

    o_ref[...] += _swiglu_partial(x_ref[...], wa_ref[...], wu_ref[...], wd_ref[...])

    @pl.when(j == pl.num_programs(1) - 1)
    def _():
        o = _layer_norm_rows(DN_ALPHA * res_ref[...] + o_ref[...], g_ref[...], b_ref[...])
        o_ref[...] = o
        ob_ref[...] = o.astype(BF16)


def ffn_res_ln(xb, res, w_gu, w_down, layer, g, b):
    rows, d = xb.shape
    f = w_down.shape[-2]
    nj = f // FFN_TILE
    tm = FFN_ROW_TILE if rows % FFN_ROW_TILE == 0 else ROW_TILE
    return pl.pallas_call(
        _ffn_kernel,
        grid=(pl.cdiv(rows, tm), nj),
        in_specs=[pl.BlockSpec((tm, d), lambda i, j: (i, 0)),
                  pl.BlockSpec((None, d, FFN_TILE), lambda i, j: (layer, 0, j)),
                  pl.BlockSpec((None, d, FFN_TILE), lambda i, j: (layer, 0, nj + j)),
                  pl.BlockSpec((None, FFN_TILE, d), lambda i, j: (layer, j, 0)),
                  pl.BlockSpec((tm, d), lambda i, j: (i, 0)),
                  pl.BlockSpec((1, d), lambda i, j: (0, 0)),
                  pl.BlockSpec((1, d), lambda i, j: (0, 0))],
        out_specs=[pl.BlockSpec((tm, d), lambda i, j: (i, 0)),
                   pl.BlockSpec((tm, d), lambda i, j: (i, 0))],
        out_shape=[jax.ShapeDtypeStruct((rows, d), F32), jax.ShapeDtypeStruct((rows, d), BF16)],
        compiler_params=_params("parallel", "arbitrary"),
        name="ffn_res_ln",
    )(xb, w_gu, w_gu, w_down, res, g, b)


def _router_kernel(x_ref, w_ref, o_ref):
    logits = _dot(x_ref[...], w_ref[...])
    lane = lax.broadcasted_iota(jnp.int32, logits.shape, 1)
    lane_f = lane.astype(F32)
    logits = jnp.where(lane < N_EXPERTS, logits, -jnp.inf)
    m1 = jnp.max(logits, axis=-1, keepdims=True)
    i1 = jnp.min(jnp.where(logits == m1, lane_f, float(LANES)), axis=-1, keepdims=True)
    rest = jnp.where(lane_f == i1, -jnp.inf, logits)
    m2 = jnp.max(rest, axis=-1, keepdims=True)
    i2 = jnp.min(jnp.where(rest == m2, lane_f, float(LANES)), axis=-1, keepdims=True)
    e2 = jnp.exp(m2 - m1)
    g1 = 1.0 / (1.0 + e2)
    g2 = e2 / (1.0 + e2)
    o_ref[...] = jnp.where(lane == 0, i1, jnp.where(lane == 1, i2,
                           jnp.where(lane == 2, g1, jnp.where(lane == 3, g2, 0.0))))


def router(xb, w_router_pad):
    rows, d = xb.shape
    return pl.pallas_call(
        _router_kernel,
        grid=(pl.cdiv(rows, ROW_TILE),),
        in_specs=[pl.BlockSpec((ROW_TILE, d), lambda i: (i, 0)),
                  pl.BlockSpec((d, LANES), lambda i: (0, 0))],
        out_specs=pl.BlockSpec((ROW_TILE, LANES), lambda i: (i, 0)),
        out_shape=jax.ShapeDtypeStruct((rows, LANES), F32),
        compiler_params=_params("parallel"),
        name="router",
    )(xb, w_router_pad)


def _moe_kernel(blk_e_ref, n_half_ref, tok_ref, h_hbm, wa_ref, wu_ref, wd_ref, o_ref,
                land_ref, x_ref, sem):
    del blk_e_ref
    i = pl.program_id(0)
    j = pl.program_id(1)
    n_blk = pl.num_programs(0)
    nj = pl.num_programs(1)
    n_half = n_half_ref[i]
    half = MOE_ROW_TILE // 2
    land_rows = land_ref.shape[0]

    def row_copy(slot, dst_row):
        return pltpu.make_async_copy(h_hbm.at[pl.ds(tok_ref[slot], 1), :],
                                     land_ref.at[pl.ds(dst_row, 1), :], sem)

    def wait_landing():
        pltpu.make_async_copy(h_hbm.at[pl.ds(0, land_rows), :], land_ref, sem).wait()

    @pl.when(j == 0)
    def _():
        o_ref[...] = jnp.zeros_like(o_ref)

        @pl.when(i == 0)
        def _():
            def issue(r, carry):
                row_copy(r, r).start()
                return carry
            lax.fori_loop(0, land_rows, issue, 0, unroll=8)

        @pl.when(jnp.logical_or(i == 0, n_half_ref[jnp.maximum(i - 1, 0)] > 0))
        def _():
            wait_landing()

        for hf in range(2):
            @pl.when(n_half > hf)
            def _():
                sl = slice(hf * half, (hf + 1) * half)
                x_ref[sl, :] = land_ref[sl, :].astype(BF16)

    @pl.when(n_half > 0)
    def _():
        next_base = jnp.minimum(i + 1, n_blk - 1) * MOE_ROW_TILE
        r0 = j * MOE_GATHER_ROWS
        for k in range(MOE_GATHER_ROWS):
            row_copy(next_base + r0 + k, r0 + k).start()
        wa = wa_ref[...].astype(BF16)
        wu = wu_ref[...].astype(BF16)
        wd = wd_ref[...].astype(BF16)
        o_ref[:half, :] += _swiglu_partial(x_ref[:half, :], wa, wu, wd)

        @pl.when(n_half > 1)
        def _():
            o_ref[half:, :] += _swiglu_partial(x_ref[half:, :], wa, wu, wd)

    @pl.when(jnp.logical_and(jnp.logical_and(i == n_blk - 1, j == nj - 1), n_half > 0))
    def _():
        wait_landing()


def moe_experts(h, buf_tok, blk_e, n_half, w_gu, w_down, layer):
    d = h.shape[1]
    p_rows = buf_tok.shape[0]
    f = w_down.shape[-2]
    nj = f // FFN_TILE
    n_blk = p_rows // MOE_ROW_TILE
    assert nj * MOE_GATHER_ROWS >= MOE_ROW_TILE
    buf_tok = jnp.pad(buf_tok, (0, nj * MOE_GATHER_ROWS - MOE_ROW_TILE))

    def jj(i, j, nh):
        return jnp.where(nh[i] > 0, j, nj - 1)

    grid_spec = pltpu.PrefetchScalarGridSpec(
        num_scalar_prefetch=3,
        grid=(n_blk, nj),
        in_specs=[pl.BlockSpec(memory_space=pl.ANY),
                  pl.BlockSpec((None, None, d, FFN_TILE),
                               lambda i, j, be, nh, tk: (layer, be[i], 0, jj(i, j, nh))),
                  pl.BlockSpec((None, None, d, FFN_TILE),
                               lambda i, j, be, nh, tk: (layer, be[i], 0, nj + jj(i, j, nh))),
                  pl.BlockSpec((None, None, FFN_TILE, d),
                               lambda i, j, be, nh, tk: (layer, be[i], jj(i, j, nh), 0))],
        out_specs=pl.BlockSpec((MOE_ROW_TILE, d), lambda i, j, be, nh, tk: (i, 0)),
        scratch_shapes=[pltpu.VMEM((nj * MOE_GATHER_ROWS, d), F32),
                        pltpu.VMEM((MOE_ROW_TILE, d), BF16),
                        pltpu.SemaphoreType.DMA(())],
    )
    return pl.pallas_call(
        _moe_kernel,
        grid_spec=grid_spec,
        out_shape=jax.ShapeDtypeStruct((p_rows, d), F32),
        compiler_params=pltpu.CompilerParams(dimension_semantics=("arbitrary", "arbitrary"),
                                             vmem_limit_bytes=MOE_VMEM_LIMIT),
        name="moe_experts",
    )(blk_e, n_half, buf_tok, h, w_gu, w_gu, w_down)


def _combine_ln_kernel(res_ref, y0_ref, y1_ref, gate_ref, g_ref, b_ref, o_ref, ob_ref):
    gate = gate_ref[...]
    moe = y0_ref[...] * gate[:, 2:3] + y1_ref[...] * gate[:, 3:4]
    o = _layer_norm_rows(DN_ALPHA * res_ref[...] + moe, g_ref[...], b_ref[...])
    o_ref[...] = o
    ob_ref[...] = o.astype(BF16)


def combine_ln(res, y0, y1, route, g, b, out_rows):
    d = res.shape[1]
    row_spec = pl.BlockSpec((ROW_TILE, d), lambda i: (i, 0))
    vec_spec = pl.BlockSpec((1, d), lambda i: (0, 0))
    return pl.pallas_call(
        _combine_ln_kernel,
        grid=(pl.cdiv(out_rows, ROW_TILE),),
        in_specs=[row_spec, row_spec, row_spec, pl.BlockSpec((ROW_TILE, LANES), lambda i: (i, 0)),
                  vec_spec, vec_spec],
        out_specs=[row_spec, row_spec],
        out_shape=[jax.ShapeDtypeStruct((out_rows, d), F32),
                   jax.ShapeDtypeStruct((out_rows, d), BF16)],
        compiler_params=_params("parallel"),
        name="combine_ln",
    )(res, y0, y1, route, g, b)


def moe_res_ln(h, hb, w_router_pad, w_gu, w_down, layer, g, b, out_rows):
    rows, d = h.shape
    route = router(hb, w_router_pad)
    flat_e = route[:, :TOP_K].astype(jnp.int32).reshape(-1)
    n_flat = rows * TOP_K
    n_blk = -(-n_flat // MOE_ROW_TILE) + N_EXPERTS
    p_rows = n_blk * MOE_ROW_TILE
    onehot = (flat_e[:, None] == jnp.arange(N_EXPERTS, dtype=jnp.int32)[None, :]).astype(jnp.int32)
    csum = jnp.cumsum(onehot, axis=0)
    rank = jnp.sum(csum * onehot, axis=1) - 1
    counts = csum[-1]
    padded = ((counts + MOE_ROW_TILE - 1) // MOE_ROW_TILE) * MOE_ROW_TILE
    padded_end = jnp.cumsum(padded)
    start_padded = padded_end - padded
    pos = start_padded[flat_e] + rank
    flat_tok = jnp.arange(n_flat, dtype=jnp.int32) // TOP_K
    buf_tok = jnp.zeros((p_rows,), jnp.int32).at[pos].set(flat_tok)
    blk_start = jnp.arange(n_blk, dtype=jnp.int32) * MOE_ROW_TILE
    blk_e = jnp.minimum(jnp.searchsorted(padded_end, blk_start, side='right'),
                        N_EXPERTS - 1).astype(jnp.int32)
    valid = jnp.clip(counts[blk_e] - (blk_start - start_padded[blk_e]), 0, MOE_ROW_TILE)
    valid = jnp.where(blk_start < padded_end[-1], valid, 0)
    half = MOE_ROW_TILE // 2
    n_half = ((valid + half - 1) // half).astype(jnp.int32)
    y = moe_experts(h, buf_tok, blk_e, n_half, w_gu, w_down, layer)
    pos2 = pos.reshape(rows, TOP_K)
    y0 = y.at[pos2[:, 0]].get(mode="promise_in_bounds")
    y1 = y.at[pos2[:, 1]].get(mode="promise_in_bounds")
    return combine_ln(h, y0, y1, route, g, b, out_rows)


def _head_norm_gate(o, r, gain):
    mu = jnp.mean(o, axis=-1, keepdims=True)
    oc = o - mu
    var = jnp.mean(oc * oc, axis=-1, keepdims=True)
    on = oc * lax.rsqrt(var + LN_EPS) * gain
    return (on * (r * jax.nn.sigmoid(r))).astype(BF16)


def _masked_cumsum(mask, x):
    m = mask.astype(BF16)
    hi = x.astype(BF16)
    r1 = x - hi.astype(F32)
    mid = r1.astype(BF16)
    lo = (r1 - mid.astype(F32)).astype(BF16)
    return _dot(m, hi) + _dot(m, mid) + _dot(m, lo)


def _gla_first_chunk(q, k, v, r, la, gain):
    c = q.shape[0]
    row = lax.broadcasted_iota(jnp.int32, (c, c), 0)
    col = lax.broadcasted_iota(jnp.int32, (c, c), 1)
    causal = row >= col
    bcum = _masked_cumsum(causal, la)
    q_dec = (q * (GLA_DK_HEAD ** -0.5) * jnp.exp(bcum)).astype(BF16)
    k_inv = (k * jnp.exp(-bcum)).astype(BF16)
    vb = v.astype(BF16)
    att = jnp.where(causal, _dot_nt(q_dec, k_inv), 0.0)
    o = _dot(att.astype(BF16), vb)
    k_state = (k * jnp.exp(bcum[c - 1:c, :] - bcum)).astype(BF16)
    return _head_norm_gate(o, r, gain), _dot_tn(k_state, vb)


def _gla_intra(q, k, v, la):
    n = q.shape[0]
    c = GLA_CHUNK
    row = lax.broadcasted_iota(jnp.int32, (n, n), 0)
    col = lax.broadcasted_iota(jnp.int32, (n, n), 1)
    causal = (row >= col) & ((row // c) == (col // c))
    bcum = _masked_cumsum(causal, la)
    q_dec = (q * (GLA_DK_HEAD ** -0.5) * jnp.exp(bcum)).astype(BF16)
    k_inv = (k * jnp.exp(-bcum)).astype(BF16)
    vb = v.astype(BF16)
    att = jnp.where(causal, _dot_nt(q_dec, k_inv), 0.0)
    o_intra = _dot(att.astype(BF16), vb)
    b_last = [bcum[(ci + 1) * c - 1:(ci + 1) * c, :] for ci in range(n // c)]
    b_tot = jnp.concatenate([jnp.broadcast_to(bl, (c, bl.shape[1])) for bl in b_last], axis=0)
    k_state = (k * jnp.exp(b_tot - bcum)).astype(BF16)
    return q_dec, k_state, vb, o_intra, b_last


def _gla_walk(state, q_dec, k_state, vb, b_last):
    c = GLA_CHUNK
    o_inter = []
    for ci, bl in enumerate(b_last):
        sl = slice(ci * c, (ci + 1) * c)
        o_inter.append(_dot(q_dec[sl], state.astype(BF16)))
        decay = jnp.exp(jnp.transpose(jnp.broadcast_to(bl, (LANES, bl.shape[1]))))
        decay = jnp.concatenate([decay] * (GLA_DV_HEAD // LANES), axis=1)
        state = state * decay + _dot_tn(k_state[sl], vb[sl])
    return jnp.concatenate(o_inter, axis=0), state


def _gla_kernel(qm_ref, km_ref, vm_ref, rm_ref, lam_ref, q_ref, k_ref, v_ref, r_ref, la_ref,
                gain_ref, o_ref, state_ref, *, blocks_per_seq):
    s = pl.program_id(1)
    gain = gain_ref[...]
    n_batch = state_ref.shape[0]

    @pl.when(s == 0)
    def _():
        for bi in range(n_batch):
            sl = slice(bi * N_META, (bi + 1) * N_META)
            og, st = _gla_first_chunk(qm_ref[sl, :], km_ref[sl, :], vm_ref[sl, :], rm_ref[sl, :],
                                      lam_ref[sl, :], gain)
            state_ref[bi] = st
            o_ref[sl, :] = og

    @pl.when(s > 0)
    def _():
        bi = (s - 1) // blocks_per_seq
        subs = [slice(r0, r0 + GLA_SUB_ROWS) for r0 in range(0, q_ref.shape[0], GLA_SUB_ROWS)]
        intra = [_gla_intra(q_ref[sl, :], k_ref[sl, :], v_ref[sl, :], la_ref[sl, :]) for sl in subs]
        st = state_ref[bi]
        for sl, (q_dec, k_state, vb, o_intra, b_last) in zip(subs, intra):
            o_inter, st = _gla_walk(st, q_dec, k_state, vb, b_last)
            o_ref[sl, :] = _head_norm_gate(o_intra + o_inter, r_ref[sl, :], gain)
        state_ref[bi] = st


def gla_mix(qkvr, log_a, norm_gain, n_batch, seq):
    rows = qkvr.shape[0]
    n_real = n_batch * seq
    meta_rows = n_batch * N_META
    blocks_per_seq = seq // GLA_ROWS
    n_steps = 1 + n_real // GLA_ROWS
    meta_blk = n_real // meta_rows
    meta_out_blk = n_real // GLA_ROWS
    dk, dv, nh = GLA_DK_HEAD, GLA_DV_HEAD, GLA_HEADS
    kq, kk, kv_, kr = 0, GLA_DK // dk, 2 * GLA_DK // dv, (2 * GLA_DK + GLA_DV) // dv

    def real_blk(s):
        return jnp.maximum(s - 1, 0)

    def meta_spec(width, col0):
        return pl.BlockSpec((meta_rows, width), lambda h, s: (meta_blk, col0 + h))

    def real_spec(width, col0):
        return pl.BlockSpec((GLA_ROWS, width), lambda h, s: (real_blk(s), col0 + h))

    return pl.pallas_call(
        functools.partial(_gla_kernel, blocks_per_seq=blocks_per_seq),
        grid=(nh, n_steps),
        in_specs=[meta_spec(dk, kq), meta_spec(dk, kk), meta_spec(dv, kv_), meta_spec(dv, kr),
                  meta_spec(dk, 0),
                  real_spec(dk, kq), real_spec(dk, kk), real_spec(dv, kv_), real_spec(dv, kr),
                  real_spec(dk, 0),
                  pl.BlockSpec((1, dv), lambda h, s: (0, h))],
        out_specs=pl.BlockSpec((GLA_ROWS, dv),
                               lambda h, s: (jnp.where(s == 0, meta_out_blk, s - 1), h)),
        out_shape=jax.ShapeDtypeStruct((rows, GLA_DV), BF16),
        scratch_shapes=[pltpu.VMEM((n_batch, dk, dv), F32)],
        compiler_params=_params("parallel", "arbitrary"),
        name="gla_mix",
    )(qkvr, qkvr, qkvr, qkvr, log_a, qkvr, qkvr, qkvr, qkvr, log_a, norm_gain)


def _block_diag2(x2):
    lane = lax.broadcasted_iota(jnp.int32, x2.shape, 1)
    zero = jnp.zeros_like(x2)
    return jnp.concatenate([jnp.where(lane < SWA_HEAD_DIM, x2, zero),
                            jnp.where(lane >= SWA_HEAD_DIM, x2, zero)], axis=0)


def _swa_kernel(q_ref, km_ref, kp_ref, kc_ref, vm_ref, vp_ref, vc_ref,
                bm_ref, bp_ref, bc_ref, o_ref, *, blocks_per_seq):
    s = pl.program_id(1)
    bi = jnp.maximum(s - 1, 0) // blocks_per_seq
    m0 = pl.multiple_of(bi * N_META, N_META)
    pad = jnp.zeros((SWA_BLOCK - N_META, LANES), BF16)
    k_m = jnp.concatenate([km_ref[pl.ds(m0, N_META), :], pad], axis=0)
    v_m = jnp.concatenate([vm_ref[pl.ds(m0, N_META), :], pad], axis=0)
    kbd = [_block_diag2(k_m), _block_diag2(kp_ref[...]), _block_diag2(kc_ref[...])]
    vbd = [_block_diag2(v_m), _block_diag2(vp_ref[...]), _block_diag2(vc_ref[...])]
    bias_refs = (bm_ref, bp_ref, bc_ref)
    groups = [range(g0, g0 + SWA_PAIRS_PER_GROUP) for g0 in range(0, SWA_PAIRS, SWA_PAIRS_PER_GROUP)]
    scores = []
    for pairs in groups:
        rows = slice(pairs[0] * SWA_BLOCK, (pairs[-1] + 1) * SWA_BLOCK)
        qp = jnp.concatenate([q_ref[:, p * LANES:(p + 1) * LANES] for p in pairs], axis=0)
        scores.append([_dot_nt(qp, kbd[t]) + bias_refs[t][0, 0, rows, :] for t in range(3)])
    for pairs, sc in zip(groups, scores):
        top = jnp.maximum(jnp.maximum(sc[0], sc[1]), sc[2])
        m_even = jnp.max(top[:, :LANES], axis=-1, keepdims=True)
        m_odd = jnp.max(top[:, LANES:], axis=-1, keepdims=True)
        lane2 = lax.broadcasted_iota(jnp.int32, top.shape, 1)
        m_full = jnp.where(lane2 < LANES, m_even, m_odd)
        pr = [jnp.exp(sc[t] - m_full) for t in range(3)]
        p_sum = pr[0] + pr[1] + pr[2]
        l_even = jnp.sum(p_sum[:, :LANES], axis=-1, keepdims=True)
        l_odd = jnp.sum(p_sum[:, LANES:], axis=-1, keepdims=True)
        o = (_dot(pr[0].astype(BF16), vbd[0]) + _dot(pr[1].astype(BF16), vbd[1])
             + _dot(pr[2].astype(BF16), vbd[2]))
        lane1 = lax.broadcasted_iota(jnp.int32, o.shape, 1)
        o = o / jnp.where(lane1 < SWA_HEAD_DIM, l_even, l_odd)
        for k, p in enumerate(pairs):
            o_ref[:, p * LANES:(p + 1) * LANES] = (
                o[k * SWA_BLOCK:(k + 1) * SWA_BLOCK].astype(o_ref.dtype))


def swa_mix(q, kv2, bias_m, bias_p, bias_c, n_batch, seq):
    rows = q.shape[0]
    blocks_per_seq = seq // SWA_BLOCK
    n_real_blk = n_batch * blocks_per_seq
    n_steps = 1 + n_real_blk
    meta_blk = n_real_blk
    assert n_batch * N_META == SWA_BLOCK
    kvn, qw = SWA_KV_HEADS, SWA_GROUP * SWA_HEAD_DIM

    def cur_blk(s):
        return jnp.where(s == 0, meta_blk, s - 1)

    def prev_blk(s):
        return jnp.maximum(s - 2, 0)

    def variant(s):
        return jnp.where(s == 0, 2, jnp.where((s - 1) % blocks_per_seq == 0, 0, 1))

    def kv_spec(head0, blk_fn):
        return pl.BlockSpec((SWA_BLOCK, LANES), lambda kh, s: (blk_fn(s), head0 + kh))

    bias_spec = pl.BlockSpec((1, 1, SWA_PAIRS * SWA_BLOCK, 2 * SWA_BLOCK),
                             lambda kh, s: (variant(s), kh, 0, 0))
    return pl.pallas_call(
        functools.partial(_swa_kernel, blocks_per_seq=blocks_per_seq),
        grid=(kvn, n_steps),
        in_specs=[pl.BlockSpec((SWA_BLOCK, qw), lambda kh, s: (cur_blk(s), kh)),
                  kv_spec(0, lambda s: meta_blk), kv_spec(0, prev_blk), kv_spec(0, cur_blk),
                  kv_spec(kvn, lambda s: meta_blk), kv_spec(kvn, prev_blk), kv_spec(kvn, cur_blk),
                  bias_spec, bias_spec, bias_spec],
        out_specs=pl.BlockSpec((SWA_BLOCK, qw), lambda kh, s: (cur_blk(s), kh)),
        out_shape=jax.ShapeDtypeStruct((rows, kvn * qw), BF16),
        compiler_params=_params("parallel", "arbitrary"),
        name="swa_mix",
    )(q, kv2, kv2, kv2, kv2, kv2, kv2, bias_m, bias_p, bias_c)


def _t5_bucket(dist):
    exact = REL_BUCKETS // 2
    d = jnp.maximum(dist, 0)
    df = jnp.maximum(d, 1).astype(F32)
    large = exact + (jnp.log(df / exact) / math.log(REL_MAX_DIST / exact)
                     * (REL_BUCKETS - exact)).astype(jnp.int32)
    large = jnp.minimum(large, REL_BUCKETS - 1)
    return jnp.where(d < exact, d, large)


def _pair_layout(bias):
    nv = bias.shape[0]
    b = bias.reshape(nv, SWA_KV_HEADS, SWA_PAIRS, 2, SWA_BLOCK, SWA_BLOCK)
    b = jnp.transpose(b, (0, 1, 2, 4, 3, 5))
    return b.reshape(nv, SWA_KV_HEADS, SWA_PAIRS * SWA_BLOCK, 2 * SWA_BLOCK)


def _swa_bias_static(table):
    tab = table.astype(F32)
    blk = SWA_BLOCK
    i = jnp.arange(blk, dtype=jnp.int32)[:, None]
    j = jnp.arange(blk, dtype=jnp.int32)[None, :]
    m = jnp.arange(N_META, dtype=jnp.int32)[None, :]

    def lookup(dist):
        onehot = (_t5_bucket(dist)[..., None] == jnp.arange(REL_BUCKETS, dtype=jnp.int32)).astype(F32)
        return jnp.einsum('qsb,bh->hqs', onehot, tab, precision=lax.Precision.HIGHEST)

    neg = jnp.full((SWA_Q_HEADS, blk, blk), NEG_INF, F32)
    cur = jnp.where((i - j >= 0)[None], lookup(i - j), NEG_INF)
    d_prev = blk + i - j
    prev = jnp.where(((d_prev >= 0) & (d_prev < SWA_WINDOW))[None], lookup(d_prev), NEG_INF)
    meta0 = lookup(N_META + i - m)
    meta1 = lookup(N_META + blk + i - m)
    same_seq = (i // N_META) == (j // N_META)
    dm = (i % N_META) - (j % N_META)
    meta_tile = jnp.where((same_seq & (dm >= 0))[None], lookup(dm), NEG_INF)
    bias_p = _pair_layout(jnp.stack([neg, prev, neg]))
    bias_c = _pair_layout(jnp.stack([cur, cur, meta_tile]))
    return bias_p, bias_c, jnp.stack([meta0, meta1])


def _swa_bias_meta(meta01, sinks):
    nh, blk = SWA_Q_HEADS, SWA_BLOCK
    sink_col = jnp.broadcast_to(sinks.astype(F32)[None, :, None, None], (3, nh, blk, 1))
    meta = jnp.concatenate([meta01, jnp.full((1, nh, blk, N_META), NEG_INF, F32)], axis=0)
    rest = jnp.full((3, nh, blk, blk - N_META - 1), NEG_INF, F32)
    return _pair_layout(jnp.concatenate([meta, sink_col, rest], axis=-1))


def kernel(x, meta_tokens, rel_bias_table, ln_gain, ln_bias, gla_w_in, gla_w_gate2, gla_b_gate,
           gla_norm_gain, gla_w_out, kv_w_shared, swa_w_q, swa_sinks, swa_w_out,
           ffn_w_gate_up, ffn_w_down, moe_w_router, moe_w_gate_up, moe_w_down):
    bsz, seq, d = x.shape
    n_real = bsz * seq
    h = jnp.concatenate([x.reshape(n_real, d),
                         jnp.broadcast_to(meta_tokens.astype(x.dtype)[None], (bsz, N_META, d))
                         .reshape(bsz * N_META, d)], axis=0)
    rows = h.shape[0]
    hb = h.astype(BF16)
    bias_p, bias_c, meta01 = _swa_bias_static(rel_bias_table)

    w_in_b = gla_w_in.astype(BF16)
    w_gla_out_b = gla_w_out.astype(BF16)
    w_q_b = swa_w_q.astype(BF16)
    w_swa_out_b = swa_w_out.astype(BF16)
    w_ffn_gu_b = ffn_w_gate_up.astype(BF16)
    w_ffn_down_b = ffn_w_down.astype(BF16)
    hd = SWA_HEAD_DIM
    w_kv = kv_w_shared.reshape(d, 2 * SWA_KV_HEADS, 1, hd)
    w_kv2 = jnp.broadcast_to(w_kv, (d, 2 * SWA_KV_HEADS, 2, hd)).reshape(1, d, 4 * SWA_KV_HEADS * hd)
    w_kv2 = w_kv2.astype(BF16)

    kv2 = None
    for li in range(DEPTH):
        g0, b0 = ln_gain[li, 0][None, :], ln_bias[li, 0][None, :]
        g1, b1 = ln_gain[li, 1][None, :], ln_bias[li, 1][None, :]
        if li < N_A_LAYERS:
            qkvr = linear(hb, w_in_b, li, GLA_MAIN, F32)
            wg_pad = jnp.pad(w_in_b[li, :, GLA_MAIN:], ((0, 0), (0, LANES - GLA_GATE_RANK)))
            w2_pad = jnp.pad(gla_w_gate2[li].astype(BF16), ((0, LANES - GLA_GATE_RANK), (0, 0)))
            log_a = gla_gate(hb, wg_pad, w2_pad, gla_b_gate[li][None, :])
            mix_in = gla_mix(qkvr, log_a, gla_norm_gain[li][None, :], bsz, seq)
            h, hb = linear_res_ln(mix_in, w_gla_out_b, li, h, g0, b0)
        else:
            jb = li - N_A_LAYERS
            q = linear(hb, w_q_b, jb, d, BF16, scale=hd ** -0.5)
            bias_m = _swa_bias_meta(meta01, swa_sinks[jb])
            mix_in = swa_mix(q, kv2, bias_m, bias_p, bias_c, bsz, seq)
            h, hb = linear_res_ln(mix_in, w_swa_out_b, jb, h, g0, b0)
        if li % 2 == 0:
            h, hb = ffn_res_ln(hb, h, w_ffn_gu_b, w_ffn_down_b, li // 2, g1, b1)
        else:
            w_r = jnp.pad(moe_w_router[li // 2].astype(BF16), ((0, 0), (0, LANES - N_EXPERTS)))
            out_rows = n_real if li == DEPTH - 1 else rows
            h, hb = moe_res_ln(h, hb, w_r, moe_w_gate_up, moe_w_down, li // 2, g1, b1, out_rows)
        if li == N_A_LAYERS - 1:
            kv2 = linear(hb, w_kv2, 0, w_kv2.shape[-1], BF16)
    return h.reshape(bsz, seq, d)
```

```python
import functools
import math

import jax
import jax.numpy as jnp
from jax import lax
from jax.experimental import pallas as pl
from jax.experimental.pallas import tpu as pltpu

F32 = jnp.float32
BF16 = jnp.bfloat16

D_MODEL = 2048
DEPTH = 4
N_META = 16
N_A_LAYERS = DEPTH // 2
DN_ALPHA = (2 * DEPTH) ** 0.25
LN_EPS = 1e-5

GLA_HEADS = 4
GLA_DK = D_MODEL // 2
GLA_DV = D_MODEL
GLA_DK_HEAD = GLA_DK // GLA_HEADS
GLA_DV_HEAD = GLA_DV // GLA_HEADS
GLA_GATE_RANK = 16
GLA_GATE_TAU = 16.0
GLA_CHUNK = 64
GLA_MAIN = 2 * GLA_DK + 2 * GLA_DV

SWA_HEAD_DIM = 64
SWA_Q_HEADS = D_MODEL // SWA_HEAD_DIM
SWA_GROUP = 8
SWA_KV_HEADS = SWA_Q_HEADS // SWA_GROUP
SWA_PAIRS = SWA_GROUP // 2
SWA_PAIRS_PER_GROUP = 1
SWA_WINDOW = 128
SWA_BLOCK = 128

REL_BUCKETS = 32
REL_MAX_DIST = 128

FFN_DIM = 7 * D_MODEL // 2
N_EXPERTS = 8
TOP_K = 2
NEG_INF = -1e9

LANES = 128
VMEM_LIMIT = 56 * 1024 * 1024
ROW_TILE = 512
LN_ROW_PARTS = 4
PROJ_ROW_TILE = 1376
PROJ_COL_TILE = 1024
FFN_TILE = 512
FFN_ROW_TILE = 688
MOE_ROW_TILE = 1024
MOE_GATHER_ROWS = 76
MOE_VMEM_LIMIT = 62 * 1024 * 1024
GLA_ROWS = 1024
GLA_SUB_ROWS = 256


def _params(*sem):
    return pltpu.CompilerParams(dimension_semantics=sem, vmem_limit_bytes=VMEM_LIMIT)


def _layer_norm_rows(y, g, b):
    mu = jnp.mean(y, axis=-1, keepdims=True)
    yc = y - mu
    var = jnp.mean(yc * yc, axis=-1, keepdims=True)
    return yc * lax.rsqrt(var + LN_EPS) * g + b


def _dot(a, b):
    return jnp.dot(a, b, preferred_element_type=F32)


def _dot_nt(a, b):
    return lax.dot_general(a, b, (((1,), (1,)), ((), ())), preferred_element_type=F32)


def _dot_tn(a, b, precision=None):
    return lax.dot_general(a, b, (((0,), (0,)), ((), ())), preferred_element_type=F32,
                           precision=precision)


def _linear_kernel(x_ref, w_ref, o_ref, *, scale):
    acc = _dot(x_ref[...], w_ref[...])
    if scale != 1.0:
        acc = acc * scale
    o_ref[...] = acc.astype(o_ref.dtype)


def linear(x, w, layer, n_out, out_dtype, scale=1.0):
    rows, k = x.shape
    tm = PROJ_ROW_TILE if rows % PROJ_ROW_TILE == 0 else ROW_TILE
    tn = PROJ_COL_TILE
    return pl.pallas_call(
        functools.partial(_linear_kernel, scale=scale),
        grid=(pl.cdiv(rows, tm), n_out // tn),
        in_specs=[pl.BlockSpec((tm, k), lambda i, j: (i, 0)),
                  pl.BlockSpec((None, k, tn), lambda i, j: (layer, 0, j))],
        out_specs=pl.BlockSpec((tm, tn), lambda i, j: (i, j)),
        out_shape=jax.ShapeDtypeStruct((rows, n_out), out_dtype),
        compiler_params=_params("parallel", "arbitrary"),
        name="linear",
    )(x, w)


def _gla_in_proj_kernel(x_ref, w_ref, wg_ref, w2_ref, b_ref, o_ref, la_ref):
    x = x_ref[...]
    o_ref[...] = _dot(x, w_ref[...])

    @pl.when(pl.program_id(1) == 0)
    def _():
        g_low = _dot(x, wg_ref[...])
        z = _dot(g_low.astype(BF16), w2_ref[...]) + b_ref[...]
        log_sig = jnp.minimum(z, 0.0) - jnp.log1p(jnp.exp(-jnp.abs(z)))
        la_ref[...] = log_sig / GLA_GATE_TAU


def gla_in_proj(x, w, layer, wg_pad, w2_pad, b_gate):
    rows, k = x.shape
    tm = PROJ_ROW_TILE if rows % PROJ_ROW_TILE == 0 else ROW_TILE
    tn = PROJ_COL_TILE
    return pl.pallas_call(
        _gla_in_proj_kernel,
        grid=(pl.cdiv(rows, tm), GLA_MAIN // tn),
        in_specs=[pl.BlockSpec((tm, k), lambda i, j: (i, 0)),
                  pl.BlockSpec((None, k, tn), lambda i, j: (layer, 0, j)),
                  pl.BlockSpec((k, LANES), lambda i, j: (0, 0)),
                  pl.BlockSpec((LANES, GLA_DK), lambda i, j: (0, 0)),
                  pl.BlockSpec((1, GLA_DK), lambda i, j: (0, 0))],
        out_specs=[pl.BlockSpec((tm, tn), lambda i, j: (i, j)),
                   pl.BlockSpec((tm, GLA_DK), lambda i, j: (i, 0))],
        out_shape=[jax.ShapeDtypeStruct((rows, GLA_MAIN), F32),
                   jax.ShapeDtypeStruct((rows, GLA_DK), F32)],
        compiler_params=_params("parallel", "arbitrary"),
        name="gla_in_proj",
    )(x, w, wg_pad, w2_pad, b_gate)


def _linear_res_ln_kernel(x_ref, w_ref, res_ref, g_ref, b_ref, o_ref, ob_ref):
    part = x_ref.shape[0] // LN_ROW_PARTS
    parts = [slice(k * part, (k + 1) * part) for k in range(LN_ROW_PARTS)]
    ys = [DN_ALPHA * res_ref[sl, :] + _dot(x_ref[sl, :], w_ref[...]) for sl in parts]
    for sl, y in zip(parts, ys):
        o = _layer_norm_rows(y, g_ref[...], b_ref[...])
        o_ref[sl, :] = o
        ob_ref[sl, :] = o.astype(BF16)


def linear_res_ln(x, w, layer, res, g, b):
    rows, k = x.shape
    d = w.shape[-1]
    return pl.pallas_call(
        _linear_res_ln_kernel,
        grid=(pl.cdiv(rows, ROW_TILE),),
        in_specs=[pl.BlockSpec((ROW_TILE, k), lambda i: (i, 0)),
                  pl.BlockSpec((None, k, d), lambda i: (layer, 0, 0)),
                  pl.BlockSpec((ROW_TILE, d), lambda i: (i, 0)),
                  pl.BlockSpec((1, d), lambda i: (0, 0)),
                  pl.BlockSpec((1, d), lambda i: (0, 0))],
        out_specs=[pl.BlockSpec((ROW_TILE, d), lambda i: (i, 0)),
                   pl.BlockSpec((ROW_TILE, d), lambda i: (i, 0))],
        out_shape=[jax.ShapeDtypeStruct((rows, d), F32), jax.ShapeDtypeStruct((rows, d), BF16)],
        compiler_params=_params("parallel"),
        name="linear_res_ln",
    )(x, w, res, g, b)


def _swiglu_partial(x, wa, wu, wd):
    a = _dot(x, wa)
    u = _dot(x, wu)
    hidden = (a * jax.nn.sigmoid(a) * u).astype(BF16)
    return _dot(hidden, wd)


def _ffn_kernel(x_ref, wa_ref, wu_ref, wd_ref, res_ref, g_ref, b_ref, o_ref, ob_ref):
    j = pl.program_id(1)

    @pl.when(j == 0)
    def _():
        o_ref[...] = jnp.zeros_like(o_ref)

    o_ref[...] += _swiglu_partial(x_ref[...], wa_ref[...], wu_ref[...], wd_ref[...])

    @pl.when(j == pl.num_programs(1) - 1)
    def _():
        o = _layer_norm_rows(DN_ALPHA * res_ref[...] + o_ref[...], g_ref[...], b_ref[...])
        o_ref[...] = o
        ob_ref[...] = o.astype(BF16)


def ffn_res_ln(xb, res, w_gu, w_down, layer, g, b):
    rows, d = xb.shape
    f = w_down.shape[-2]
    nj = f // FFN_TILE
    tm = FFN_ROW_TILE if rows % FFN_ROW_TILE == 0 else ROW_TILE
    return pl.pallas_call(
        _ffn_kernel,
        grid=(pl.cdiv(rows, tm), nj),
        in_specs=[pl.BlockSpec((tm, d), lambda i, j: (i, 0)),
                  pl.BlockSpec((None, d, FFN_TILE), lambda i, j: (layer, 0, j)),
                  pl.BlockSpec((None, d, FFN_TILE), lambda i, j: (layer, 0, nj + j)),
                  pl.BlockSpec((None, FFN_TILE, d), lambda i, j: (layer, j, 0)),
                  pl.BlockSpec((tm, d), lambda i, j: (i, 0)),
                  pl.BlockSpec((1, d), lambda i, j: (0, 0)),
                  pl.BlockSpec((1, d), lambda i, j: (0, 0))],
        out_specs=[pl.BlockSpec((tm, d), lambda i, j: (i, 0)),
                   pl.BlockSpec((tm, d), lambda i, j: (i, 0))],
        out_shape=[jax.ShapeDtypeStruct((rows, d), F32), jax.ShapeDtypeStruct((rows, d), BF16)],
        compiler_params=_params("parallel", "arbitrary"),
        name="ffn_res_ln",
    )(xb, w_gu, w_gu, w_down, res, g, b)


def _router_kernel(x_ref, w_ref, o_ref):
    logits = _dot(x_ref[...], w_ref[...])
    lane = lax.broadcasted_iota(jnp.int32, logits.shape, 1)
    lane_f = lane.astype(F32)
    logits = jnp.where(lane < N_EXPERTS, logits, -jnp.inf)
    m1 = jnp.max(logits, axis=-1, keepdims=True)
    i1 = jnp.min(jnp.where(logits == m1, lane_f, float(LANES)), axis=-1, keepdims=True)
    rest = jnp.where(lane_f == i1, -jnp.inf, logits)
    m2 = jnp.max(rest, axis=-1, keepdims=True)
    i2 = jnp.min(jnp.where(rest == m2, lane_f, float(LANES)), axis=-1, keepdims=True)
    e2 = jnp.exp(m2 - m1)
    g1 = 1.0 / (1.0 + e2)
    g2 = e2 / (1.0 + e2)
    o_ref[...] = jnp.where(lane == 0, i1, jnp.where(lane == 1, i2,
                           jnp.where(lane == 2, g1, jnp.where(lane == 3, g2, 0.0))))


def router(xb, w_router_pad):
    rows, d = xb.shape
    return pl.pallas_call(
        _router_kernel,
        grid=(pl.cdiv(rows, ROW_TILE),),
        in_specs=[pl.BlockSpec((ROW_TILE, d), lambda i: (i, 0)),
                  pl.BlockSpec((d, LANES), lambda i: (0, 0))],
        out_specs=pl.BlockSpec((ROW_TILE, LANES), lambda i: (i, 0)),
        out_shape=jax.ShapeDtypeStruct((rows, LANES), F32),
        compiler_params=_params("parallel"),
        name="router",
    )(xb, w_router_pad)


def _moe_kernel(blk_e_ref, n_half_ref, tok_ref, h_hbm, wa_ref, wu_ref, wd_ref, o_ref,
                land_ref, x_ref, sem):
    del blk_e_ref
    i = pl.program_id(0)
    j = pl.program_id(1)
    n_blk = pl.num_programs(0)
    nj = pl.num_programs(1)
    n_half = n_half_ref[i]
    half = MOE_ROW_TILE // 2
    land_rows = land_ref.shape[0]

    def row_copy(slot, dst_row):
        return pltpu.make_async_copy(h_hbm.at[pl.ds(tok_ref[slot], 1), :],
                                     land_ref.at[pl.ds(dst_row, 1), :], sem)

    def wait_landing():
        pltpu.make_async_copy(h_hbm.at[pl.ds(0, land_rows), :], land_ref, sem).wait()

    @pl.when(j == 0)
    def _():
        o_ref[...] = jnp.zeros_like(o_ref)

        @pl.when(i == 0)
        def _():
            def issue(r, carry):
                row_copy(r, r).start()
                return carry
            lax.fori_loop(0, land_rows, issue, 0, unroll=8)

        @pl.when(jnp.logical_or(i == 0, n_half_ref[jnp.maximum(i - 1, 0)] > 0))
        def _():
            wait_landing()

        for hf in range(2):
            @pl.when(n_half > hf)
            def _():
                sl = slice(hf * half, (hf + 1) * half)
                x_ref[sl, :] = land_ref[sl, :].astype(BF16)

    @pl.when(n_half > 0)
    def _():
        next_base = jnp.minimum(i + 1, n_blk - 1) * MOE_ROW_TILE
        r0 = j * MOE_GATHER_ROWS
        for k in range(MOE_GATHER_ROWS):
            row_copy(next_base + r0 + k, r0 + k).start()
        wa = wa_ref[...].astype(BF16)
        wu = wu_ref[...].astype(BF16)
        wd = wd_ref[...].astype(BF16)
        o_ref[:half, :] += _swiglu_partial(x_ref[:half, :], wa, wu, wd)

        @pl.when(n_half > 1)
        def _():
            o_ref[half:, :] += _swiglu_partial(x_ref[half:, :], wa, wu, wd)

    @pl.when(jnp.logical_and(jnp.logical_and(i == n_blk - 1, j == nj - 1), n_half > 0))
    def _():
        wait_landing()


def moe_experts(h, buf_tok, blk_e, n_half, w_gu, w_down, layer):
    d = h.shape[1]
    p_rows = buf_tok.shape[0]
    f = w_down.shape[-2]
    nj = f // FFN_TILE
    n_blk = p_rows // MOE_ROW_TILE
    assert nj * MOE_GATHER_ROWS >= MOE_ROW_TILE
    buf_tok = jnp.pad(buf_tok, (0, nj * MOE_GATHER_ROWS - MOE_ROW_TILE))

    def jj(i, j, nh):
        return jnp.where(nh[i] > 0, j, nj - 1)

    grid_spec = pltpu.PrefetchScalarGridSpec(
        num_scalar_prefetch=3,
        grid=(n_blk, nj),
        in_specs=[pl.BlockSpec(memory_space=pl.ANY),
                  pl.BlockSpec((None, None, d, FFN_TILE),
                               lambda i, j, be, nh, tk: (layer, be[i], 0, jj(i, j, nh))),
                  pl.BlockSpec((None, None, d, FFN_TILE),
                               lambda i, j, be, nh, tk: (layer, be[i], 0, nj + jj(i, j, nh))),
                  pl.BlockSpec((None, None, FFN_TILE, d),
                               lambda i, j, be, nh, tk: (layer, be[i], jj(i, j, nh), 0))],
        out_specs=pl.BlockSpec((MOE_ROW_TILE, d), lambda i, j, be, nh, tk: (i, 0)),
        scratch_shapes=[pltpu.VMEM((nj * MOE_GATHER_ROWS, d), F32),
                        pltpu.VMEM((MOE_ROW_TILE, d), BF16),
                        pltpu.SemaphoreType.DMA(())],
    )
    return pl.pallas_call(
        _moe_kernel,
        grid_spec=grid_spec,
        out_shape=jax.ShapeDtypeStruct((p_rows, d), F32),
        compiler_params=pltpu.CompilerParams(dimension_semantics=("arbitrary", "arbitrary"),
                                             vmem_limit_bytes=MOE_VMEM_LIMIT),
        name="moe_experts",
    )(blk_e, n_half, buf_tok, h, w_gu, w_gu, w_down)


def _combine_ln_kernel(res_ref, y0_ref, y1_ref, gate_ref, g_ref, b_ref, o_ref, ob_ref):
    gate = gate_ref[...]
    moe = y0_ref[...] * gate[:, 2:3] + y1_ref[...] * gate[:, 3:4]
    o = _layer_norm_rows(DN_ALPHA * res_ref[...] + moe, g_ref[...], b_ref[...])
    o_ref[...] = o
    ob_ref[...] = o.astype(BF16)


def combine_ln(res, y0, y1, route, g, b, out_rows):
    d = res.shape[1]
    row_spec = pl.BlockSpec((ROW_TILE, d), lambda i: (i, 0))
    vec_spec = pl.BlockSpec((1, d), lambda i: (0, 0))
    return pl.pallas_call(
        _combine_ln_kernel,
        grid=(pl.cdiv(out_rows, ROW_TILE),),
        in_specs=[row_spec, row_spec, row_spec, pl.BlockSpec((ROW_TILE, LANES), lambda i: (i, 0)),
                  vec_spec, vec_spec],
        out_specs=[row_spec, row_spec],
        out_shape=[jax.ShapeDtypeStruct((out_rows, d), F32),
                   jax.ShapeDtypeStruct((out_rows, d), BF16)],
        compiler_params=_params("parallel"),
        name="combine_ln",
    )(res, y0, y1, route, g, b)


def moe_res_ln(h, hb, w_router_pad, w_gu, w_down, layer, g, b, out_rows):
    rows, d = h.shape
    route = router(hb, w_router_pad)
    flat_e = route[:, :TOP_K].astype(jnp.int32).reshape(-1)
    n_flat = rows * TOP_K
    n_blk = -(-n_flat // MOE_ROW_TILE) + N_EXPERTS
    p_rows = n_blk * MOE_ROW_TILE
    onehot = (flat_e[:, None] == jnp.arange(N_EXPERTS, dtype=jnp.int32)[None, :]).astype(jnp.int32)
    csum = jnp.cumsum(onehot, axis=0)
    rank = jnp.sum(csum * onehot, axis=1) - 1
    counts = csum[-1]
    padded = ((counts + MOE_ROW_TILE - 1) // MOE_ROW_TILE) * MOE_ROW_TILE
    padded_end = jnp.cumsum(padded)
    start_padded = padded_end - padded
    pos = start_padded[flat_e] + rank
    flat_tok = jnp.arange(n_flat, dtype=jnp.int32) // TOP_K
    buf_tok = jnp.zeros((p_rows,), jnp.int32).at[pos].set(flat_tok)
    blk_start = jnp.arange(n_blk, dtype=jnp.int32) * MOE_ROW_TILE
    blk_e = jnp.minimum(jnp.searchsorted(padded_end, blk_start, side='right'),
                        N_EXPERTS - 1).astype(jnp.int32)
    valid = jnp.clip(counts[blk_e] - (blk_start - start_padded[blk_e]), 0, MOE_ROW_TILE)
    valid = jnp.where(blk_start < padded_end[-1], valid, 0)
    half = MOE_ROW_TILE // 2
    n_half = ((valid + half - 1) // half).astype(jnp.int32)
    y = moe_experts(h, buf_tok, blk_e, n_half, w_gu, w_down, layer)
    pos2 = pos.reshape(rows, TOP_K)
    y0 = y.at[pos2[:, 0]].get(mode="promise_in_bounds")
    y1 = y.at[pos2[:, 1]].get(mode="promise_in_bounds")
    return combine_ln(h, y0, y1, route, g, b, out_rows)


def _head_norm_gate(o, r, gain):
    mu = jnp.mean(o, axis=-1, keepdims=True)
    oc = o - mu
    var = jnp.mean(oc * oc, axis=-1, keepdims=True)
    on = oc * lax.rsqrt(var + LN_EPS) * gain
    return (on * (r * jax.nn.sigmoid(r))).astype(BF16)


def _masked_cumsum(mask, x):
    m = mask.astype(BF16)
    hi = x.astype(BF16)
    r1 = x - hi.astype(F32)
    mid = r1.astype(BF16)
    lo = (r1 - mid.astype(F32)).astype(BF16)
    return _dot(m, hi) + _dot(m, mid) + _dot(m, lo)


def _gla_first_chunk(q, k, v, r, la, gain):
    c = q.shape[0]
    row = lax.broadcasted_iota(jnp.int32, (c, c), 0)
    col = lax.broadcasted_iota(jnp.int32, (c, c), 1)
    causal = row >= col
    bcum = _masked_cumsum(causal, la)
    q_dec = (q * (GLA_DK_HEAD ** -0.5) * jnp.exp(bcum)).astype(BF16)
    k_inv = (k * jnp.exp(-bcum)).astype(BF16)
    vb = v.astype(BF16)
    att = jnp.where(causal, _dot_nt(q_dec, k_inv), 0.0)
    o = _dot(att.astype(BF16), vb)
    k_state = (k * jnp.exp(bcum[c - 1:c, :] - bcum)).astype(BF16)
    return _head_norm_gate(o, r, gain), _dot_tn(k_state, vb)


def _gla_intra(q, k, v, la):
    n = q.shape[0]
    c = GLA_CHUNK
    row = lax.broadcasted_iota(jnp.int32, (n, n), 0)
    col = lax.broadcasted_iota(jnp.int32, (n, n), 1)
    causal = (row >= col) & ((row // c) == (col // c))
    bcum = _masked_cumsum(causal, la)
    q_dec = (q * (GLA_DK_HEAD ** -0.5) * jnp.exp(bcum)).astype(BF16)
    k_inv = (k * jnp.exp(-bcum)).astype(BF16)
    vb = v.astype(BF16)
    att = jnp.where(causal, _dot_nt(q_dec, k_inv), 0.0)
    o_intra = _dot(att.astype(BF16), vb)
    b_last = [bcum[(ci + 1) * c - 1:(ci + 1) * c, :] for ci in range(n // c)]
    b_tot = jnp.concatenate([jnp.broadcast_to(bl, (c, bl.shape[1])) for bl in b_last], axis=0)
    k_state = (k * jnp.exp(b_tot - bcum)).astype(BF16)
    return q_dec, k_state, vb, o_intra, b_last


def _gla_walk(state, q_dec, k_state, vb, b_last):
    c = GLA_CHUNK
    o_inter = []
    for ci, bl in enumerate(b_last):
        sl = slice(ci * c, (ci + 1) * c)
        o_inter.append(_dot(q_dec[sl], state.astype(BF16)))
        decay = jnp.exp(jnp.transpose(jnp.broadcast_to(bl, (LANES, bl.shape[1]))))
        decay = jnp.concatenate([decay] * (GLA_DV_HEAD // LANES), axis=1)
        state = state * decay + _dot_tn(k_state[sl], vb[sl])
    return jnp.concatenate(o_inter, axis=0), state


def _gla_kernel(qm_ref, km_ref, vm_ref, rm_ref, lam_ref, q_ref, k_ref, v_ref, r_ref, la_ref,
                gain_ref, o_ref, state_ref, *, blocks_per_seq):
    s = pl.program_id(1)
    gain = gain_ref[...]
    n_batch = state_ref.shape[0]

    @pl.when(s == 0)
    def _():
        for bi in range(n_batch):
            sl = slice(bi * N_META, (bi + 1) * N_META)
            og, st = _gla_first_chunk(qm_ref[sl, :], km_ref[sl, :], vm_ref[sl, :], rm_ref[sl, :],
                                      lam_ref[sl, :], gain)
            state_ref[bi] = st
            o_ref[sl, :] = og

    @pl.when(s > 0)
    def _():
        bi = (s - 1) // blocks_per_seq
        subs = [slice(r0, r0 + GLA_SUB_ROWS) for r0 in range(0, q_ref.shape[0], GLA_SUB_ROWS)]
        intra = [_gla_intra(q_ref[sl, :], k_ref[sl, :], v_ref[sl, :], la_ref[sl, :]) for sl in subs]
        st = state_ref[bi]
        for sl, (q_dec, k_state, vb, o_intra, b_last) in zip(subs, intra):
            o_inter, st = _gla_walk(st, q_dec, k_state, vb, b_last)
            o_ref[sl, :] = _head_norm_gate(o_intra + o_inter, r_ref[sl, :], gain)
        state_ref[bi] = st


def gla_mix(qkvr, log_a, norm_gain, n_batch, seq):
    rows = qkvr.shape[0]
    n_real = n_batch * seq
    meta_rows = n_batch * N_META
    blocks_per_seq = seq // GLA_ROWS
    n_steps = 1 + n_real // GLA_ROWS
    meta_blk = n_real // meta_rows
    meta_out_blk = n_real // GLA_ROWS
    dk, dv, nh = GLA_DK_HEAD, GLA_DV_HEAD, GLA_HEADS
    kq, kk, kv_, kr = 0, GLA_DK // dk, 2 * GLA_DK // dv, (2 * GLA_DK + GLA_DV) // dv

    def real_blk(s):
        return jnp.maximum(s - 1, 0)

    def meta_spec(width, col0):
        return pl.BlockSpec((meta_rows, width), lambda h, s: (meta_blk, col0 + h))

    def real_spec(width, col0):
        return pl.BlockSpec((GLA_ROWS, width), lambda h, s: (real_blk(s), col0 + h))

    return pl.pallas_call(
        functools.partial(_gla_kernel, blocks_per_seq=blocks_per_seq),
        grid=(nh, n_steps),
        in_specs=[meta_spec(dk, kq), meta_spec(dk, kk), meta_spec(dv, kv_), meta_spec(dv, kr),
                  meta_spec(dk, 0),
                  real_spec(dk, kq), real_spec(dk, kk), real_spec(dv, kv_), real_spec(dv, kr),
                  real_spec(dk, 0),
                  pl.BlockSpec((1, dv), lambda h, s: (0, h))],
        out_specs=pl.BlockSpec((GLA_ROWS, dv),
                               lambda h, s: (jnp.where(s == 0, meta_out_blk, s - 1), h)),
        out_shape=jax.ShapeDtypeStruct((rows, GLA_DV), BF16),
        scratch_shapes=[pltpu.VMEM((n_batch, dk, dv), F32)],
        compiler_params=_params("parallel", "arbitrary"),
        name="gla_mix",
    )(qkvr, qkvr, qkvr, qkvr, log_a, qkvr, qkvr, qkvr, qkvr, log_a, norm_gain)


def _block_diag2(x2):
    lane = lax.broadcasted_iota(jnp.int32, x2.shape, 1)
    zero = jnp.zeros_like(x2)
    return jnp.concatenate([jnp.where(lane < SWA_HEAD_DIM, x2, zero),
                            jnp.where(lane >= SWA_HEAD_DIM, x2, zero)], axis=0)


def _swa_kernel(q_ref, km_ref, kp_ref, kc_ref, vm_ref, vp_ref, vc_ref,
                bm_ref, bp_ref, bc_ref, o_ref, *, blocks_per_seq):
    s = pl.program_id(1)
    bi = jnp.maximum(s - 1, 0) // blocks_per_seq
    m0 = pl.multiple_of(bi * N_META, N_META)
    pad = jnp.zeros((SWA_BLOCK - N_META, LANES), BF16)
    k_m = jnp.concatenate([km_ref[pl.ds(m0, N_META), :], pad], axis=0)
    v_m = jnp.concatenate([vm_ref[pl.ds(m0, N_META), :], pad], axis=0)
    kbd = [_block_diag2(k_m), _block_diag2(kp_ref[...]), _block_diag2(kc_ref[...])]
    vbd = [_block_diag2(v_m), _block_diag2(vp_ref[...]), _block_diag2(vc_ref[...])]
    bias_refs = (bm_ref, bp_ref, bc_ref)
    groups = [range(g0, g0 + SWA_PAIRS_PER_GROUP) for g0 in range(0, SWA_PAIRS, SWA_PAIRS_PER_GROUP)]
    scores = []
    for pairs in groups:
        rows = slice(pairs[0] * SWA_BLOCK, (pairs[-1] + 1) * SWA_BLOCK)
        qp = jnp.concatenate([q_ref[:, p * LANES:(p + 1) * LANES] for p in pairs], axis=0)
        scores.append([_dot_nt(qp, kbd[t]) + bias_refs[t][0, 0, rows, :] for t in range(3)])
    for pairs, sc in zip(groups, scores):
        top = jnp.maximum(jnp.maximum(sc[0], sc[1]), sc[2])
        m_even = jnp.max(top[:, :LANES], axis=-1, keepdims=True)
        m_odd = jnp.max(top[:, LANES:], axis=-1, keepdims=True)
        lane2 = lax.broadcasted_iota(jnp.int32, top.shape, 1)
        m_full = jnp.where(lane2 < LANES, m_even, m_odd)
        pr = [jnp.exp(sc[t] - m_full) for t in range(3)]
        p_sum = pr[0] + pr[1] + pr[2]
        l_even = jnp.sum(p_sum[:, :LANES], axis=-1, keepdims=True)
        l_odd = jnp.sum(p_sum[:, LANES:], axis=-1, keepdims=True)
        o = (_dot(pr[0].astype(BF16), vbd[0]) + _dot(pr[1].astype(BF16), vbd[1])
             + _dot(pr[2].astype(BF16), vbd[2]))
        lane1 = lax.broadcasted_iota(jnp.int32, o.shape, 1)
        o = o / jnp.where(lane1 < SWA_HEAD_DIM, l_even, l_odd)
        for k, p in enumerate(pairs):
            o_ref[:, p * LANES:(p + 1) * LANES] = (
                o[k * SWA_BLOCK:(k + 1) * SWA_BLOCK].astype(o_ref.dtype))


def swa_mix(q, kv2, bias_m, bias_p, bias_c, n_batch, seq):
    rows = q.shape[0]
    blocks_per_seq = seq // SWA_BLOCK
    n_real_blk = n_batch * blocks_per_seq
    n_steps = 1 + n_real_blk
    meta_blk = n_real_blk
    assert n_batch * N_META == SWA_BLOCK
    kvn, qw = SWA_KV_HEADS, SWA_GROUP * SWA_HEAD_DIM

    def cur_blk(s):
        return jnp.where(s == 0, meta_blk, s - 1)

    def prev_blk(s):
        return jnp.maximum(s - 2, 0)

    def variant(s):
        return jnp.where(s == 0, 2, jnp.where((s - 1) % blocks_per_seq == 0, 0, 1))

    def kv_spec(head0, blk_fn):
        return pl.BlockSpec((SWA_BLOCK, LANES), lambda kh, s: (blk_fn(s), head0 + kh))

    bias_spec = pl.BlockSpec((1, 1, SWA_PAIRS * SWA_BLOCK, 2 * SWA_BLOCK),
                             lambda kh, s: (variant(s), kh, 0, 0))
    return pl.pallas_call(
        functools.partial(_swa_kernel, blocks_per_seq=blocks_per_seq),
        grid=(kvn, n_steps),
        in_specs=[pl.BlockSpec((SWA_BLOCK, qw), lambda kh, s: (cur_blk(s), kh)),
                  kv_spec(0, lambda s: meta_blk), kv_spec(0, prev_blk), kv_spec(0, cur_blk),
                  kv_spec(kvn, lambda s: meta_blk), kv_spec(kvn, prev_blk), kv_spec(kvn, cur_blk),
                  bias_spec, bias_spec, bias_spec],
        out_specs=pl.BlockSpec((SWA_BLOCK, qw), lambda kh, s: (cur_blk(s), kh)),
        out_shape=jax.ShapeDtypeStruct((rows, kvn * qw), BF16),
        compiler_params=_params("parallel", "arbitrary"),
        name="swa_mix",
    )(q, kv2, kv2, kv2, kv2, kv2, kv2, bias_m, bias_p, bias_c)


def _t5_bucket(dist):
    exact = REL_BUCKETS // 2
    d = jnp.maximum(dist, 0)
    df = jnp.maximum(d, 1).astype(F32)
    large = exact + (jnp.log(df / exact) / math.log(REL_MAX_DIST / exact)
                     * (REL_BUCKETS - exact)).astype(jnp.int32)
    large = jnp.minimum(large, REL_BUCKETS - 1)
    return jnp.where(d < exact, d, large)


def _pair_layout(bias):
    nv = bias.shape[0]
    b = bias.reshape(nv, SWA_KV_HEADS, SWA_PAIRS, 2, SWA_BLOCK, SWA_BLOCK)
    b = jnp.transpose(b, (0, 1, 2, 4, 3, 5))
    return b.reshape(nv, SWA_KV_HEADS, SWA_PAIRS * SWA_BLOCK, 2 * SWA_BLOCK)


def _swa_bias_static(table):
    tab = table.astype(F32)
    blk = SWA_BLOCK
    i = jnp.arange(blk, dtype=jnp.int32)[:, None]
    j = jnp.arange(blk, dtype=jnp.int32)[None, :]
    m = jnp.arange(N_META, dtype=jnp.int32)[None, :]

    def lookup(dist):
        onehot = (_t5_bucket(dist)[..., None] == jnp.arange(REL_BUCKETS, dtype=jnp.int32)).astype(F32)
        return jnp.einsum('qsb,bh->hqs', onehot, tab, precision=lax.Precision.HIGHEST)

    neg = jnp.full((SWA_Q_HEADS, blk, blk), NEG_INF, F32)
    cur = jnp.where((i - j >= 0)[None], lookup(i - j), NEG_INF)
    d_prev = blk + i - j
    prev = jnp.where(((d_prev >= 0) & (d_prev < SWA_WINDOW))[None], lookup(d_prev), NEG_INF)
    meta0 = lookup(N_META + i - m)
    meta1 = lookup(N_META + blk + i - m)
    same_seq = (i // N_META) == (j // N_META)
    dm = (i % N_META) - (j % N_META)
    meta_tile = jnp.where((same_seq & (dm >= 0))[None], lookup(dm), NEG_INF)
    bias_p = _pair_layout(jnp.stack([neg, prev, neg]))
    bias_c = _pair_layout(jnp.stack([cur, cur, meta_tile]))
    return bias_p, bias_c, jnp.stack([meta0, meta1])


def _swa_bias_meta(meta01, sinks):
    nh, blk = SWA_Q_HEADS, SWA_BLOCK
    sink_col = jnp.broadcast_to(sinks.astype(F32)[None, :, None, None], (3, nh, blk, 1))
    meta = jnp.concatenate([meta01, jnp.full((1, nh, blk, N_META), NEG_INF, F32)], axis=0)
    rest = jnp.full((3, nh, blk, blk - N_META - 1), NEG_INF, F32)
    return _pair_layout(jnp.concatenate([meta, sink_col, rest], axis=-1))


def kernel(x, meta_tokens, rel_bias_table, ln_gain, ln_bias, gla_w_in, gla_w_gate2, gla_b_gate,
           gla_norm_gain, gla_w_out, kv_w_shared, swa_w_q, swa_sinks, swa_w_out,
           ffn_w_gate_up, ffn_w_down, moe_w_router, moe_w_gate_up, moe_w_down):
    bsz, seq, d = x.shape
    n_real = bsz * seq
    h = jnp.concatenate([x.reshape(n_real, d),
                         jnp.broadcast_to(meta_tokens.astype(x.dtype)[None], (bsz, N_META, d))
                         .reshape(bsz * N_META, d)], axis=0)
    rows = h.shape[0]
    hb = h.astype(BF16)
    bias_p, bias_c, meta01 = _swa_bias_static(rel_bias_table)

    w_in_b = gla_w_in.astype(BF16)
    w_gla_out_b = gla_w_out.astype(BF16)
    w_q_b = swa_w_q.astype(BF16)
    w_swa_out_b = swa_w_out.astype(BF16)
    w_ffn_gu_b = ffn_w_gate_up.astype(BF16)
    w_ffn_down_b = ffn_w_down.astype(BF16)
    hd = SWA_HEAD_DIM
    w_kv = kv_w_shared.reshape(d, 2 * SWA_KV_HEADS, 1, hd)
    w_kv2 = jnp.broadcast_to(w_kv, (d, 2 * SWA_KV_HEADS, 2, hd)).reshape(1, d, 4 * SWA_KV_HEADS * hd)
    w_kv2 = w_kv2.astype(BF16)

    kv2 = None
    for li in range(DEPTH):
        g0, b0 = ln_gain[li, 0][None, :], ln_bias[li, 0][None, :]
        g1, b1 = ln_gain[li, 1][None, :], ln_bias[li, 1][None, :]
        if li < N_A_LAYERS:
            wg_pad = jnp.pad(w_in_b[li, :, GLA_MAIN:], ((0, 0), (0, LANES - GLA_GATE_RANK)))
            w2_pad = jnp.pad(gla_w_gate2[li].astype(BF16), ((0, LANES - GLA_GATE_RANK), (0, 0)))
            qkvr, log_a = gla_in_proj(hb, w_in_b, li, wg_pad, w2_pad, gla_b_gate[li][None, :])
            mix_in = gla_mix(qkvr, log_a, gla_norm_gain[li][None, :], bsz, seq)
            h, hb = linear_res_ln(mix_in, w_gla_out_b, li, h, g0, b0)
        else:
            jb = li - N_A_LAYERS
            q = linear(hb, w_q_b, jb, d, BF16, scale=hd ** -0.5)
            bias_m = _swa_bias_meta(meta01, swa_sinks[jb])
            mix_in = swa_mix(q, kv2, bias_m, bias_p, bias_c, bsz, seq)
            h, hb = linear_res_ln(mix_in, w_swa_out_b, jb, h, g0, b0)
        if li % 2 == 0:
            h, hb = ffn_res_ln(hb, h, w_ffn_gu_b, w_ffn_down_b, li // 2, g1, b1)
        else:
            w_r = jnp.pad(moe_w_router[li // 2].astype(BF16), ((0, 0), (0, LANES - N_EXPERTS)))
            out_rows = n_real if li == DEPTH - 1 else rows
            h, hb = moe_res_ln(h, hb, w_r, moe_w_gate_up, moe_w_down, li // 2, g1, b1, out_rows)
        if li == N_A_LAYERS - 1:
            kv2 = linear(hb, w_kv2, 0, w_kv2.shape[-1], BF16)
    return h.reshape(bsz, seq, d)
```

```python
import functools
import math

import jax
import jax.numpy as jnp
from jax import lax
from jax.experimental import pallas as pl
from jax.experimental.pallas import tpu as pltpu

F32 = jnp.float32
BF16 = jnp.bfloat16

D_MODEL = 2048
DEPTH = 4
N_META = 16
N_A_LAYERS = DEPTH // 2
DN_ALPHA = (2 * DEPTH) ** 0.25
LN_EPS = 1e-5

GLA_HEADS = 4
GLA_DK = D_MODEL // 2
GLA_DV = D_MODEL
GLA_DK_HEAD = GLA_DK // GLA_HEADS
GLA_DV_HEAD = GLA_DV // GLA_HEADS
GLA_GATE_RANK = 16
GLA_GATE_TAU = 16.0
GLA_CHUNK = 64
GLA_MAIN = 2 * GLA_DK + 2 * GLA_DV

SWA_HEAD_DIM = 64
SWA_Q_HEADS = D_MODEL // SWA_HEAD_DIM
SWA_GROUP = 8
SWA_KV_HEADS = SWA_Q_HEADS // SWA_GROUP
SWA_PAIRS = SWA_GROUP // 2
SWA_STEP_BLOCKS = 2
SWA_WINDOW = 128
SWA_BLOCK = 128

REL_BUCKETS = 32
REL_MAX_DIST = 128

FFN_DIM = 7 * D_MODEL // 2
N_EXPERTS = 8
TOP_K = 2
NEG_INF = -1e9

LANES = 128
VMEM_LIMIT = 56 * 1024 * 1024
ROW_TILE = 512
LN_ROW_PARTS = 4
PROJ_ROW_TILE = 1376
PROJ_COL_TILE = 1024
FFN_TILE = 512
FFN_ROW_TILE = 688
MOE_ROW_TILE = 1024
MOE_GATHER_ROWS = 76
MOE_VMEM_LIMIT = 62 * 1024 * 1024
GLA_ROWS = 1024
GLA_SUB_ROWS = 256


def _params(*sem):
    return pltpu.CompilerParams(dimension_semantics=sem, vmem_limit_bytes=VMEM_LIMIT)


def _layer_norm_rows(y, g, b):
    mu = jnp.mean(y, axis=-1, keepdims=True)
    yc = y - mu
    var = jnp.mean(yc * yc, axis=-1, keepdims=True)
    return yc * lax.rsqrt(var + LN_EPS) * g + b


def _dot(a, b):
    return jnp.dot(a, b, preferred_element_type=F32)


def _dot_nt(a, b):
    return lax.dot_general(a, b, (((1,), (1,)), ((), ())), preferred_element_type=F32)


def _dot_tn(a, b):
    return lax.dot_general(a, b, (((0,), (0,)), ((), ())), preferred_element_type=F32)


def _linear_kernel(x_ref, w_ref, o_ref, *, scale):
    acc = _dot(x_ref[...], w_ref[...])
    if scale != 1.0:
        acc = acc * scale
    o_ref[...] = acc.astype(o_ref.dtype)


def linear(x, w, layer, n_out, out_dtype, scale=1.0):
    rows, k = x.shape
    tm = PROJ_ROW_TILE if rows % PROJ_ROW_TILE == 0 else ROW_TILE
    tn = PROJ_COL_TILE
    return pl.pallas_call(
        functools.partial(_linear_kernel, scale=scale),
        grid=(pl.cdiv(rows, tm), n_out // tn),
        in_specs=[pl.BlockSpec((tm, k), lambda i, j: (i, 0)),
                  pl.BlockSpec((None, k, tn), lambda i, j: (layer, 0, j))],
        out_specs=pl.BlockSpec((tm, tn), lambda i, j: (i, j)),
        out_shape=jax.ShapeDtypeStruct((rows, n_out), out_dtype),
        compiler_params=_params("parallel", "arbitrary"),
        name="linear",
    )(x, w)


def _gla_in_proj_kernel(x_ref, w_ref, wg_ref, w2_ref, b_ref, o_ref, la_ref):
    x = x_ref[...]
    o_ref[...] = _dot(x, w_ref[...])

    @pl.when(pl.program_id(1) == 0)
    def _():
        g_low = _dot(x, wg_ref[...])
        z = _dot(g_low.astype(BF16), w2_ref[...]) + b_ref[...]
        log_sig = jnp.minimum(z, 0.0) - jnp.log1p(jnp.exp(-jnp.abs(z)))
        la_ref[...] = log_sig / GLA_GATE_TAU


def gla_in_proj(x, w, layer, wg_pad, w2_pad, b_gate):
    rows, k = x.shape
    tm = PROJ_ROW_TILE if rows % PROJ_ROW_TILE == 0 else ROW_TILE
    tn = PROJ_COL_TILE
    return pl.pallas_call(
        _gla_in_proj_kernel,
        grid=(pl.cdiv(rows, tm), GLA_MAIN // tn),
        in_specs=[pl.BlockSpec((tm, k), lambda i, j: (i, 0)),
                  pl.BlockSpec((None, k, tn), lambda i, j: (layer, 0, j)),
                  pl.BlockSpec((k, LANES), lambda i, j: (0, 0)),
                  pl.BlockSpec((LANES, GLA_DK), lambda i, j: (0, 0)),
                  pl.BlockSpec((1, GLA_DK), lambda i, j: (0, 0))],
        out_specs=[pl.BlockSpec((tm, tn), lambda i, j: (i, j)),
                   pl.BlockSpec((tm, GLA_DK), lambda i, j: (i, 0))],
        out_shape=[jax.ShapeDtypeStruct((rows, GLA_MAIN), F32),
                   jax.ShapeDtypeStruct((rows, GLA_DK), F32)],
        compiler_params=_params("parallel", "arbitrary"),
        name="gla_in_proj",
    )(x, w, wg_pad, w2_pad, b_gate)


def _linear_res_ln_kernel(x_ref, w_ref, res_ref, g_ref, b_ref, o_ref, ob_ref):
    part = x_ref.shape[0] // LN_ROW_PARTS
    parts = [slice(k * part, (k + 1) * part) for k in range(LN_ROW_PARTS)]
    ys = [DN_ALPHA * res_ref[sl, :] + _dot(x_ref[sl, :], w_ref[...]) for sl in parts]
    for sl, y in zip(parts, ys):
        o = _layer_norm_rows(y, g_ref[...], b_ref[...])
        o_ref[sl, :] = o
        ob_ref[sl, :] = o.astype(BF16)


def linear_res_ln(x, w, layer, res, g, b):
    rows, k = x.shape
    d = w.shape[-1]
    return pl.pallas_call(
        _linear_res_ln_kernel,
        grid=(pl.cdiv(rows, ROW_TILE),),
        in_specs=[pl.BlockSpec((ROW_TILE, k), lambda i: (i, 0)),
                  pl.BlockSpec((None, k, d), lambda i: (layer, 0, 0)),
                  pl.BlockSpec((ROW_TILE, d), lambda i: (i, 0)),
                  pl.BlockSpec((1, d), lambda i: (0, 0)),
                  pl.BlockSpec((1, d), lambda i: (0, 0))],
        out_specs=[pl.BlockSpec((ROW_TILE, d), lambda i: (i, 0)),
                   pl.BlockSpec((ROW_TILE, d), lambda i: (i, 0))],
        out_shape=[jax.ShapeDtypeStruct((rows, d), F32), jax.ShapeDtypeStruct((rows, d), BF16)],
        compiler_params=_params("parallel"),
        name="linear_res_ln",
    )(x, w, res, g, b)


def _swiglu_partial(x, wa, wu, wd):
    a = _dot(x, wa)
    u = _dot(x, wu)
    hidden = (a * jax.nn.sigmoid(a) * u).astype(BF16)
    return _dot(hidden, wd)


def _ffn_kernel(x_ref, wa_ref, wu_ref, wd_ref, res_ref, g_ref, b_ref, o_ref, ob_ref):
    j = pl.program_id(1)

    @pl.when(j == 0)
    def _():
        o_ref[...] = jnp.zeros_like(o_ref)

    o_ref[...] += _swiglu_partial(x_ref[...], wa_ref[...], wu_ref[...], wd_ref[...])

    @pl.when(j == pl.num_programs(1) - 1)
    def _():
        o = _layer_norm_rows(DN_ALPHA * res_ref[...] + o_ref[...], g_ref[...], b_ref[...])
        o_ref[...] = o
        ob_ref[...] = o.astype(BF16)


def ffn_res_ln(xb, res, w_gu, w_down, layer, g, b):
    rows, d = xb.shape
    f = w_down.shape[-2]
    nj = f // FFN_TILE
    tm = FFN_ROW_TILE if rows % FFN_ROW_TILE == 0 else ROW_TILE
    return pl.pallas_call(
        _ffn_kernel,
        grid=(pl.cdiv(rows, tm), nj),
        in_specs=[pl.BlockSpec((tm, d), lambda i, j: (i, 0)),
                  pl.BlockSpec((None, d, FFN_TILE), lambda i, j: (layer, 0, j)),
                  pl.BlockSpec((None, d, FFN_TILE), lambda i, j: (layer, 0, nj + j)),
                  pl.BlockSpec((None, FFN_TILE, d), lambda i, j: (layer, j, 0)),
                  pl.BlockSpec((tm, d), lambda i, j: (i, 0)),
                  pl.BlockSpec((1, d), lambda i, j: (0, 0)),
                  pl.BlockSpec((1, d), lambda i, j: (0, 0))],
        out_specs=[pl.BlockSpec((tm, d), lambda i, j: (i, 0)),
                   pl.BlockSpec((tm, d), lambda i, j: (i, 0))],
        out_shape=[jax.ShapeDtypeStruct((rows, d), F32), jax.ShapeDtypeStruct((rows, d), BF16)],
        compiler_params=_params("parallel", "arbitrary"),
        name="ffn_res_ln",
    )(xb, w_gu, w_gu, w_down, res, g, b)


def _router_kernel(x_ref, w_ref, o_ref):
    logits = _dot(x_ref[...], w_ref[...])
    lane = lax.broadcasted_iota(jnp.int32, logits.shape, 1)
    lane_f = lane.astype(F32)
    logits = jnp.where(lane < N_EXPERTS, logits, -jnp.inf)
    m1 = jnp.max(logits, axis=-1, keepdims=True)
    i1 = jnp.min(jnp.where(logits == m1, lane_f, float(LANES)), axis=-1, keepdims=True)
    rest = jnp.where(lane_f == i1, -jnp.inf, logits)
    m2 = jnp.max(rest, axis=-1, keepdims=True)
    i2 = jnp.min(jnp.where(rest == m2, lane_f, float(LANES)), axis=-1, keepdims=True)
    e2 = jnp.exp(m2 - m1)
    g1 = 1.0 / (1.0 + e2)
    g2 = e2 / (1.0 + e2)
    o_ref[...] = jnp.where(lane == 0, i1, jnp.where(lane == 1, i2,
                           jnp.where(lane == 2, g1, jnp.where(lane == 3, g2, 0.0))))


def router(xb, w_router_pad):
    rows, d = xb.shape
    return pl.pallas_call(
        _router_kernel,
        grid=(pl.cdiv(rows, ROW_TILE),),
        in_specs=[pl.BlockSpec((ROW_TILE, d), lambda i: (i, 0)),
                  pl.BlockSpec((d, LANES), lambda i: (0, 0))],
        out_specs=pl.BlockSpec((ROW_TILE, LANES), lambda i: (i, 0)),
        out_shape=jax.ShapeDtypeStruct((rows, LANES), F32),
        compiler_params=_params("parallel"),
        name="router",
    )(xb, w_router_pad)


def _moe_kernel(blk_e_ref, n_half_ref, tok_ref, h_hbm, wa_ref, wu_ref, wd_ref, o_ref,
                land_ref, x_ref, sem):
    del blk_e_ref
    i = pl.program_id(0)
    j = pl.program_id(1)
    n_blk = pl.num_programs(0)
    nj = pl.num_programs(1)
    n_half = n_half_ref[i]
    half = MOE_ROW_TILE // 2
    land_rows = land_ref.shape[0]

    def row_copy(slot, dst_row):
        return pltpu.make_async_copy(h_hbm.at[pl.ds(tok_ref[slot], 1), :],
                                     land_ref.at[pl.ds(dst_row, 1), :], sem)

    def wait_landing():
        pltpu.make_async_copy(h_hbm.at[pl.ds(0, land_rows), :], land_ref, sem).wait()

    @pl.when(j == 0)
    def _():
        o_ref[...] = jnp.zeros_like(o_ref)

        @pl.when(i == 0)
        def _():
            def issue(r, carry):
                row_copy(r, r).start()
                return carry
            lax.fori_loop(0, land_rows, issue, 0, unroll=8)

        @pl.when(jnp.logical_or(i == 0, n_half_ref[jnp.maximum(i - 1, 0)] > 0))
        def _():
            wait_landing()

        for hf in range(2):
            @pl.when(n_half > hf)
            def _():
                sl = slice(hf * half, (hf + 1) * half)
                x_ref[sl, :] = land_ref[sl, :].astype(BF16)

    @pl.when(n_half > 0)
    def _():
        next_base = jnp.minimum(i + 1, n_blk - 1) * MOE_ROW_TILE
        r0 = j * MOE_GATHER_ROWS
        for k in range(MOE_GATHER_ROWS):
            row_copy(next_base + r0 + k, r0 + k).start()
        wa = wa_ref[...].astype(BF16)
        wu = wu_ref[...].astype(BF16)
        wd = wd_ref[...].astype(BF16)
        o_ref[:half, :] += _swiglu_partial(x_ref[:half, :], wa, wu, wd)

        @pl.when(n_half > 1)
        def _():
            o_ref[half:, :] += _swiglu_partial(x_ref[half:, :], wa, wu, wd)

    @pl.when(jnp.logical_and(jnp.logical_and(i == n_blk - 1, j == nj - 1), n_half > 0))
    def _():
        wait_landing()


def moe_experts(h, buf_tok, blk_e, n_half, w_gu, w_down, layer):
    d = h.shape[1]
    p_rows = buf_tok.shape[0]
    f = w_down.shape[-2]
    nj = f // FFN_TILE
    n_blk = p_rows // MOE_ROW_TILE
    assert nj * MOE_GATHER_ROWS >= MOE_ROW_TILE
    buf_tok = jnp.pad(buf_tok, (0, nj * MOE_GATHER_ROWS - MOE_ROW_TILE))

    def jj(i, j, nh):
        return jnp.where(nh[i] > 0, j, nj - 1)

    grid_spec = pltpu.PrefetchScalarGridSpec(
        num_scalar_prefetch=3,
        grid=(n_blk, nj),
        in_specs=[pl.BlockSpec(memory_space=pl.ANY),
                  pl.BlockSpec((None, None, d, FFN_TILE),
                               lambda i, j, be, nh, tk: (layer, be[i], 0, jj(i, j, nh))),
                  pl.BlockSpec((None, None, d, FFN_TILE),
                               lambda i, j, be, nh, tk: (layer, be[i], 0, nj + jj(i, j, nh))),
                  pl.BlockSpec((None, None, FFN_TILE, d),
                               lambda i, j, be, nh, tk: (layer, be[i], jj(i, j, nh), 0))],
        out_specs=pl.BlockSpec((MOE_ROW_TILE, d), lambda i, j, be, nh, tk: (i, 0)),
        scratch_shapes=[pltpu.VMEM((nj * MOE_GATHER_ROWS, d), F32),
                        pltpu.VMEM((MOE_ROW_TILE, d), BF16),
                        pltpu.SemaphoreType.DMA(())],
    )
    return pl.pallas_call(
        _moe_kernel,
        grid_spec=grid_spec,
        out_shape=jax.ShapeDtypeStruct((p_rows, d), F32),
        compiler_params=pltpu.CompilerParams(dimension_semantics=("arbitrary", "arbitrary"),
                                             vmem_limit_bytes=MOE_VMEM_LIMIT),
        name="moe_experts",
    )(blk_e, n_half, buf_tok, h, w_gu, w_gu, w_down)


def _combine_ln_kernel(res_ref, y0_ref, y1_ref, gate_ref, g_ref, b_ref, o_ref, ob_ref):
    gate = gate_ref[...]
    moe = y0_ref[...] * gate[:, 2:3] + y1_ref[...] * gate[:, 3:4]
    o = _layer_norm_rows(DN_ALPHA * res_ref[...] + moe, g_ref[...], b_ref[...])
    o_ref[...] = o
    ob_ref[...] = o.astype(BF16)


def combine_ln(res, y0, y1, route, g, b, out_rows):
    d = res.shape[1]
    row_spec = pl.BlockSpec((ROW_TILE, d), lambda i: (i, 0))
    vec_spec = pl.BlockSpec((1, d), lambda i: (0, 0))
    return pl.pallas_call(
        _combine_ln_kernel,
        grid=(pl.cdiv(out_rows, ROW_TILE),),
        in_specs=[row_spec, row_spec, row_spec, pl.BlockSpec((ROW_TILE, LANES), lambda i: (i, 0)),
                  vec_spec, vec_spec],
        out_specs=[row_spec, row_spec],
        out_shape=[jax.ShapeDtypeStruct((out_rows, d), F32),
                   jax.ShapeDtypeStruct((out_rows, d), BF16)],
        compiler_params=_params("parallel"),
        name="combine_ln",
    )(res, y0, y1, route, g, b)


def moe_res_ln(h, hb, w_router_pad, w_gu, w_down, layer, g, b, out_rows):
    rows, d = h.shape
    route = router(hb, w_router_pad)
    flat_e = route[:, :TOP_K].astype(jnp.int32).reshape(-1)
    n_flat = rows * TOP_K
    n_blk = -(-n_flat // MOE_ROW_TILE) + N_EXPERTS
    p_rows = n_blk * MOE_ROW_TILE
    onehot = (flat_e[:, None] == jnp.arange(N_EXPERTS, dtype=jnp.int32)[None, :]).astype(jnp.int32)
    csum = jnp.cumsum(onehot, axis=0)
    rank = jnp.sum(csum * onehot, axis=1) - 1
    counts = csum[-1]
    padded = ((counts + MOE_ROW_TILE - 1) // MOE_ROW_TILE) * MOE_ROW_TILE
    padded_end = jnp.cumsum(padded)
    start_padded = padded_end - padded
    pos = start_padded[flat_e] + rank
    flat_tok = jnp.arange(n_flat, dtype=jnp.int32) // TOP_K
    buf_tok = jnp.zeros((p_rows,), jnp.int32).at[pos].set(flat_tok)
    blk_start = jnp.arange(n_blk, dtype=jnp.int32) * MOE_ROW_TILE
    blk_e = jnp.minimum(jnp.searchsorted(padded_end, blk_start, side='right'),
                        N_EXPERTS - 1).astype(jnp.int32)
    valid = jnp.clip(counts[blk_e] - (blk_start - start_padded[blk_e]), 0, MOE_ROW_TILE)
    valid = jnp.where(blk_start < padded_end[-1], valid, 0)
    half = MOE_ROW_TILE // 2
    n_half = ((valid + half - 1) // half).astype(jnp.int32)
    y = moe_experts(h, buf_tok, blk_e, n_half, w_gu, w_down, layer)
    pos2 = pos.reshape(rows, TOP_K)
    y0 = y.at[pos2[:, 0]].get(mode="promise_in_bounds")
    y1 = y.at[pos2[:, 1]].get(mode="promise_in_bounds")
    return combine_ln(h, y0, y1, route, g, b, out_rows)


def _head_norm_gate(o, r, gain):
    mu = jnp.mean(o, axis=-1, keepdims=True)
    oc = o - mu
    var = jnp.mean(oc * oc, axis=-1, keepdims=True)
    on = oc * lax.rsqrt(var + LN_EPS) * gain
    return (on * (r * jax.nn.sigmoid(r))).astype(BF16)


def _masked_cumsum(mask, x):
    m = mask.astype(BF16)
    hi = x.astype(BF16)
    r1 = x - hi.astype(F32)
    mid = r1.astype(BF16)
    lo = (r1 - mid.astype(F32)).astype(BF16)
    return _dot(m, hi) + _dot(m, mid) + _dot(m, lo)


def _gla_first_chunk(q, k, v, r, la, gain):
    c = q.shape[0]
    row = lax.broadcasted_iota(jnp.int32, (c, c), 0)
    col = lax.broadcasted_iota(jnp.int32, (c, c), 1)
    causal = row >= col
    bcum = _masked_cumsum(causal, la)
    q_dec = (q * (GLA_DK_HEAD ** -0.5) * jnp.exp(bcum)).astype(BF16)
    k_inv = (k * jnp.exp(-bcum)).astype(BF16)
    vb = v.astype(BF16)
    att = jnp.where(causal, _dot_nt(q_dec, k_inv), 0.0)
    o = _dot(att.astype(BF16), vb)
    k_state = (k * jnp.exp(bcum[c - 1:c, :] - bcum)).astype(BF16)
    return _head_norm_gate(o, r, gain), _dot_tn(k_state, vb)


def _gla_intra(q, k, v, la):
    n = q.shape[0]
    c = GLA_CHUNK
    row = lax.broadcasted_iota(jnp.int32, (n, n), 0)
    col = lax.broadcasted_iota(jnp.int32, (n, n), 1)
    causal = (row >= col) & ((row // c) == (col // c))
    bcum = _masked_cumsum(causal, la)
    q_dec = (q * (GLA_DK_HEAD ** -0.5) * jnp.exp(bcum)).astype(BF16)
    k_inv = (k * jnp.exp(-bcum)).astype(BF16)
    vb = v.astype(BF16)
    att = jnp.where(causal, _dot_nt(q_dec, k_inv), 0.0)
    o_intra = _dot(att.astype(BF16), vb)
    b_last = [bcum[(ci + 1) * c - 1:(ci + 1) * c, :] for ci in range(n // c)]
    b_tot = jnp.concatenate([jnp.broadcast_to(bl, (c, bl.shape[1])) for bl in b_last], axis=0)
    k_state = (k * jnp.exp(b_tot - bcum)).astype(BF16)
    return q_dec, k_state, vb, o_intra, b_last


def _gla_walk(state, q_dec, k_state, vb, b_last):
    c = GLA_CHUNK
    o_inter = []
    for ci, bl in enumerate(b_last):
        sl = slice(ci * c, (ci + 1) * c)
        o_inter.append(_dot(q_dec[sl], state.astype(BF16)))
        decay = jnp.exp(jnp.transpose(jnp.broadcast_to(bl, (LANES, bl.shape[1]))))
        decay = jnp.concatenate([decay] * (GLA_DV_HEAD // LANES), axis=1)
        state = state * decay + _dot_tn(k_state[sl], vb[sl])
    return jnp.concatenate(o_inter, axis=0), state


def _gla_kernel(qm_ref, km_ref, vm_ref, rm_ref, lam_ref, q_ref, k_ref, v_ref, r_ref, la_ref,
                gain_ref, o_ref, state_ref, *, blocks_per_seq):
    s = pl.program_id(1)
    gain = gain_ref[...]
    n_batch = state_ref.shape[0]

    @pl.when(s == 0)
    def _():
        for bi in range(n_batch):
            sl = slice(bi * N_META, (bi + 1) * N_META)
            og, st = _gla_first_chunk(qm_ref[sl, :], km_ref[sl, :], vm_ref[sl, :], rm_ref[sl, :],
                                      lam_ref[sl, :], gain)
            state_ref[bi] = st
            o_ref[sl, :] = og

    @pl.when(s > 0)
    def _():
        bi = (s - 1) // blocks_per_seq
        subs = [slice(r0, r0 + GLA_SUB_ROWS) for r0 in range(0, q_ref.shape[0], GLA_SUB_ROWS)]
        intra = [_gla_intra(q_ref[sl, :], k_ref[sl, :], v_ref[sl, :], la_ref[sl, :]) for sl in subs]
        st = state_ref[bi]
        for sl, (q_dec, k_state, vb, o_intra, b_last) in zip(subs, intra):
            o_inter, st = _gla_walk(st, q_dec, k_state, vb, b_last)
            o_ref[sl, :] = _head_norm_gate(o_intra + o_inter, r_ref[sl, :], gain)
        state_ref[bi] = st


def gla_mix(qkvr, log_a, norm_gain, n_batch, seq):
    rows = qkvr.shape[0]
    n_real = n_batch * seq
    meta_rows = n_batch * N_META
    blocks_per_seq = seq // GLA_ROWS
    n_steps = 1 + n_real // GLA_ROWS
    meta_blk = n_real // meta_rows
    meta_out_blk = n_real // GLA_ROWS
    dk, dv, nh = GLA_DK_HEAD, GLA_DV_HEAD, GLA_HEADS
    kq, kk, kv_, kr = 0, GLA_DK // dk, 2 * GLA_DK // dv, (2 * GLA_DK + GLA_DV) // dv

    def real_blk(s):
        return jnp.maximum(s - 1, 0)

    def meta_spec(width, col0):
        return pl.BlockSpec((meta_rows, width), lambda h, s: (meta_blk, col0 + h))

    def real_spec(width, col0):
        return pl.BlockSpec((GLA_ROWS, width), lambda h, s: (real_blk(s), col0 + h))

    return pl.pallas_call(
        functools.partial(_gla_kernel, blocks_per_seq=blocks_per_seq),
        grid=(nh, n_steps),
        in_specs=[meta_spec(dk, kq), meta_spec(dk, kk), meta_spec(dv, kv_), meta_spec(dv, kr),
                  meta_spec(dk, 0),
                  real_spec(dk, kq), real_spec(dk, kk), real_spec(dv, kv_), real_spec(dv, kr),
                  real_spec(dk, 0),
                  pl.BlockSpec((1, dv), lambda h, s: (0, h))],
        out_specs=pl.BlockSpec((GLA_ROWS, dv),
                               lambda h, s: (jnp.where(s == 0, meta_out_blk, s - 1), h)),
        out_shape=jax.ShapeDtypeStruct((rows, GLA_DV), BF16),
        scratch_shapes=[pltpu.VMEM((n_batch, dk, dv), F32)],
        compiler_params=_params("parallel", "arbitrary"),
        name="gla_mix",
    )(qkvr, qkvr, qkvr, qkvr, log_a, qkvr, qkvr, qkvr, qkvr, log_a, norm_gain)


def _block_diag2(x2):
    lane = lax.broadcasted_iota(jnp.int32, x2.shape, 1)
    zero = jnp.zeros_like(x2)
    return jnp.concatenate([jnp.where(lane < SWA_HEAD_DIM, x2, zero),
                            jnp.where(lane >= SWA_HEAD_DIM, x2, zero)], axis=0)


def _swa_kernel(q_ref, km_ref, kp_ref, kc_ref, vm_ref, vp_ref, vc_ref,
                bm_ref, bp_ref, bc_ref, bm1_ref, bp1_ref, bc1_ref, o_ref, *, blocks_per_seq):
    s = pl.program_id(1)
    bi = (SWA_STEP_BLOCKS * jnp.maximum(s - 1, 0)) // blocks_per_seq
    m0 = pl.multiple_of(bi * N_META, N_META)
    pad = jnp.zeros((SWA_BLOCK - N_META, LANES), BF16)
    blk_a, blk_b = slice(0, SWA_BLOCK), slice(SWA_BLOCK, 2 * SWA_BLOCK)
    k_m = _block_diag2(jnp.concatenate([km_ref[pl.ds(m0, N_META), :], pad], axis=0))
    v_m = _block_diag2(jnp.concatenate([vm_ref[pl.ds(m0, N_META), :], pad], axis=0))
    k_p, v_p = _block_diag2(kp_ref[...]), _block_diag2(vp_ref[...])
    k_a, v_a = _block_diag2(kc_ref[blk_a, :]), _block_diag2(vc_ref[blk_a, :])
    k_b, v_b = _block_diag2(kc_ref[blk_b, :]), _block_diag2(vc_ref[blk_b, :])
    blocks = [(blk_a, (k_m, k_p, k_a), (v_m, v_p, v_a), (bm_ref, bp_ref, bc_ref)),
              (blk_b, (k_m, k_a, k_b), (v_m, v_a, v_b), (bm1_ref, bp1_ref, bc1_ref))]
    groups = [(qrows, p, ks, vs, bs) for qrows, ks, vs, bs in blocks for p in range(SWA_PAIRS)]
    scores = []
    for qrows, p, ks, _, bs in groups:
        qp = q_ref[qrows, p * LANES:(p + 1) * LANES]
        brows = slice(p * SWA_BLOCK, (p + 1) * SWA_BLOCK)
        scores.append([_dot_nt(qp, ks[t]) + bs[t][0, 0, brows, :] for t in range(3)])
    for (qrows, p, _, vs, _), sc in zip(groups, scores):
        top = jnp.maximum(jnp.maximum(sc[0], sc[1]), sc[2])
        m_even = jnp.max(top[:, :LANES], axis=-1, keepdims=True)
        m_odd = jnp.max(top[:, LANES:], axis=-1, keepdims=True)
        lane2 = lax.broadcasted_iota(jnp.int32, top.shape, 1)
        m_full = jnp.where(lane2 < LANES, m_even, m_odd)
        pr = [jnp.exp(sc[t] - m_full) for t in range(3)]
        p_sum = pr[0] + pr[1] + pr[2]
        l_even = jnp.sum(p_sum[:, :LANES], axis=-1, keepdims=True)
        l_odd = jnp.sum(p_sum[:, LANES:], axis=-1, keepdims=True)
        o = (_dot(pr[0].astype(BF16), vs[0]) + _dot(pr[1].astype(BF16), vs[1])
             + _dot(pr[2].astype(BF16), vs[2]))
        lane1 = lax.broadcasted_iota(jnp.int32, o.shape, 1)
        o = o / jnp.where(lane1 < SWA_HEAD_DIM, l_even, l_odd)
        o_ref[qrows, p * LANES:(p + 1) * LANES] = o.astype(o_ref.dtype)


def swa_mix(q, kv2, bias_m, bias_p, bias_c, n_batch, seq):
    rows = q.shape[0]
    nb = SWA_STEP_BLOCKS
    blocks_per_seq = seq // SWA_BLOCK
    assert blocks_per_seq % nb == 0 and n_batch * N_META == SWA_BLOCK
    n_real_blk = n_batch * blocks_per_seq
    n_steps = 1 + n_real_blk // nb
    meta_blk = n_real_blk
    kvn, qw = SWA_KV_HEADS, SWA_GROUP * SWA_HEAD_DIM

    def cur_step_blk(s):
        return jnp.where(s == 0, meta_blk // nb, s - 1)

    def prev_blk(s):
        return jnp.maximum(nb * (s - 1) - 1, 0)

    def variant(s):
        return jnp.where(s == 0, 2, jnp.where((nb * (s - 1)) % blocks_per_seq == 0, 0, 1))

    def kv_spec(n_blocks, head0, blk_fn):
        return pl.BlockSpec((n_blocks * SWA_BLOCK, LANES), lambda kh, s: (blk_fn(s), head0 + kh))

    bias_shape = (1, 1, SWA_PAIRS * SWA_BLOCK, 2 * SWA_BLOCK)
    bias_spec = pl.BlockSpec(bias_shape, lambda kh, s: (variant(s), kh, 0, 0))
    bias1_spec = pl.BlockSpec(bias_shape, lambda kh, s: (1, kh, 0, 0))
    return pl.pallas_call(
        functools.partial(_swa_kernel, blocks_per_seq=blocks_per_seq),
        grid=(kvn, n_steps),
        in_specs=[pl.BlockSpec((nb * SWA_BLOCK, qw), lambda kh, s: (cur_step_blk(s), kh)),
                  kv_spec(1, 0, lambda s: meta_blk), kv_spec(1, 0, prev_blk),
                  kv_spec(nb, 0, cur_step_blk),
                  kv_spec(1, kvn, lambda s: meta_blk), kv_spec(1, kvn, prev_blk),
                  kv_spec(nb, kvn, cur_step_blk),
                  bias_spec, bias_spec, bias_spec, bias1_spec, bias1_spec, bias1_spec],
        out_specs=pl.BlockSpec((nb * SWA_BLOCK, qw), lambda kh, s: (cur_step_blk(s), kh)),
        out_shape=jax.ShapeDtypeStruct((rows, kvn * qw), BF16),
        compiler_params=_params("parallel", "arbitrary"),
        name="swa_mix",
    )(q, kv2, kv2, kv2, kv2, kv2, kv2, bias_m, bias_p, bias_c, bias_m, bias_p, bias_c)


def _t5_bucket(dist):
    exact = REL_BUCKETS // 2
    d = jnp.maximum(dist, 0)
    df = jnp.maximum(d, 1).astype(F32)
    large = exact + (jnp.log(df / exact) / math.log(REL_MAX_DIST / exact)
                     * (REL_BUCKETS - exact)).astype(jnp.int32)
    large = jnp.minimum(large, REL_BUCKETS - 1)
    return jnp.where(d < exact, d, large)


def _pair_layout(bias):
    nv = bias.shape[0]
    b = bias.reshape(nv, SWA_KV_HEADS, SWA_PAIRS, 2, SWA_BLOCK, SWA_BLOCK)
    b = jnp.transpose(b, (0, 1, 2, 4, 3, 5))
    return b.reshape(nv, SWA_KV_HEADS, SWA_PAIRS * SWA_BLOCK, 2 * SWA_BLOCK)


def _swa_bias_static(table):
    tab = table.astype(F32)
    blk = SWA_BLOCK
    i = jnp.arange(blk, dtype=jnp.int32)[:, None]
    j = jnp.arange(blk, dtype=jnp.int32)[None, :]
    m = jnp.arange(N_META, dtype=jnp.int32)[None, :]

    def lookup(dist):
        onehot = (_t5_bucket(dist)[..., None] == jnp.arange(REL_BUCKETS, dtype=jnp.int32)).astype(F32)
        return jnp.einsum('qsb,bh->hqs', onehot, tab, precision=lax.Precision.HIGHEST)

    neg = jnp.full((SWA_Q_HEADS, blk, blk), NEG_INF, F32)
    cur = jnp.where((i - j >= 0)[None], lookup(i - j), NEG_INF)
    d_prev = blk + i - j
    prev = jnp.where(((d_prev >= 0) & (d_prev < SWA_WINDOW))[None], lookup(d_prev), NEG_INF)
    meta0 = lookup(N_META + i - m)
    meta1 = lookup(N_META + blk + i - m)
    same_seq = (i // N_META) == (j // N_META)
    dm = (i % N_META) - (j % N_META)
    meta_tile = jnp.where((same_seq & (dm >= 0))[None], lookup(dm), NEG_INF)
    bias_p = _pair_layout(jnp.stack([neg, prev, neg]))
    bias_c = _pair_layout(jnp.stack([cur, cur, meta_tile]))
    return bias_p, bias_c, jnp.stack([meta0, meta1])


def _swa_bias_meta(meta01, sinks):
    nh, blk = SWA_Q_HEADS, SWA_BLOCK
    sink_col = jnp.broadcast_to(sinks.astype(F32)[None, :, None, None], (3, nh, blk, 1))
    meta = jnp.concatenate([meta01, jnp.full((1, nh, blk, N_META), NEG_INF, F32)], axis=0)
    rest = jnp.full((3, nh, blk, blk - N_META - 1), NEG_INF, F32)
    return _pair_layout(jnp.concatenate([meta, sink_col, rest], axis=-1))


def kernel(x, meta_tokens, rel_bias_table, ln_gain, ln_bias, gla_w_in, gla_w_gate2, gla_b_gate,
           gla_norm_gain, gla_w_out, kv_w_shared, swa_w_q, swa_sinks, swa_w_out,
           ffn_w_gate_up, ffn_w_down, moe_w_router, moe_w_gate_up, moe_w_down):
    bsz, seq, d = x.shape
    n_real = bsz * seq
    h = jnp.concatenate([x.reshape(n_real, d),
                         jnp.broadcast_to(meta_tokens.astype(x.dtype)[None], (bsz, N_META, d))
                         .reshape(bsz * N_META, d)], axis=0)
    rows = h.shape[0]
    hb = h.astype(BF16)
    bias_p, bias_c, meta01 = _swa_bias_static(rel_bias_table)

    w_in_b = gla_w_in.astype(BF16)
    w_gla_out_b = gla_w_out.astype(BF16)
    w_q_b = swa_w_q.astype(BF16)
    w_swa_out_b = swa_w_out.astype(BF16)
    w_ffn_gu_b = ffn_w_gate_up.astype(BF16)
    w_ffn_down_b = ffn_w_down.astype(BF16)
    hd = SWA_HEAD_DIM
    w_kv = kv_w_shared.reshape(d, 2 * SWA_KV_HEADS, 1, hd)
    w_kv2 = jnp.broadcast_to(w_kv, (d, 2 * SWA_KV_HEADS, 2, hd)).reshape(1, d, 4 * SWA_KV_HEADS * hd)
    w_kv2 = w_kv2.astype(BF16)

    kv2 = None
    for li in range(DEPTH):
        g0, b0 = ln_gain[li, 0][None, :], ln_bias[li, 0][None, :]
        g1, b1 = ln_gain[li, 1][None, :], ln_bias[li, 1][None, :]
        if li < N_A_LAYERS:
            wg_pad = jnp.pad(w_in_b[li, :, GLA_MAIN:], ((0, 0), (0, LANES - GLA_GATE_RANK)))
            w2_pad = jnp.pad(gla_w_gate2[li].astype(BF16), ((0, LANES - GLA_GATE_RANK), (0, 0)))
            qkvr, log_a = gla_in_proj(hb, w_in_b, li, wg_pad, w2_pad, gla_b_gate[li][None, :])
            mix_in = gla_mix(qkvr, log_a, gla_norm_gain[li][None, :], bsz, seq)
            h, hb = linear_res_ln(mix_in, w_gla_out_b, li, h, g0, b0)
        else:
            jb = li - N_A_LAYERS
            q = linear(hb, w_q_b, jb, d, BF16, scale=hd ** -0.5)
            bias_m = _swa_bias_meta(meta01, swa_sinks[jb])
            mix_in = swa_mix(q, kv2, bias_m, bias_p, bias_c, bsz, seq)
            h, hb = linear_res_ln(mix_in, w_swa_out_b, jb, h, g0, b0)
        if li % 2 == 0:
            h, hb = ffn_res_ln(hb, h, w_ffn_gu_b, w_ffn_down_b, li // 2, g1, b1)
        else:
            w_r = jnp.pad(moe_w_router[li // 2].astype(BF16), ((0, 0), (0, LANES - N_EXPERTS)))
            out_rows = n_real if li == DEPTH - 1 else rows
            h, hb = moe_res_ln(h, hb, w_r, moe_w_gate_up, moe_w_down, li // 2, g1, b1, out_rows)
        if li == N_A_LAYERS - 1:
            kv2 = linear(hb, w_kv2, 0, w_kv2.shape[-1], BF16)
    return h.reshape(bsz, seq, d)
```

```python
import functools
import math

import jax
import jax.numpy as jnp
from jax import lax
from jax.experimental import pallas as pl
from jax.experimental.pallas import tpu as pltpu

F32 = jnp.float32
BF16 = jnp.bfloat16

D_MODEL = 2048
DEPTH = 4
N_META = 16
N_A_LAYERS = DEPTH // 2
DN_ALPHA = (2 * DEPTH) ** 0.25
LN_EPS = 1e-5

GLA_HEADS = 4
GLA_DK = D_MODEL // 2
GLA_DV = D_MODEL
GLA_DK_HEAD = GLA_DK // GLA_HEADS
GLA_DV_HEAD = GLA_DV // GLA_HEADS
GLA_GATE_RANK = 16
GLA_GATE_TAU = 16.0
GLA_CHUNK = 64
GLA_MAIN = 2 * GLA_DK + 2 * GLA_DV

SWA_HEAD_DIM = 64
SWA_Q_HEADS = D_MODEL // SWA_HEAD_DIM
SWA_GROUP = 8
SWA_KV_HEADS = SWA_Q_HEADS // SWA_GROUP
SWA_PAIRS = SWA_GROUP // 2
SWA_STEP_BLOCKS = 2
SWA_WINDOW = 128
SWA_BLOCK = 128

REL_BUCKETS = 32
REL_MAX_DIST = 128

FFN_DIM = 7 * D_MODEL // 2
N_EXPERTS = 8
TOP_K = 2
NEG_INF = -1e9

LANES = 128
VMEM_LIMIT = 56 * 1024 * 1024
ROW_TILE = 512
LN_ROW_PARTS = 4
PROJ_ROW_TILE = 1376
PROJ_COL_TILE = 1024
FFN_TILE = 512
FFN_ROW_TILE = 688
MOE_ROW_TILE = 1024
MOE_GATHER_ROWS = 76
MOE_VMEM_LIMIT = 62 * 1024 * 1024
GLA_ROWS = 1024
GLA_SUB_ROWS = 256


def _params(*sem):
    return pltpu.CompilerParams(dimension_semantics=sem, vmem_limit_bytes=VMEM_LIMIT)


def _layer_norm_rows(y, g, b):
    mu = jnp.mean(y, axis=-1, keepdims=True)
    yc = y - mu
    var = jnp.mean(yc * yc, axis=-1, keepdims=True)
    return yc * lax.rsqrt(var + LN_EPS) * g + b


def _dot(a, b):
    return jnp.dot(a, b, preferred_element_type=F32)


def _dot_nt(a, b):
    return lax.dot_general(a, b, (((1,), (1,)), ((), ())), preferred_element_type=F32)


def _dot_tn(a, b):
    return lax.dot_general(a, b, (((0,), (0,)), ((), ())), preferred_element_type=F32)


def _linear_kernel(x_ref, w_ref, o_ref, *, scale):
    acc = _dot(x_ref[...], w_ref[...])
    if scale != 1.0:
        acc = acc * scale
    o_ref[...] = acc.astype(o_ref.dtype)


def linear(x, w, layer, n_out, out_dtype, scale=1.0):
    rows, k = x.shape
    tm = PROJ_ROW_TILE if rows % PROJ_ROW_TILE == 0 else ROW_TILE
    tn = PROJ_COL_TILE
    return pl.pallas_call(
        functools.partial(_linear_kernel, scale=scale),
        grid=(pl.cdiv(rows, tm), n_out // tn),
        in_specs=[pl.BlockSpec((tm, k), lambda i, j: (i, 0)),
                  pl.BlockSpec((None, k, tn), lambda i, j: (layer, 0, j))],
        out_specs=pl.BlockSpec((tm, tn), lambda i, j: (i, j)),
        out_shape=jax.ShapeDtypeStruct((rows, n_out), out_dtype),
        compiler_params=_params("parallel", "arbitrary"),
        name="linear",
    )(x, w)


def _gla_in_proj_kernel(x_ref, w_ref, wg_ref, w2_ref, b_ref, o_ref, la_ref):
    x = x_ref[...]
    o_ref[...] = _dot(x, w_ref[...])

    @pl.when(pl.program_id(1) == 0)
    def _():
        g_low = _dot(x, wg_ref[...])
        z = _dot(g_low.astype(BF16), w2_ref[...]) + b_ref[...]
        log_sig = jnp.minimum(z, 0.0) - jnp.log1p(jnp.exp(-jnp.abs(z)))
        la_ref[...] = log_sig / GLA_GATE_TAU


def gla_in_proj(x, w, layer, wg_pad, w2_pad, b_gate):
    rows, k = x.shape
    tm = PROJ_ROW_TILE if rows % PROJ_ROW_TILE == 0 else ROW_TILE
    tn = PROJ_COL_TILE
    return pl.pallas_call(
        _gla_in_proj_kernel,
        grid=(pl.cdiv(rows, tm), GLA_MAIN // tn),
        in_specs=[pl.BlockSpec((tm, k), lambda i, j: (i, 0)),
                  pl.BlockSpec((None, k, tn), lambda i, j: (layer, 0, j)),
                  pl.BlockSpec((k, LANES), lambda i, j: (0, 0)),
                  pl.BlockSpec((LANES, GLA_DK), lambda i, j: (0, 0)),
                  pl.BlockSpec((1, GLA_DK), lambda i, j: (0, 0))],
        out_specs=[pl.BlockSpec((tm, tn), lambda i, j: (i, j)),
                   pl.BlockSpec((tm, GLA_DK), lambda i, j: (i, 0))],
        out_shape=[jax.ShapeDtypeStruct((rows, GLA_MAIN), F32),
                   jax.ShapeDtypeStruct((rows, GLA_DK), F32)],
        compiler_params=_params("parallel", "arbitrary"),
        name="gla_in_proj",
    )(x, w, wg_pad, w2_pad, b_gate)


def _linear_res_ln_kernel(x_ref, w_ref, res_ref, g_ref, b_ref, o_ref, ob_ref):
    part = x_ref.shape[0] // LN_ROW_PARTS
    parts = [slice(k * part, (k + 1) * part) for k in range(LN_ROW_PARTS)]
    ys = [DN_ALPHA * res_ref[sl, :] + _dot(x_ref[sl, :], w_ref[...]) for sl in parts]
    for sl, y in zip(parts, ys):
        o = _layer_norm_rows(y, g_ref[...], b_ref[...])
        o_ref[sl, :] = o
        ob_ref[sl, :] = o.astype(BF16)


def linear_res_ln(x, w, layer, res, g, b):
    rows, k = x.shape
    d = w.shape[-1]
    return pl.pallas_call(
        _linear_res_ln_kernel,
        grid=(pl.cdiv(rows, ROW_TILE),),
        in_specs=[pl.BlockSpec((ROW_TILE, k), lambda i: (i, 0)),
                  pl.BlockSpec((None, k, d), lambda i: (layer, 0, 0)),
                  pl.BlockSpec((ROW_TILE, d), lambda i: (i, 0)),
                  pl.BlockSpec((1, d), lambda i: (0, 0)),
                  pl.BlockSpec((1, d), lambda i: (0, 0))],
        out_specs=[pl.BlockSpec((ROW_TILE, d), lambda i: (i, 0)),
                   pl.BlockSpec((ROW_TILE, d), lambda i: (i, 0))],
        out_shape=[jax.ShapeDtypeStruct((rows, d), F32), jax.ShapeDtypeStruct((rows, d), BF16)],
        compiler_params=_params("parallel"),
        name="linear_res_ln",
    )(x, w, res, g, b)


def _swiglu_partial(x, wa, wu, wd):
    a = _dot(x, wa)
    u = _dot(x, wu)
    hidden = (a * jax.nn.sigmoid(a) * u).astype(BF16)
    return _dot(hidden, wd)


def _ffn_kernel(x_ref, wa_ref, wu_ref, wd_ref, res_ref, g_ref, b_ref, o_ref, ob_ref):
    j = pl.program_id(1)

    @pl.when(j == 0)
    def _():
        o_ref[...] = jnp.zeros_like(o_ref)

    o_ref[...] += _swiglu_partial(x_ref[...], wa_ref[...], wu_ref[...], wd_ref[...])

    @pl.when(j == pl.num_programs(1) - 1)
    def _():
        o = _layer_norm_rows(DN_ALPHA * res_ref[...] + o_ref[...], g_ref[...], b_ref[...])
        o_ref[...] = o
        ob_ref[...] = o.astype(BF16)


def ffn_res_ln(xb, res, w_gu, w_down, layer, g, b):
    rows, d = xb.shape
    f = w_down.shape[-2]
    nj = f // FFN_TILE
    tm = FFN_ROW_TILE if rows % FFN_ROW_TILE == 0 else ROW_TILE
    return pl.pallas_call(
        _ffn_kernel,
        grid=(pl.cdiv(rows, tm), nj),
        in_specs=[pl.BlockSpec((tm, d), lambda i, j: (i, 0)),
                  pl.BlockSpec((None, d, FFN_TILE), lambda i, j: (layer, 0, j)),
                  pl.BlockSpec((None, d, FFN_TILE), lambda i, j: (layer, 0, nj + j)),
                  pl.BlockSpec((None, FFN_TILE, d), lambda i, j: (layer, j, 0)),
                  pl.BlockSpec((tm, d), lambda i, j: (i, 0)),
                  pl.BlockSpec((1, d), lambda i, j: (0, 0)),
                  pl.BlockSpec((1, d), lambda i, j: (0, 0))],
        out_specs=[pl.BlockSpec((tm, d), lambda i, j: (i, 0)),
                   pl.BlockSpec((tm, d), lambda i, j: (i, 0))],
        out_shape=[jax.ShapeDtypeStruct((rows, d), F32), jax.ShapeDtypeStruct((rows, d), BF16)],
        compiler_params=_params("parallel", "arbitrary"),
        name="ffn_res_ln",
    )(xb, w_gu, w_gu, w_down, res, g, b)


def _router_kernel(x_ref, w_ref, o_ref):
    logits = _dot(x_ref[...], w_ref[...])
    lane = lax.broadcasted_iota(jnp.int32, logits.shape, 1)
    lane_f = lane.astype(F32)
    logits = jnp.where(lane < N_EXPERTS, logits, -jnp.inf)
    m1 = jnp.max(logits, axis=-1, keepdims=True)
    i1 = jnp.min(jnp.where(logits == m1, lane_f, float(LANES)), axis=-1, keepdims=True)
    rest = jnp.where(lane_f == i1, -jnp.inf, logits)
    m2 = jnp.max(rest, axis=-1, keepdims=True)
    i2 = jnp.min(jnp.where(rest == m2, lane_f, float(LANES)), axis=-1, keepdims=True)
    e2 = jnp.exp(m2 - m1)
    g1 = 1.0 / (1.0 + e2)
    g2 = e2 / (1.0 + e2)
    o_ref[...] = jnp.where(lane == 0, i1, jnp.where(lane == 1, i2,
                           jnp.where(lane == 2, g1, jnp.where(lane == 3, g2, 0.0))))


def router(xb, w_router_pad):
    rows, d = xb.shape
    return pl.pallas_call(
        _router_kernel,
        grid=(pl.cdiv(rows, ROW_TILE),),
        in_specs=[pl.BlockSpec((ROW_TILE, d), lambda i: (i, 0)),
                  pl.BlockSpec((d, LANES), lambda i: (0, 0))],
        out_specs=pl.BlockSpec((ROW_TILE, LANES), lambda i: (i, 0)),
        out_shape=jax.ShapeDtypeStruct((rows, LANES), F32),
        compiler_params=_params("parallel"),
        name="router",
    )(xb, w_router_pad)


def _moe_kernel(blk_e_ref, n_half_ref, tok_ref, h_hbm, wa_ref, wu_ref, wd_ref, o_ref,
                land_ref, x_ref, sem):
    del blk_e_ref
    i = pl.program_id(0)
    j = pl.program_id(1)
    n_blk = pl.num_programs(0)
    nj = pl.num_programs(1)
    n_half = n_half_ref[i]
    half = MOE_ROW_TILE // 2
    land_rows = land_ref.shape[0]

    def row_copy(slot, dst_row):
        return pltpu.make_async_copy(h_hbm.at[pl.ds(tok_ref[slot], 1), :],
                                     land_ref.at[pl.ds(dst_row, 1), :], sem)

    def wait_landing():
        pltpu.make_async_copy(h_hbm.at[pl.ds(0, land_rows), :], land_ref, sem).wait()

    @pl.when(j == 0)
    def _():
        o_ref[...] = jnp.zeros_like(o_ref)

        @pl.when(i == 0)
        def _():
            def issue(r, carry):
                row_copy(r, r).start()
                return carry
            lax.fori_loop(0, land_rows, issue, 0, unroll=8)

        @pl.when(jnp.logical_or(i == 0, n_half_ref[jnp.maximum(i - 1, 0)] > 0))
        def _():
            wait_landing()

        for hf in range(2):
            @pl.when(n_half > hf)
            def _():
                sl = slice(hf * half, (hf + 1) * half)
                x_ref[sl, :] = land_ref[sl, :].astype(BF16)

    @pl.when(n_half > 0)
    def _():
        next_base = jnp.minimum(i + 1, n_blk - 1) * MOE_ROW_TILE
        r0 = j * MOE_GATHER_ROWS
        for k in range(MOE_GATHER_ROWS):
            row_copy(next_base + r0 + k, r0 + k).start(priority=k % 2)
        wa = wa_ref[...].astype(BF16)
        wu = wu_ref[...].astype(BF16)
        wd = wd_ref[...].astype(BF16)
        o_ref[:half, :] += _swiglu_partial(x_ref[:half, :], wa, wu, wd)

        @pl.when(n_half > 1)
        def _():
            o_ref[half:, :] += _swiglu_partial(x_ref[half:, :], wa, wu, wd)

    @pl.when(jnp.logical_and(jnp.logical_and(i == n_blk - 1, j == nj - 1), n_half > 0))
    def _():
        wait_landing()


def moe_experts(h, buf_tok, blk_e, n_half, w_gu, w_down, layer):
    d = h.shape[1]
    p_rows = buf_tok.shape[0]
    f = w_down.shape[-2]
    nj = f // FFN_TILE
    n_blk = p_rows // MOE_ROW_TILE
    assert nj * MOE_GATHER_ROWS >= MOE_ROW_TILE
    buf_tok = jnp.pad(buf_tok, (0, nj * MOE_GATHER_ROWS - MOE_ROW_TILE))

    def jj(i, j, nh):
        return jnp.where(nh[i] > 0, j, nj - 1)

    grid_spec = pltpu.PrefetchScalarGridSpec(
        num_scalar_prefetch=3,
        grid=(n_blk, nj),
        in_specs=[pl.BlockSpec(memory_space=pl.ANY),
                  pl.BlockSpec((None, None, d, FFN_TILE),
                               lambda i, j, be, nh, tk: (layer, be[i], 0, jj(i, j, nh))),
                  pl.BlockSpec((None, None, d, FFN_TILE),
                               lambda i, j, be, nh, tk: (layer, be[i], 0, nj + jj(i, j, nh))),
                  pl.BlockSpec((None, None, FFN_TILE, d),
                               lambda i, j, be, nh, tk: (layer, be[i], jj(i, j, nh), 0))],
        out_specs=pl.BlockSpec((MOE_ROW_TILE, d), lambda i, j, be, nh, tk: (i, 0)),
        scratch_shapes=[pltpu.VMEM((nj * MOE_GATHER_ROWS, d), F32),
                        pltpu.VMEM((MOE_ROW_TILE, d), BF16),
                        pltpu.SemaphoreType.DMA(())],
    )
    return pl.pallas_call(
        _moe_kernel,
        grid_spec=grid_spec,
        out_shape=jax.ShapeDtypeStruct((p_rows, d), F32),
        compiler_params=pltpu.CompilerParams(dimension_semantics=("arbitrary", "arbitrary"),
                                             vmem_limit_bytes=MOE_VMEM_LIMIT),
        name="moe_experts",
    )(blk_e, n_half, buf_tok, h, w_gu, w_gu, w_down)


def _combine_ln_kernel(res_ref, y0_ref, y1_ref, gate_ref, g_ref, b_ref, o_ref, ob_ref):
    gate = gate_ref[...]
    moe = y0_ref[...] * gate[:, 2:3] + y1_ref[...] * gate[:, 3:4]
    o = _layer_norm_rows(DN_ALPHA * res_ref[...] + moe, g_ref[...], b_ref[...])
    o_ref[...] = o
    ob_ref[...] = o.astype(BF16)


def combine_ln(res, y0, y1, route, g, b, out_rows):
    d = res.shape[1]
    row_spec = pl.BlockSpec((ROW_TILE, d), lambda i: (i, 0))
    vec_spec = pl.BlockSpec((1, d), lambda i: (0, 0))
    return pl.pallas_call(
        _combine_ln_kernel,
        grid=(pl.cdiv(out_rows, ROW_TILE),),
        in_specs=[row_spec, row_spec, row_spec, pl.BlockSpec((ROW_TILE, LANES), lambda i: (i, 0)),
                  vec_spec, vec_spec],
        out_specs=[row_spec, row_spec],
        out_shape=[jax.ShapeDtypeStruct((out_rows, d), F32),
                   jax.ShapeDtypeStruct((out_rows, d), BF16)],
        compiler_params=_params("parallel"),
        name="combine_ln",
    )(res, y0, y1, route, g, b)


def moe_res_ln(h, hb, w_router_pad, w_gu, w_down, layer, g, b, out_rows):
    rows, d = h.shape
    route = router(hb, w_router_pad)
    flat_e = route[:, :TOP_K].astype(jnp.int32).reshape(-1)
    n_flat = rows * TOP_K
    n_blk = -(-n_flat // MOE_ROW_TILE) + N_EXPERTS
    p_rows = n_blk * MOE_ROW_TILE
    onehot = (flat_e[:, None] == jnp.arange(N_EXPERTS, dtype=jnp.int32)[None, :]).astype(jnp.int32)
    csum = jnp.cumsum(onehot, axis=0)
    rank = jnp.sum(csum * onehot, axis=1) - 1
    counts = csum[-1]
    padded = ((counts + MOE_ROW_TILE - 1) // MOE_ROW_TILE) * MOE_ROW_TILE
    padded_end = jnp.cumsum(padded)
    start_padded = padded_end - padded
    pos = start_padded[flat_e] + rank
    flat_tok = jnp.arange(n_flat, dtype=jnp.int32) // TOP_K
    buf_tok = jnp.zeros((p_rows,), jnp.int32).at[pos].set(flat_tok)
    blk_start = jnp.arange(n_blk, dtype=jnp.int32) * MOE_ROW_TILE
    blk_e = jnp.minimum(jnp.searchsorted(padded_end, blk_start, side='right'),
                        N_EXPERTS - 1).astype(jnp.int32)
    valid = jnp.clip(counts[blk_e] - (blk_start - start_padded[blk_e]), 0, MOE_ROW_TILE)
    valid = jnp.where(blk_start < padded_end[-1], valid, 0)
    half = MOE_ROW_TILE // 2
    n_half = ((valid + half - 1) // half).astype(jnp.int32)
    y = moe_experts(h, buf_tok, blk_e, n_half, w_gu, w_down, layer)
    pos2 = pos.reshape(rows, TOP_K)
    y0 = y.at[pos2[:, 0]].get(mode="promise_in_bounds")
    y1 = y.at[pos2[:, 1]].get(mode="promise_in_bounds")
    return combine_ln(h, y0, y1, route, g, b, out_rows)


def _head_norm_gate(o, r, gain):
    mu = jnp.mean(o, axis=-1, keepdims=True)
    oc = o - mu
    var = jnp.mean(oc * oc, axis=-1, keepdims=True)
    on = oc * lax.rsqrt(var + LN_EPS) * gain
    return (on * (r * jax.nn.sigmoid(r))).astype(BF16)


def _masked_cumsum(mask, x):
    m = mask.astype(BF16)
    hi = x.astype(BF16)
    r1 = x - hi.astype(F32)
    mid = r1.astype(BF16)
    lo = (r1 - mid.astype(F32)).astype(BF16)
    return _dot(m, hi) + _dot(m, mid) + _dot(m, lo)


def _gla_first_chunk(q, k, v, r, la, gain):
    c = q.shape[0]
    row = lax.broadcasted_iota(jnp.int32, (c, c), 0)
    col = lax.broadcasted_iota(jnp.int32, (c, c), 1)
    causal = row >= col
    bcum = _masked_cumsum(causal, la)
    q_dec = (q * (GLA_DK_HEAD ** -0.5) * jnp.exp(bcum)).astype(BF16)
    k_inv = (k * jnp.exp(-bcum)).astype(BF16)
    vb = v.astype(BF16)
    att = jnp.where(causal, _dot_nt(q_dec, k_inv), 0.0)
    o = _dot(att.astype(BF16), vb)
    k_state = (k * jnp.exp(bcum[c - 1:c, :] - bcum)).astype(BF16)
    return _head_norm_gate(o, r, gain), _dot_tn(k_state, vb)


def _gla_intra(q, k, v, la):
    n = q.shape[0]
    c = GLA_CHUNK
    row = lax.broadcasted_iota(jnp.int32, (n, n), 0)
    col = lax.broadcasted_iota(jnp.int32, (n, n), 1)
    causal = (row >= col) & ((row // c) == (col // c))
    bcum = _masked_cumsum(causal, la)
    q_dec = (q * (GLA_DK_HEAD ** -0.5) * jnp.exp(bcum)).astype(BF16)
    k_inv = (k * jnp.exp(-bcum)).astype(BF16)
    vb = v.astype(BF16)
    att = jnp.where(causal, _dot_nt(q_dec, k_inv), 0.0)
    o_intra = _dot(att.astype(BF16), vb)
    b_last = [bcum[(ci + 1) * c - 1:(ci + 1) * c, :] for ci in range(n // c)]
    b_tot = jnp.concatenate([jnp.broadcast_to(bl, (c, bl.shape[1])) for bl in b_last], axis=0)
    k_state = (k * jnp.exp(b_tot - bcum)).astype(BF16)
    return q_dec, k_state, vb, o_intra, b_last


def _gla_walk(state, q_dec, k_state, vb, b_last):
    c = GLA_CHUNK
    o_inter = []
    for ci, bl in enumerate(b_last):
        sl = slice(ci * c, (ci + 1) * c)
        o_inter.append(_dot(q_dec[sl], state.astype(BF16)))
        decay = jnp.exp(jnp.transpose(jnp.broadcast_to(bl, (LANES, bl.shape[1]))))
        decay = jnp.concatenate([decay] * (GLA_DV_HEAD // LANES), axis=1)
        state = state * decay + _dot_tn(k_state[sl], vb[sl])
    return jnp.concatenate(o_inter, axis=0), state


def _gla_kernel(qm_ref, km_ref, vm_ref, rm_ref, lam_ref, q_ref, k_ref, v_ref, r_ref, la_ref,
                gain_ref, o_ref, state_ref, *, blocks_per_seq):
    s = pl.program_id(1)
    gain = gain_ref[...]
    n_batch = state_ref.shape[0]

    @pl.when(s == 0)
    def _():
        for bi in range(n_batch):
            sl = slice(bi * N_META, (bi + 1) * N_META)
            og, st = _gla_first_chunk(qm_ref[sl, :], km_ref[sl, :], vm_ref[sl, :], rm_ref[sl, :],
                                      lam_ref[sl, :], gain)
            state_ref[bi] = st
            o_ref[sl, :] = og

    @pl.when(s > 0)
    def _():
        bi = (s - 1) // blocks_per_seq
        subs = [slice(r0, r0 + GLA_SUB_ROWS) for r0 in range(0, q_ref.shape[0], GLA_SUB_ROWS)]
        intra = [_gla_intra(q_ref[sl, :], k_ref[sl, :], v_ref[sl, :], la_ref[sl, :]) for sl in subs]
        st = state_ref[bi]
        for sl, (q_dec, k_state, vb, o_intra, b_last) in zip(subs, intra):
            o_inter, st = _gla_walk(st, q_dec, k_state, vb, b_last)
            o_ref[sl, :] = _head_norm_gate(o_intra + o_inter, r_ref[sl, :], gain)
        state_ref[bi] = st


def gla_mix(qkvr, log_a, norm_gain, n_batch, seq):
    rows = qkvr.shape[0]
    n_real = n_batch * seq
    meta_rows = n_batch * N_META
    blocks_per_seq = seq // GLA_ROWS
    n_steps = 1 + n_real // GLA_ROWS
    meta_blk = n_real // meta_rows
    meta_out_blk = n_real // GLA_ROWS
    dk, dv, nh = GLA_DK_HEAD, GLA_DV_HEAD, GLA_HEADS
    kq, kk, kv_, kr = 0, GLA_DK // dk, 2 * GLA_DK // dv, (2 * GLA_DK + GLA_DV) // dv

    def real_blk(s):
        return jnp.maximum(s - 1, 0)

    def meta_spec(width, col0):
        return pl.BlockSpec((meta_rows, width), lambda h, s: (meta_blk, col0 + h))

    def real_spec(width, col0):
        return pl.BlockSpec((GLA_ROWS, width), lambda h, s: (real_blk(s), col0 + h))

    return pl.pallas_call(
        functools.partial(_gla_kernel, blocks_per_seq=blocks_per_seq),
        grid=(nh, n_steps),
        in_specs=[meta_spec(dk, kq), meta_spec(dk, kk), meta_spec(dv, kv_), meta_spec(dv, kr),
                  meta_spec(dk, 0),
                  real_spec(dk, kq), real_spec(dk, kk), real_spec(dv, kv_), real_spec(dv, kr),
                  real_spec(dk, 0),
                  pl.BlockSpec((1, dv), lambda h, s: (0, h))],
        out_specs=pl.BlockSpec((GLA_ROWS, dv),
                               lambda h, s: (jnp.where(s == 0, meta_out_blk, s - 1), h)),
        out_shape=jax.ShapeDtypeStruct((rows, GLA_DV), BF16),
        scratch_shapes=[pltpu.VMEM((n_batch, dk, dv), F32)],
        compiler_params=_params("parallel", "arbitrary"),
        name="gla_mix",
    )(qkvr, qkvr, qkvr, qkvr, log_a, qkvr, qkvr, qkvr, qkvr, log_a, norm_gain)


def _block_diag2(x2):
    lane = lax.broadcasted_iota(jnp.int32, x2.shape, 1)
    zero = jnp.zeros_like(x2)
    return jnp.concatenate([jnp.where(lane < SWA_HEAD_DIM, x2, zero),
                            jnp.where(lane >= SWA_HEAD_DIM, x2, zero)], axis=0)


def _swa_kernel(q_ref, km_ref, kp_ref, kc_ref, vm_ref, vp_ref, vc_ref,
                bm_ref, bp_ref, bc_ref, bm1_ref, bp1_ref, bc1_ref, o_ref, *, blocks_per_seq):
    s = pl.program_id(1)
    bi = (SWA_STEP_BLOCKS * jnp.maximum(s - 1, 0)) // blocks_per_seq
    m0 = pl.multiple_of(bi * N_META, N_META)
    pad = jnp.zeros((SWA_BLOCK - N_META, LANES), BF16)
    blk_a, blk_b = slice(0, SWA_BLOCK), slice(SWA_BLOCK, 2 * SWA_BLOCK)
    k_m = _block_diag2(jnp.concatenate([km_ref[pl.ds(m0, N_META), :], pad], axis=0))
    v_m = _block_diag2(jnp.concatenate([vm_ref[pl.ds(m0, N_META), :], pad], axis=0))
    k_p, v_p = _block_diag2(kp_ref[...]), _block_diag2(vp_ref[...])
    k_a, v_a = _block_diag2(kc_ref[blk_a, :]), _block_diag2(vc_ref[blk_a, :])
    k_b, v_b = _block_diag2(kc_ref[blk_b, :]), _block_diag2(vc_ref[blk_b, :])
    blocks = [(blk_a, (k_m, k_p, k_a), (v_m, v_p, v_a), (bm_ref, bp_ref, bc_ref)),
              (blk_b, (k_m, k_a, k_b), (v_m, v_a, v_b), (bm1_ref, bp1_ref, bc1_ref))]
    groups = [(qrows, p, ks, vs, bs) for qrows, ks, vs, bs in blocks for p in range(SWA_PAIRS)]
    scores = []
    for qrows, p, ks, _, bs in groups:
        qp = q_ref[qrows, p * LANES:(p + 1) * LANES]
        brows = slice(p * SWA_BLOCK, (p + 1) * SWA_BLOCK)
        scores.append([_dot_nt(qp, ks[t]) + bs[t][0, 0, brows, :] for t in range(3)])
    for (qrows, p, _, vs, _), sc in zip(groups, scores):
        top = jnp.maximum(jnp.maximum(sc[0], sc[1]), sc[2])
        m_even = jnp.max(top[:, :LANES], axis=-1, keepdims=True)
        m_odd = jnp.max(top[:, LANES:], axis=-1, keepdims=True)
        lane2 = lax.broadcasted_iota(jnp.int32, top.shape, 1)
        m_full = jnp.where(lane2 < LANES, m_even, m_odd)
        pr = [jnp.exp(sc[t] - m_full) for t in range(3)]
        p_sum = pr[0] + pr[1] + pr[2]
        l_even = jnp.sum(p_sum[:, :LANES], axis=-1, keepdims=True)
        l_odd = jnp.sum(p_sum[:, LANES:], axis=-1, keepdims=True)
        o = (_dot(pr[0].astype(BF16), vs[0]) + _dot(pr[1].astype(BF16), vs[1])
             + _dot(pr[2].astype(BF16), vs[2]))
        lane1 = lax.broadcasted_iota(jnp.int32, o.shape, 1)
        o = o / jnp.where(lane1 < SWA_HEAD_DIM, l_even, l_odd)
        o_ref[qrows, p * LANES:(p + 1) * LANES] = o.astype(o_ref.dtype)


def swa_mix(q, kv2, bias_m, bias_p, bias_c, n_batch, seq):
    rows = q.shape[0]
    nb = SWA_STEP_BLOCKS
    blocks_per_seq = seq // SWA_BLOCK
    assert blocks_per_seq % nb == 0 and n_batch * N_META == SWA_BLOCK
    n_real_blk = n_batch * blocks_per_seq
    n_steps = 1 + n_real_blk // nb
    meta_blk = n_real_blk
    kvn, qw = SWA_KV_HEADS, SWA_GROUP * SWA_HEAD_DIM

    def cur_step_blk(s):
        return jnp.where(s == 0, meta_blk // nb, s - 1)

    def prev_blk(s):
        return jnp.maximum(nb * (s - 1) - 1, 0)

    def variant(s):
        return jnp.where(s == 0, 2, jnp.where((nb * (s - 1)) % blocks_per_seq == 0, 0, 1))

    def kv_spec(n_blocks, head0, blk_fn):
        return pl.BlockSpec((n_blocks * SWA_BLOCK, LANES), lambda kh, s: (blk_fn(s), head0 + kh))

    bias_shape = (1, 1, SWA_PAIRS * SWA_BLOCK, 2 * SWA_BLOCK)
    bias_spec = pl.BlockSpec(bias_shape, lambda kh, s: (variant(s), kh, 0, 0))
    bias1_spec = pl.BlockSpec(bias_shape, lambda kh, s: (1, kh, 0, 0))
    return pl.pallas_call(
        functools.partial(_swa_kernel, blocks_per_seq=blocks_per_seq),
        grid=(kvn, n_steps),
        in_specs=[pl.BlockSpec((nb * SWA_BLOCK, qw), lambda kh, s: (cur_step_blk(s), kh)),
                  kv_spec(1, 0, lambda s: meta_blk), kv_spec(1, 0, prev_blk),
                  kv_spec(nb, 0, cur_step_blk),
                  kv_spec(1, kvn, lambda s: meta_blk), kv_spec(1, kvn, prev_blk),
                  kv_spec(nb, kvn, cur_step_blk),
                  bias_spec, bias_spec, bias_spec, bias1_spec, bias1_spec, bias1_spec],
        out_specs=pl.BlockSpec((nb * SWA_BLOCK, qw), lambda kh, s: (cur_step_blk(s), kh)),
        out_shape=jax.ShapeDtypeStruct((rows, kvn * qw), BF16),
        compiler_params=_params("parallel", "arbitrary"),
        name="swa_mix",
    )(q, kv2, kv2, kv2, kv2, kv2, kv2, bias_m, bias_p, bias_c, bias_m, bias_p, bias_c)


def _t5_bucket(dist):
    exact = REL_BUCKETS // 2
    d = jnp.maximum(dist, 0)
    df = jnp.maximum(d, 1).astype(F32)
    large = exact + (jnp.log(df / exact) / math.log(REL_MAX_DIST / exact)
                     * (REL_BUCKETS - exact)).astype(jnp.int32)
    large = jnp.minimum(large, REL_BUCKETS - 1)
    return jnp.where(d < exact, d, large)


def _pair_layout(bias):
    nv = bias.shape[0]
    b = bias.reshape(nv, SWA_KV_HEADS, SWA_PAIRS, 2, SWA_BLOCK, SWA_BLOCK)
    b = jnp.transpose(b, (0, 1, 2, 4, 3, 5))
    return b.reshape(nv, SWA_KV_HEADS, SWA_PAIRS * SWA_BLOCK, 2 * SWA_BLOCK)


def _swa_bias_static(table):
    tab = table.astype(F32)
    blk = SWA_BLOCK
    i = jnp.arange(blk, dtype=jnp.int32)[:, None]
    j = jnp.arange(blk, dtype=jnp.int32)[None, :]
    m = jnp.arange(N_META, dtype=jnp.int32)[None, :]

    def lookup(dist):
        onehot = (_t5_bucket(dist)[..., None] == jnp.arange(REL_BUCKETS, dtype=jnp.int32)).astype(F32)
        return jnp.einsum('qsb,bh->hqs', onehot, tab, precision=lax.Precision.HIGHEST)

    neg = jnp.full((SWA_Q_HEADS, blk, blk), NEG_INF, F32)
    cur = jnp.where((i - j >= 0)[None], lookup(i - j), NEG_INF)
    d_prev = blk + i - j
    prev = jnp.where(((d_prev >= 0) & (d_prev < SWA_WINDOW))[None], lookup(d_prev), NEG_INF)
    meta0 = lookup(N_META + i - m)
    meta1 = lookup(N_META + blk + i - m)
    same_seq = (i // N_META) == (j // N_META)
    dm = (i % N_META) - (j % N_META)
    meta_tile = jnp.where((same_seq & (dm >= 0))[None], lookup(dm), NEG_INF)
    bias_p = _pair_layout(jnp.stack([neg, prev, neg]))
    bias_c = _pair_layout(jnp.stack([cur, cur, meta_tile]))
    return bias_p, bias_c, jnp.stack([meta0, meta1])


def _swa_bias_meta(meta01, sinks):
    nh, blk = SWA_Q_HEADS, SWA_BLOCK
    sink_col = jnp.broadcast_to(sinks.astype(F32)[None, :, None, None], (3, nh, blk, 1))
    meta = jnp.concatenate([meta01, jnp.full((1, nh, blk, N_META), NEG_INF, F32)], axis=0)
    rest = jnp.full((3, nh, blk, blk - N_META - 1), NEG_INF, F32)
    return _pair_layout(jnp.concatenate([meta, sink_col, rest], axis=-1))


def kernel(x, meta_tokens, rel_bias_table, ln_gain, ln_bias, gla_w_in, gla_w_gate2, gla_b_gate,
           gla_norm_gain, gla_w_out, kv_w_shared, swa_w_q, swa_sinks, swa_w_out,
           ffn_w_gate_up, ffn_w_down, moe_w_router, moe_w_gate_up, moe_w_down):
    bsz, seq, d = x.shape
    n_real = bsz * seq
    h = jnp.concatenate([x.reshape(n_real, d),
                         jnp.broadcast_to(meta_tokens.astype(x.dtype)[None], (bsz, N_META, d))
                         .reshape(bsz * N_META, d)], axis=0)
    rows = h.shape[0]
    hb = h.astype(BF16)
    bias_p, bias_c, meta01 = _swa_bias_static(rel_bias_table)

    w_in_b = gla_w_in.astype(BF16)
    w_gla_out_b = gla_w_out.astype(BF16)
    w_q_b = swa_w_q.astype(BF16)
    w_swa_out_b = swa_w_out.astype(BF16)
    w_ffn_gu_b = ffn_w_gate_up.astype(BF16)
    w_ffn_down_b = ffn_w_down.astype(BF16)
    hd = SWA_HEAD_DIM
    w_kv = kv_w_shared.reshape(d, 2 * SWA_KV_HEADS, 1, hd)
    w_kv2 = jnp.broadcast_to(w_kv, (d, 2 * SWA_KV_HEADS, 2, hd)).reshape(1, d, 4 * SWA_KV_HEADS * hd)
    w_kv2 = w_kv2.astype(BF16)

    kv2 = None
    for li in range(DEPTH):
        g0, b0 = ln_gain[li, 0][None, :], ln_bias[li, 0][None, :]
        g1, b1 = ln_gain[li, 1][None, :], ln_bias[li, 1][None, :]
        if li < N_A_LAYERS:
            wg_pad = jnp.pad(w_in_b[li, :, GLA_MAIN:], ((0, 0), (0, LANES - GLA_GATE_RANK)))
            w2_pad = jnp.pad(gla_w_gate2[li].astype(BF16), ((0, LANES - GLA_GATE_RANK), (0, 0)))
            qkvr, log_a = gla_in_proj(hb, w_in_b, li, wg_pad, w2_pad, gla_b_gate[li][None, :])
            mix_in = gla_mix(qkvr, log_a, gla_norm_gain[li][None, :], bsz, seq)
            h, hb = linear_res_ln(mix_in, w_gla_out_b, li, h, g0, b0)
        else:
            jb = li - N_A_LAYERS
            q = linear(hb, w_q_b, jb, d, BF16, scale=hd ** -0.5)
            bias_m = _swa_bias_meta(meta01, swa_sinks[jb])
            mix_in = swa_mix(q, kv2, bias_m, bias_p, bias_c, bsz, seq)
            h, hb = linear_res_ln(mix_in, w_swa_out_b, jb, h, g0, b0)
        if li % 2 == 0:
            h, hb = ffn_res_ln(hb, h, w_ffn_gu_b, w_ffn_down_b, li // 2, g1, b1)
        else:
            w_r = jnp.pad(moe_w_router[li // 2].astype(BF16), ((0, 0), (0, LANES - N_EXPERTS)))
            out_rows = n_real if li == DEPTH - 1 else rows
            h, hb = moe_res_ln(h, hb, w_r, moe_w_gate_up, moe_w_down, li // 2, g1, b1, out_rows)
        if li == N_A_LAYERS - 1:
            kv2 = linear(hb, w_kv2, 0, w_kv2.shape[-1], BF16)
    return h.reshape(bsz, seq, d)
```
